```python
import jax, jax.numpy as jnp
from jax import lax
import numpy as np

D_MODEL = 2048
BATCH = 8
SEQ = 2048
DEPTH = 2

A_CHUNK = 128
A_HALF = D_MODEL
A_GROUPS = A_HALF // 128
A_GROUP_DIM = A_HALF // A_GROUPS
HEAD_DIM = 64
N_Q_HEADS = D_MODEL // HEAD_DIM
N_KV_HEADS = N_Q_HEADS // 8
Q_PER_KV = N_Q_HEADS // N_KV_HEADS
QKV_DIM = (N_Q_HEADS + 2 * N_KV_HEADS) * HEAD_DIM
WINDOW = 128
ATTN_BLOCK = 128
N_EXPERTS = 64
TOP_K = 8
N_GROUPS = 8
TOPK_GROUPS = 4
EXPERT_FF = D_MODEL // 4
SHARED_FF = EXPERT_FF
ROUTED_SCALE = 2.5
MOE_BLOCK = 128
DN_ALPHA = (2 * DEPTH) ** 0.25
DN_BETA = (8 * DEPTH) ** -0.25
LN_EPS = 1e-5
N_A_LAYERS = (DEPTH + 1) // 2
N_B_LAYERS = DEPTH // 2

kernel_name = 'hybrid_gmlp_swa_sinks_moe_deepnorm'


def layer_norm(x, g, b):
    xf = x.astype(jnp.float32)
    mu = jnp.mean(xf, axis=-1, keepdims=True)
    var = jnp.mean(jnp.square(xf - mu), axis=-1, keepdims=True)
    y = (xf - mu) * lax.rsqrt(var + LN_EPS)
    return (y * g.astype(jnp.float32) + b.astype(jnp.float32)).astype(x.dtype)


def chunked_gmlp(x, w_in, b_in, ln_g, ln_b, w_s, b_s, w_out):
    bsz, s_len, _ = x.shape
    z = jax.nn.gelu(x @ w_in + b_in, approximate=False)
    u, v = z[..., :A_HALF], z[..., A_HALF:]
    v = layer_norm(v, ln_g, ln_b)
    nc = s_len // A_CHUNK
    v = v.reshape(bsz, nc, A_CHUNK, A_GROUPS, A_GROUP_DIM)
    causal = jnp.tril(jnp.ones((A_CHUNK, A_CHUNK), dtype=bool))
    w_mix = jnp.where(causal[None], w_s, 0.0).astype(v.dtype)
    mixed = jnp.einsum('gts,bnsgc->bntgc', w_mix, v) + jnp.transpose(b_s)[None, None, :, :, None]
    gated = u * mixed.reshape(bsz, s_len, A_HALF)
    return gated @ w_out


def swa_sinks_attention(x, w_qkv, b_qkv, sinks, w_o, b_o):
    bsz, s_len, _ = x.shape
    nb = s_len // ATTN_BLOCK
    qkv = x @ w_qkv + b_qkv
    q_end = N_Q_HEADS * HEAD_DIM
    k_end = q_end + N_KV_HEADS * HEAD_DIM
    q = qkv[..., :q_end].reshape(bsz, nb, ATTN_BLOCK, N_KV_HEADS, Q_PER_KV, HEAD_DIM)
    k = qkv[..., q_end:k_end].reshape(bsz, nb, ATTN_BLOCK, N_KV_HEADS, HEAD_DIM)
    v = qkv[..., k_end:].reshape(bsz, nb, ATTN_BLOCK, N_KV_HEADS, HEAD_DIM)

    def with_prev(t):
        prev = jnp.concatenate([jnp.zeros_like(t[:, :1]), t[:, :-1]], axis=1)
        return jnp.concatenate([prev, t], axis=2)

    kk, vv = with_prev(k), with_prev(v)
    scores = jnp.einsum('bnqkgd,bnskd->bnkgqs', q, kk).astype(jnp.float32) * (HEAD_DIM ** -0.5)

    qi = jnp.arange(ATTN_BLOCK)[:, None]
    kj = jnp.arange(2 * ATTN_BLOCK)[None, :]
    dist = qi + ATTN_BLOCK - kj
    key_abs = jnp.arange(nb)[:, None, None] * ATTN_BLOCK + kj[None] - ATTN_BLOCK
    valid = (dist >= 0)[None] & (dist < WINDOW)[None] & (key_abs >= 0)

    head = jnp.arange(1, N_Q_HEADS + 1, dtype=jnp.float32)
    slopes = jnp.exp2(-8.0 * head / N_Q_HEADS).reshape(N_KV_HEADS, Q_PER_KV)
    alibi = -slopes[:, :, None, None] * dist.astype(jnp.float32)[None, None]
    scores = jnp.where(valid[None, :, None, None], scores + alibi[None, None], -jnp.inf)

    sink = sinks.astype(jnp.float32).reshape(N_KV_HEADS, Q_PER_KV)[None, None, :, :, None]
    m = jnp.maximum(jnp.max(scores, axis=-1), sink)
    e = jnp.exp(scores - m[..., None])
    denom = jnp.sum(e, axis=-1) + jnp.exp(sink - m)
    p = (e / denom[..., None]).astype(vv.dtype)
    o = jnp.einsum('bnkgqs,bnskd->bnqkgd', p, vv).reshape(bsz, s_len, N_Q_HEADS * HEAD_DIM)
    return o @ w_o + b_o


def moe_ffn(x, w_router, router_bias, w_gate, w_up, w_down, ws_gate, ws_up, ws_down):
    bsz, s_len, d = x.shape
    n_tok = bsz * s_len
    x2 = x.reshape(n_tok, d)
    scores = jax.nn.sigmoid(x2.astype(jnp.float32) @ w_router.astype(jnp.float32))
    sel = scores + router_bias.astype(jnp.float32)
    grp = sel.reshape(n_tok, N_GROUPS, N_EXPERTS // N_GROUPS)
    grp_score = jnp.sum(lax.top_k(grp, 2)[0], axis=-1)
    gidx = lax.top_k(grp_score, TOPK_GROUPS)[1]
    gmask = jnp.any(gidx[..., None] == jnp.arange(N_GROUPS), axis=-2)
    emask = jnp.repeat(gmask, N_EXPERTS // N_GROUPS, axis=-1)
    idx = lax.top_k(jnp.where(emask, sel, -jnp.inf), TOP_K)[1]
    wts = jnp.take_along_axis(scores, idx, axis=-1)
    wts = wts / (jnp.sum(wts, axis=-1, keepdims=True) + 1e-20) * ROUTED_SCALE

    n_assign = n_tok * TOP_K
    flat_e = idx.reshape(-1)
    flat_w = wts.reshape(-1)
    order = jnp.argsort(flat_e)
    sorted_e = flat_e[order]
    counts = jnp.bincount(flat_e, length=N_EXPERTS)
    padded = (counts + MOE_BLOCK - 1) // MOE_BLOCK * MOE_BLOCK
    pad_end = jnp.cumsum(padded)
    pad_start = pad_end - padded
    start = jnp.cumsum(counts) - counts
    dest = pad_start[sorted_e] + jnp.arange(n_assign) - start[sorted_e]
    n_blocks = n_assign // MOE_BLOCK + N_EXPERTS
    n_slots = n_blocks * MOE_BLOCK
    slot_tok = jnp.zeros((n_slots,), jnp.int32).at[dest].set((order // TOP_K).astype(jnp.int32))
    slot_w = jnp.zeros((n_slots,), jnp.float32).at[dest].set(flat_w[order])
    block_e = jnp.minimum(jnp.searchsorted(pad_end, jnp.arange(n_blocks) * MOE_BLOCK, side='right'),
                          N_EXPERTS - 1)

    def expert_block(args):
        tok, e, gw = args
        xb = x2[tok]
        h = jax.nn.silu(xb @ w_gate[e]) * (xb @ w_up[e])
        return (h @ w_down[e]) * gw.astype(xb.dtype)[:, None]

    y = lax.map(expert_block, (slot_tok.reshape(n_blocks, MOE_BLOCK), block_e,
                               slot_w.reshape(n_blocks, MOE_BLOCK)))
    routed = jax.ops.segment_sum(y.reshape(n_slots, d), slot_tok, num_segments=n_tok)
    shared = (jax.nn.silu(x2 @ ws_gate) * (x2 @ ws_up)) @ ws_down
    return (routed + shared).reshape(bsz, s_len, d)


def setup_inputs(seed: int = 0) -> dict:
    key = jax.random.key(seed)
    ks = jax.random.split(key, 24)
    f32 = jnp.float32
    nrm = lambda k, shape, scale: jax.random.normal(k, shape, f32) * scale
    d = D_MODEL
    x = jax.random.normal(ks[0], (BATCH, SEQ, d), f32)
    a_w_in = nrm(ks[1], (N_A_LAYERS, d, 2 * A_HALF), d ** -0.5)
    a_b_in = nrm(ks[2], (N_A_LAYERS, 2 * A_HALF), 0.01)
    a_ln_g = 1.0 + nrm(ks[3], (N_A_LAYERS, A_HALF), 0.01)
    a_ln_b = nrm(ks[4], (N_A_LAYERS, A_HALF), 0.01)
    a_w_s = nrm(ks[5], (N_A_LAYERS, A_GROUPS, A_CHUNK, A_CHUNK), A_CHUNK ** -0.5)
    a_b_s = 1.0 + nrm(ks[6], (N_A_LAYERS, A_GROUPS, A_CHUNK), 0.01)
    a_w_out = nrm(ks[7], (N_A_LAYERS, A_HALF, d), A_HALF ** -0.5 * DN_BETA)
    v_start = (N_Q_HEADS + N_KV_HEADS) * HEAD_DIM
    b_w_qkv = nrm(ks[8], (N_B_LAYERS, d, QKV_DIM), d ** -0.5)
    b_w_qkv = b_w_qkv.at[:, :, v_start:].multiply(DN_BETA)
    b_b_qkv = nrm(ks[9], (N_B_LAYERS, QKV_DIM), 0.01)
    b_sinks = nrm(ks[10], (N_B_LAYERS, N_Q_HEADS), 0.5)
    b_w_o = nrm(ks[11], (N_B_LAYERS, N_Q_HEADS * HEAD_DIM, d), (N_Q_HEADS * HEAD_DIM) ** -0.5 * DN_BETA)
    b_b_o = nrm(ks[12], (N_B_LAYERS, d), 0.01)
    moe_w_router = nrm(ks[13], (DEPTH, d, N_EXPERTS), d ** -0.5)
    moe_router_bias = nrm(ks[14], (DEPTH, N_EXPERTS), 0.01)
    moe_w_gate = nrm(ks[15], (DEPTH, N_EXPERTS, d, EXPERT_FF), d ** -0.5)
    moe_w_up = nrm(ks[16], (DEPTH, N_EXPERTS, d, EXPERT_FF), d ** -0.5)
    moe_w_down = nrm(ks[17], (DEPTH, N_EXPERTS, EXPERT_FF, d), EXPERT_FF ** -0.5 * DN_BETA)
    moe_ws_gate = nrm(ks[18], (DEPTH, d, SHARED_FF), d ** -0.5)
    moe_ws_up = nrm(ks[19], (DEPTH, d, SHARED_FF), d ** -0.5)
    moe_ws_down = nrm(ks[20], (DEPTH, SHARED_FF, d), SHARED_FF ** -0.5 * DN_BETA)
    norm_g = 1.0 + nrm(ks[21], (DEPTH, 2, d), 0.01)
    norm_b = nrm(ks[22], (DEPTH, 2, d), 0.01)
    return {'x': x,
            'a_w_in': a_w_in, 'a_b_in': a_b_in, 'a_ln_g': a_ln_g, 'a_ln_b': a_ln_b,
            'a_w_s': a_w_s, 'a_b_s': a_b_s, 'a_w_out': a_w_out,
            'b_w_qkv': b_w_qkv, 'b_b_qkv': b_b_qkv, 'b_sinks': b_sinks, 'b_w_o': b_w_o, 'b_b_o': b_b_o,
            'moe_w_router': moe_w_router, 'moe_router_bias': moe_router_bias,
            'moe_w_gate': moe_w_gate, 'moe_w_up': moe_w_up, 'moe_w_down': moe_w_down,
            'moe_ws_gate': moe_ws_gate, 'moe_ws_up': moe_ws_up, 'moe_ws_down': moe_ws_down,
            'norm_g': norm_g, 'norm_b': norm_b}


def reference(x, a_w_in, a_b_in, a_ln_g, a_ln_b, a_w_s, a_b_s, a_w_out,
              b_w_qkv, b_b_qkv, b_sinks, b_w_o, b_b_o,
              moe_w_router, moe_router_bias, moe_w_gate, moe_w_up, moe_w_down,
              moe_ws_gate, moe_ws_up, moe_ws_down, norm_g, norm_b):
    for i in range(DEPTH):
        j = i // 2
        if i % 2 == 0:
            mix = chunked_gmlp(x, a_w_in[j], a_b_in[j], a_ln_g[j], a_ln_b[j], a_w_s[j], a_b_s[j], a_w_out[j])
        else:
            mix = swa_sinks_attention(x, b_w_qkv[j], b_b_qkv[j], b_sinks[j], b_w_o[j], b_b_o[j])
        x = layer_norm(DN_ALPHA * x + mix, norm_g[i, 0], norm_b[i, 0])
        ffn = moe_ffn(x, moe_w_router[i], moe_router_bias[i], moe_w_gate[i], moe_w_up[i], moe_w_down[i],
                      moe_ws_gate[i], moe_ws_up[i], moe_ws_down[i])
        x = layer_norm(DN_ALPHA * x + ffn, norm_g[i, 1], norm_b[i, 1])
    return x
```

```python
import functools

import jax
import jax.numpy as jnp
from jax import lax
from jax.experimental import pallas as pl
from jax.experimental.pallas import tpu as pltpu

F32 = jnp.float32
BF16 = jnp.bfloat16

D_MODEL = 2048
DEPTH = 2
A_CHUNK = 128
A_HALF = D_MODEL
A_GROUPS = A_HALF // 128
HEAD_DIM = 64
N_Q_HEADS = D_MODEL // HEAD_DIM
N_KV_HEADS = N_Q_HEADS // 8
Q_PER_KV = N_Q_HEADS // N_KV_HEADS
Q_DIM = N_Q_HEADS * HEAD_DIM
KV_DIM = N_KV_HEADS * HEAD_DIM
QKV_DIM = Q_DIM + 2 * KV_DIM
WINDOW = 128
ATTN_BLOCK = 128
N_EXPERTS = 64
TOP_K = 8
N_GROUPS = 8
GROUP_SIZE = N_EXPERTS // N_GROUPS
TOPK_GROUPS = 4
EXPERT_FF = D_MODEL // 4
ROUTED_SCALE = 2.5
DN_ALPHA = (2 * DEPTH) ** 0.25
LN_EPS = 1e-5

MOE_ROWS = 256
VMEM_LIMIT = 56 * 1024 * 1024


def _params(*sem):
    return pltpu.CompilerParams(dimension_semantics=sem, vmem_limit_bytes=VMEM_LIMIT)


def _layer_norm(y, g, b):
    mu = jnp.mean(y, axis=-1, keepdims=True)
    d = y - mu
    var = jnp.mean(d * d, axis=-1, keepdims=True)
    return d * lax.rsqrt(var + LN_EPS) * g + b


def _gelu(x):
    return 0.5 * x * (1.0 + lax.erf(x * (2.0 ** -0.5)))


def _silu(x):
    return x * jax.nn.sigmoid(x)


def _linear_kernel(x_ref, w_ref, b_ref, o_ref, *, act):
    acc = jnp.dot(x_ref[...].astype(BF16), w_ref[...], preferred_element_type=F32)
    acc = acc + b_ref[...]
    if act == "gelu":
        acc = _gelu(acc)
    o_ref[...] = acc.astype(o_ref.dtype)


def _linear(x, w, b, *, act, tm, tn, out_dtype):
    m, k = x.shape
    n = w.shape[1]
    return pl.pallas_call(
        functools.partial(_linear_kernel, act=act),
        grid=(m // tm, n // tn),
        in_specs=[
            pl.BlockSpec((tm, k), lambda i, j: (i, 0)),
            pl.BlockSpec((k, tn), lambda i, j: (0, j)),
            pl.BlockSpec((1, tn), lambda i, j: (0, j)),
        ],
        out_specs=pl.BlockSpec((tm, tn), lambda i, j: (i, j)),
        out_shape=jax.ShapeDtypeStruct((m, n), out_dtype),
        name="linear_" + str(act),
        compiler_params=_params("parallel", "arbitrary"),
    )(x, w, b.reshape(1, n))


def _linear_res_ln_kernel(x_ref, w_ref, b_ref, res_ref, g_ref, beta_ref, o_ref):
    acc = jnp.dot(x_ref[...], w_ref[...], preferred_element_type=F32)
    y = DN_ALPHA * res_ref[...] + (acc + b_ref[...])
    o_ref[...] = _layer_norm(y, g_ref[...], beta_ref[...])


def _linear_res_ln(x, w, b, res, g, beta, *, tm):
    m, k = x.shape
    n = w.shape[1]
    row = lambda i: (i, 0)
    fixed = lambda i: (0, 0)
    return pl.pallas_call(
        _linear_res_ln_kernel,
        grid=(m // tm,),
        in_specs=[
            pl.BlockSpec((tm, k), row),
            pl.BlockSpec((k, n), fixed),
            pl.BlockSpec((1, n), fixed),
            pl.BlockSpec((tm, n), row),
            pl.BlockSpec((1, n), fixed),
            pl.BlockSpec((1, n), fixed),
        ],
        out_specs=pl.BlockSpec((tm, n), row),
        out_shape=jax.ShapeDtypeStruct((m, n), F32),
        name="linear_res_ln",
        compiler_params=_params("parallel"),
    )(x, w, b.reshape(1, n), res, g.reshape(1, n), beta.reshape(1, n))


def _gmlp_gate_kernel(z_ref, lng_ref, lnb_ref, wmix_ref, bst_ref, o_ref, *, tm):
    v = z_ref[:, A_HALF:].astype(F32)
    vn = _layer_norm(v, lng_ref[...], lnb_ref[...]).astype(BF16)
    for c in range(tm // A_CHUNK):
        rows = slice(c * A_CHUNK, (c + 1) * A_CHUNK)
        for g in range(A_GROUPS):
            cols = slice(g * 128, (g + 1) * 128)
            mixed = jnp.dot(wmix_ref[g], vn[rows, cols], preferred_element_type=F32)
            mixed = mixed + bst_ref[:, g:g + 1]
            u = z_ref[rows, cols].astype(F32)
            o_ref[rows, cols] = (u * mixed).astype(o_ref.dtype)


def _gmlp_gate(z, ln_g, ln_b, w_mix, b_s_t, *, tm):
    m = z.shape[0]
    return pl.pallas_call(
        functools.partial(_gmlp_gate_kernel, tm=tm),
        grid=(m // tm,),
        in_specs=[
            pl.BlockSpec((tm, 2 * A_HALF), lambda i: (i, 0)),
            pl.BlockSpec((1, A_HALF), lambda i: (0, 0)),
            pl.BlockSpec((1, A_HALF), lambda i: (0, 0)),
            pl.BlockSpec((A_GROUPS, A_CHUNK, A_CHUNK), lambda i: (0, 0, 0)),
            pl.BlockSpec((A_CHUNK, A_GROUPS), lambda i: (0, 0)),
        ],
        out_specs=pl.BlockSpec((tm, A_HALF), lambda i: (i, 0)),
        out_shape=jax.ShapeDtypeStruct((m, A_HALF), BF16),
        name="gmlp_gate",
        compiler_params=_params("parallel"),
    )(z, ln_g.reshape(1, A_HALF), ln_b.reshape(1, A_HALF), w_mix, b_s_t)


def _attn_kernel(sinks_ref, q_ref, kp_ref, kc_ref, vp_ref, vc_ref, o_ref, *, blocks_per_seq):
    n = pl.program_id(0) % blocks_per_seq
    qi = lax.broadcasted_iota(jnp.int32, (ATTN_BLOCK, 2 * ATTN_BLOCK), 0)
    kj = lax.broadcasted_iota(jnp.int32, (ATTN_BLOCK, 2 * ATTN_BLOCK), 1)
    dist = qi + ATTN_BLOCK - kj
    valid = (dist >= 0) & (dist < WINDOW) & ((kj >= ATTN_BLOCK) | (n > 0))
    dist_f = dist.astype(F32)
    scale = HEAD_DIM ** -0.5
    for g in range(N_KV_HEADS):
        kv_cols = slice(g * HEAD_DIM, (g + 1) * HEAD_DIM)
        kk = jnp.concatenate([kp_ref[:, kv_cols], kc_ref[:, kv_cols]], axis=0)
        vv = jnp.concatenate([vp_ref[:, kv_cols], vc_ref[:, kv_cols]], axis=0)
        heads = [g * Q_PER_KV + j for j in range(Q_PER_KV)]
        qg = jnp.concatenate([q_ref[:, h * HEAD_DIM:(h + 1) * HEAD_DIM] for h in heads], axis=0)
        s = lax.dot_general(qg, kk, (((1,), (1,)), ((), ())), preferred_element_type=F32)
        probs = []
        for j, h in enumerate(heads):
            slope = 2.0 ** (-8.0 * (h + 1) / N_Q_HEADS)
            sj = s[j * ATTN_BLOCK:(j + 1) * ATTN_BLOCK] * scale + (-slope) * dist_f
            sj = jnp.where(valid, sj, -jnp.inf)
            sink = sinks_ref[h]
            mx = jnp.maximum(jnp.max(sj, axis=-1, keepdims=True), sink)
            e = jnp.exp(sj - mx)
            den = jnp.sum(e, axis=-1, keepdims=True) + jnp.exp(sink - mx)
            probs.append((e / den).astype(BF16))
        p = jnp.concatenate(probs, axis=0)
        o = jnp.dot(p, vv, preferred_element_type=F32)
        for j, h in enumerate(heads):
            o_ref[:, h * HEAD_DIM:(h + 1) * HEAD_DIM] = (
                o[j * ATTN_BLOCK:(j + 1) * ATTN_BLOCK].astype(o_ref.dtype))


def _attention(qkv, sinks, *, seq_len):
    t = qkv.shape[0]
    blocks_per_seq = seq_len // ATTN_BLOCK
    kcol = Q_DIM // KV_DIM
    vcol = kcol + 1
    prev = lambda i: jnp.maximum(i - 1, 0)
    grid_spec = pltpu.PrefetchScalarGridSpec(
        num_scalar_prefetch=1,
        grid=(t // ATTN_BLOCK,),
        in_specs=[
            pl.BlockSpec((ATTN_BLOCK, Q_DIM), lambda i, s: (i, 0)),
            pl.BlockSpec((ATTN_BLOCK, KV_DIM), lambda i, s: (prev(i), kcol)),
            pl.BlockSpec((ATTN_BLOCK, KV_DIM), lambda i, s: (i, kcol)),
            pl.BlockSpec((ATTN_BLOCK, KV_DIM), lambda i, s: (prev(i), vcol)),
            pl.BlockSpec((ATTN_BLOCK, KV_DIM), lambda i, s: (i, vcol)),
        ],
        out_specs=pl.BlockSpec((ATTN_BLOCK, Q_DIM), lambda i, s: (i, 0)),
    )
    return pl.pallas_call(
        functools.partial(_attn_kernel, blocks_per_seq=blocks_per_seq),
        grid_spec=grid_spec,
        out_shape=jax.ShapeDtypeStruct((t, Q_DIM), BF16),
        name="swa_attention",
        compiler_params=_params("parallel"),
    )(sinks, qkv, qkv, qkv, qkv, qkv)


def _router_kernel(x_ref, wrt_ref, rb_ref, idx_ref, wts_ref, rank_ref, cnt_ref, carry_ref, *, tm):
    @pl.when(pl.program_id(0) == 0)
    def _():
        carry_ref[...] = jnp.zeros_like(carry_ref)

    logits = lax.dot_general(wrt_ref[...], x_ref[...], (((1,), (1,)), ((), ())),
                             precision=lax.Precision.HIGHEST, preferred_element_type=F32)
    sc = jax.nn.sigmoid(logits)
    sel = sc + rb_ref[...]
    neg = -jnp.inf
    iota_g = lax.broadcasted_iota(jnp.int32, (GROUP_SIZE, tm), 0).astype(F32)

    def first_argmax(v, iota, size):
        m = jnp.max(v, axis=0, keepdims=True)
        return jnp.min(jnp.where(v == m, iota, float(size)), axis=0, keepdims=True)

    group_rows = []
    for g in range(N_GROUPS):
        v = sel[g * GROUP_SIZE:(g + 1) * GROUP_SIZE]
        m1 = jnp.max(v, axis=0, keepdims=True)
        i1 = first_argmax(v, iota_g, GROUP_SIZE)
        m2 = jnp.max(jnp.where(iota_g == i1, neg, v), axis=0, keepdims=True)
        group_rows.append(m1 + m2)
    cur = jnp.concatenate(group_rows, axis=0)
    iota_n = lax.broadcasted_iota(jnp.int32, (N_GROUPS, tm), 0).astype(F32)
    gsel = jnp.zeros((N_GROUPS, tm), F32)
    for _ in range(TOPK_GROUPS):
        hit = iota_n == first_argmax(cur, iota_n, N_GROUPS)
        gsel = jnp.where(hit, 1.0, gsel)
        cur = jnp.where(hit, neg, cur)
    cur = jnp.concatenate(
        [jnp.where(gsel[g:g + 1] > 0.5, sel[g * GROUP_SIZE:(g + 1) * GROUP_SIZE], neg)
         for g in range(N_GROUPS)], axis=0)

    iota_e = lax.broadcasted_iota(jnp.int32, (N_EXPERTS, tm), 0).astype(F32)
    member = jnp.zeros((N_EXPERTS, tm), F32)
    hits, idx_rows, w_rows = [], [], []
    for _ in range(TOP_K):
        ii = first_argmax(cur, iota_e, N_EXPERTS)
        hit = iota_e == ii
        hits.append(hit)
        idx_rows.append(ii)
        w_rows.append(jnp.sum(jnp.where(hit, sc, 0.0), axis=0, keepdims=True))
        member = jnp.where(hit, 1.0, member)
        cur = jnp.where(hit, neg, cur)
    w = jnp.concatenate(w_rows, axis=0)
    w = w / (jnp.sum(w, axis=0, keepdims=True) + 1e-20) * ROUTED_SCALE

    r = lax.broadcasted_iota(jnp.int32, (tm, tm), 0)
    c = lax.broadcasted_iota(jnp.int32, (tm, tm), 1)
    upper = jnp.where(r < c, 1.0, 0.0).astype(BF16)
    before = jnp.dot(member.astype(BF16), upper, preferred_element_type=F32)
    rank_full = carry_ref[...] + before
    rank_rows = [jnp.sum(jnp.where(h, rank_full, 0.0), axis=0, keepdims=True) for h in hits]

    idx_ref[...] = jnp.concatenate(idx_rows, axis=0).astype(jnp.int32)
    wts_ref[...] = w
    rank_ref[...] = jnp.concatenate(rank_rows, axis=0).astype(jnp.int32)
    carry_ref[...] = carry_ref[...] + jnp.sum(member, axis=1, keepdims=True)
    cnt_ref[...] = jnp.broadcast_to(carry_ref[...], cnt_ref.shape)


def _router(x, w_router_t, router_bias, *, tm):
    t, d = x.shape
    tok = lambda i: (0, i)
    return pl.pallas_call(
        functools.partial(_router_kernel, tm=tm),
        grid=(t // tm,),
        in_specs=[
            pl.BlockSpec((tm, d), lambda i: (i, 0)),
            pl.BlockSpec((N_EXPERTS, d), lambda i: (0, 0)),
            pl.BlockSpec((N_EXPERTS, 1), lambda i: (0, 0)),
        ],
        out_specs=[
            pl.BlockSpec((TOP_K, tm), tok),
            pl.BlockSpec((TOP_K, tm), tok),
            pl.BlockSpec((TOP_K, tm), tok),
            pl.BlockSpec((N_EXPERTS, 128), lambda i: (0, 0)),
        ],
        out_shape=[
            jax.ShapeDtypeStruct((TOP_K, t), jnp.int32),
            jax.ShapeDtypeStruct((TOP_K, t), F32),
            jax.ShapeDtypeStruct((TOP_K, t), jnp.int32),
            jax.ShapeDtypeStruct((N_EXPERTS, 128), F32),
        ],
        scratch_shapes=[pltpu.VMEM((N_EXPERTS, 1), F32)],
        name="router",
        compiler_params=_params("arbitrary"),
    )(x, w_router_t, router_bias.reshape(N_EXPERTS, 1))


def _experts_kernel(be_ref, nu_ref, idx_hbm, x_hbm, wg_ref, wu_ref, wd_ref, y_hbm,
                    idx_smem, xbuf, ybuf, wg_bf, wu_bf, wd_bf, isem, gsem, ssem):
    b = pl.program_id(0)
    n_used = nu_ref[0]
    rows = MOE_ROWS

    def idx_copy(blk):
        slot = blk % 4
        return pltpu.make_async_copy(idx_hbm.at[blk], idx_smem.at[slot], isem.at[slot])

    def start_gathers(blk):
        islot = blk % 4
        xslot = blk % 2

        def body(i, carry):
            for u in range(8):
                r = i * 8 + u
                tok = idx_smem[islot, r]
                pltpu.make_async_copy(x_hbm.at[pl.ds(tok, 1)], xbuf.at[xslot, pl.ds(r, 1)],
                                      gsem.at[xslot]).start()
            return carry

        lax.fori_loop(0, rows // 8, body, 0)

    def start_scatters(blk):
        islot = blk % 4
        yslot = blk % 2

        def body(i, carry):
            for u in range(8):
                r = i * 8 + u
                dst = idx_smem[islot, rows + r]
                pltpu.make_async_copy(ybuf.at[yslot, pl.ds(r, 1)], y_hbm.at[pl.ds(dst, 1)],
                                      ssem.at[yslot]).start()
            return carry

        lax.fori_loop(0, rows // 8, body, 0)

    def wait_rows(buf, sem, slot):
        pltpu.make_async_copy(buf.at[slot], buf.at[slot], sem.at[slot]).wait()

    @pl.when(b == 0)
    def _():
        ybuf[...] = jnp.zeros_like(ybuf)
        spare0 = y_hbm.shape[0] - 2 * rows
        for s in range(2):
            fill = pltpu.make_async_copy(ybuf.at[s], y_hbm.at[pl.ds(spare0 + s * rows, rows)],
                                         ssem.at[s])
            fill.start()
            fill.wait()
        idx_copy(0).start()

        @pl.when(n_used > 1)
        def _():
            idx_copy(1).start()

        idx_copy(0).wait()
        start_gathers(0)

    @pl.when(b < n_used)
    def _():
        @pl.when(b + 1 < n_used)
        def _():
            idx_copy(b + 1).wait()
            start_gathers(b + 1)

        @pl.when(b + 2 < n_used)
        def _():
            idx_copy(b + 2).start()

        changed = jnp.logical_or(b == 0, be_ref[b] != be_ref[jnp.maximum(b - 1, 0)])

        @pl.when(changed)
        def _():
            wg_bf[...] = wg_ref[...].astype(BF16)
            wu_bf[...] = wu_ref[...].astype(BF16)
            wd_bf[...] = wd_ref[...].astype(BF16)

        slot = b % 2
        wait_rows(xbuf, gsem, slot)

        @pl.when(b >= 2)
        def _():
            wait_rows(ybuf, ssem, slot)

        x = xbuf[slot].astype(BF16)
        gate = jnp.dot(x, wg_bf[...], preferred_element_type=F32)
        up = jnp.dot(x, wu_bf[...], preferred_element_type=F32)
        h = (_silu(gate) * up).astype(BF16)
        ybuf[slot] = jnp.dot(h, wd_bf[...], preferred_element_type=F32)
        start_scatters(b)

        @pl.when(b == n_used - 1)
        def _():
            @pl.when(b >= 1)
            def _():
                wait_rows(ybuf, ssem, 1 - slot)

            wait_rows(ybuf, ssem, slot)


def _experts(x, layer, w_gate, w_up, w_down, block_expert, n_used, idx_pack, *, n_out_rows):
    t, d = x.shape
    n_blocks = idx_pack.shape[0]
    wspec_in = pl.BlockSpec((None, None, d, EXPERT_FF), lambda b, be, nu: (layer, be[b], 0, 0))
    wspec_out = pl.BlockSpec((None, None, EXPERT_FF, d), lambda b, be, nu: (layer, be[b], 0, 0))
    grid_spec = pltpu.PrefetchScalarGridSpec(
        num_scalar_prefetch=2,
        grid=(n_blocks,),
        in_specs=[
            pl.BlockSpec(memory_space=pl.ANY),
            pl.BlockSpec(memory_space=pl.ANY),
            wspec_in, wspec_in, wspec_out,
        ],
        out_specs=pl.BlockSpec(memory_space=pl.ANY),
        scratch_shapes=[
            pltpu.SMEM((4, 2 * MOE_ROWS), jnp.int32),
            pltpu.VMEM((2, MOE_ROWS, d), F32),
            pltpu.VMEM((2, MOE_ROWS, d), F32),
            pltpu.VMEM((d, EXPERT_FF), BF16),
            pltpu.VMEM((d, EXPERT_FF), BF16),
            pltpu.VMEM((EXPERT_FF, d), BF16),
            pltpu.SemaphoreType.DMA((4,)),
            pltpu.SemaphoreType.DMA((2,)),
            pltpu.SemaphoreType.DMA((2,)),
        ],
    )
    return pl.pallas_call(
        _experts_kernel,
        grid_spec=grid_spec,
        out_shape=jax.ShapeDtypeStruct((n_out_rows, d), F32),
        name="routed_experts",
        compiler_params=_params("arbitrary"),
    )(block_expert, n_used, idx_pack, x, w_gate, w_up, w_down)


def _shared_kernel(x_ref, wg_ref, wu_ref, wd_ref, o_ref):
    x = x_ref[...]
    xb = x.astype(BF16)
    gate = jnp.dot(xb, wg_ref[...], preferred_element_type=F32)
    up = jnp.dot(xb, wu_ref[...], preferred_element_type=F32)
    h = (_silu(gate) * up).astype(BF16)
    o_ref[...] = DN_ALPHA * x + jnp.dot(h, wd_ref[...], preferred_element_type=F32)


def _shared(x, wg, wu, wd, *, tm):
    t, d = x.shape
    f = wg.shape[1]
    return pl.pallas_call(
        _shared_kernel,
        grid=(t // tm,),
        in_specs=[
            pl.BlockSpec((tm, d), lambda i: (i, 0)),
            pl.BlockSpec((d, f), lambda i: (0, 0)),
            pl.BlockSpec((d, f), lambda i: (0, 0)),
            pl.BlockSpec((f, d), lambda i: (0, 0)),
        ],
        out_specs=pl.BlockSpec((tm, d), lambda i: (i, 0)),
        out_shape=jax.ShapeDtypeStruct((t, d), F32),
        name="shared_expert",
        compiler_params=_params("parallel"),
    )(x, wg, wu, wd)


def _combine_kernel(s_ref, w_ref, *refs):
    y_refs = refs[:TOP_K]
    g_ref, beta_ref, o_ref = refs[TOP_K:]
    acc = s_ref[...]
    for k in range(TOP_K):
        acc = acc + w_ref[:, k:k + 1] * y_refs[k][...]
    o_ref[...] = _layer_norm(acc, g_ref[...], beta_ref[...])


def _combine(s, wts_t, y, g, beta, *, tm):
    t, d = s.shape
    blocks = t // tm
    y_specs = [pl.BlockSpec((tm, d), functools.partial(lambda i, k: (k * blocks + i, 0), k=k))
               for k in range(TOP_K)]
    return pl.pallas_call(
        _combine_kernel,
        grid=(blocks,),
        in_specs=[
            pl.BlockSpec((tm, d), lambda i: (i, 0)),
            pl.BlockSpec((tm, TOP_K), lambda i: (i, 0)),
            *y_specs,
            pl.BlockSpec((1, d), lambda i: (0, 0)),
            pl.BlockSpec((1, d), lambda i: (0, 0)),
        ],
        out_specs=pl.BlockSpec((tm, d), lambda i: (i, 0)),
        out_shape=jax.ShapeDtypeStruct((t, d), F32),
        name="moe_combine",
        compiler_params=_params("parallel"),
    )(s, wts_t, *([y] * TOP_K), g.reshape(1, d), beta.reshape(1, d))


def _dispatch_plan(idx, rank, counts, n_tok):
    n_assign = n_tok * TOP_K
    n_blocks = n_assign // MOE_ROWS + N_EXPERTS
    n_slots = n_blocks * MOE_ROWS
    blocks_per_expert = (counts + MOE_ROWS - 1) // MOE_ROWS
    block_end = jnp.cumsum(blocks_per_expert)
    row_start = (block_end - blocks_per_expert) * MOE_ROWS
    dest = (row_start[idx] + rank).reshape(-1)
    slot = jnp.arange(n_slots, dtype=jnp.int32)
    spare = n_assign + ((slot // MOE_ROWS) % 2) * MOE_ROWS + slot % MOE_ROWS
    flat = jnp.arange(n_assign, dtype=jnp.int32)
    slot_tok = jnp.zeros((n_slots,), jnp.int32).at[dest].set(flat % n_tok)
    slot_dst = spare.at[dest].set(flat)
    idx_pack = jnp.concatenate([slot_tok.reshape(n_blocks, MOE_ROWS),
                                slot_dst.reshape(n_blocks, MOE_ROWS)], axis=1)
    block_expert = jnp.minimum(
        jnp.searchsorted(block_end, jnp.arange(n_blocks, dtype=jnp.int32), side="right"),
        N_EXPERTS - 1).astype(jnp.int32)
    n_used = block_end[-1:].astype(jnp.int32)
    return block_expert, n_used, idx_pack, n_assign + 2 * MOE_ROWS


def _moe(x, layer, w_router, router_bias, w_gate, w_up, w_down, ws_gate, ws_up, ws_down, g, beta):
    n_tok = x.shape[0]
    idx, wts, rank, cnt = _router(x, w_router.T, router_bias, tm=512)
    counts = cnt[:, 0].astype(jnp.int32)
    block_expert, n_used, idx_pack, n_out_rows = _dispatch_plan(idx, rank, counts, n_tok)
    y = _experts(x, layer, w_gate, w_up, w_down, block_expert, n_used, idx_pack,
                 n_out_rows=n_out_rows)
    s = _shared(x, ws_gate.astype(BF16), ws_up.astype(BF16), ws_down.astype(BF16), tm=512)
    return _combine(s, wts.T, y, g, beta, tm=128)


def kernel(x, a_w_in, a_b_in, a_ln_g, a_ln_b, a_w_s, a_b_s, a_w_out, b_w_qkv, b_b_qkv, b_sinks,
           b_w_o, b_b_o, moe_w_router, moe_router_bias, moe_w_gate, moe_w_up, moe_w_down,
           moe_ws_gate, moe_ws_up, moe_ws_down, norm_g, norm_b):
    bsz, seq_len, d = x.shape
    h = x.reshape(bsz * seq_len, d)
    for i in range(DEPTH):
        j = i // 2
        if i % 2 == 0:
            z = _linear(h, a_w_in[j].astype(BF16), a_b_in[j], act="gelu", tm=512, tn=1024,
                        out_dtype=BF16)
            causal = jnp.tril(jnp.ones((A_CHUNK, A_CHUNK), dtype=bool))
            w_mix = jnp.where(causal[None], a_w_s[j], 0.0).astype(BF16)
            gated = _gmlp_gate(z, a_ln_g[j], a_ln_b[j], w_mix, a_b_s[j].T, tm=256)
            h = _linear_res_ln(gated, a_w_out[j].astype(BF16), jnp.zeros((d,), F32), h,
                               norm_g[i, 0], norm_b[i, 0], tm=512)
        else:
            qkv = _linear(h, b_w_qkv[j].astype(BF16), b_b_qkv[j], act=None, tm=512, tn=1280,
                          out_dtype=BF16)
            o = _attention(qkv, b_sinks[j], seq_len=seq_len)
            h = _linear_res_ln(o, b_w_o[j].astype(BF16), b_b_o[j], h,
                               norm_g[i, 0], norm_b[i, 0], tm=512)
        h = _moe(h, i, moe_w_router[i], moe_router_bias[i], moe_w_gate, moe_w_up, moe_w_down,
                 moe_ws_gate[i], moe_ws_up[i], moe_ws_down[i], norm_g[i, 1], norm_b[i, 1])
    return h.reshape(bsz, seq_len, d)
```

```python
import functools

import jax
import jax.numpy as jnp
from jax import lax
from jax.experimental import pallas as pl
from jax.experimental.pallas import tpu as pltpu

F32 = jnp.float32
BF16 = jnp.bfloat16

D_MODEL = 2048
DEPTH = 2
A_CHUNK = 128
A_HALF = D_MODEL
A_GROUPS = A_HALF // 128
HEAD_DIM = 64
N_Q_HEADS = D_MODEL // HEAD_DIM
N_KV_HEADS = N_Q_HEADS // 8
Q_PER_KV = N_Q_HEADS // N_KV_HEADS
Q_DIM = N_Q_HEADS * HEAD_DIM
KV_DIM = N_KV_HEADS * HEAD_DIM
QKV_DIM = Q_DIM + 2 * KV_DIM
WINDOW = 128
ATTN_BLOCK = 128
N_EXPERTS = 64
TOP_K = 8
N_GROUPS = 8
GROUP_SIZE = N_EXPERTS // N_GROUPS
TOPK_GROUPS = 4
EXPERT_FF = D_MODEL // 4
ROUTED_SCALE = 2.5
DN_ALPHA = (2 * DEPTH) ** 0.25
LN_EPS = 1e-5

MOE_ROWS = 256
VMEM_LIMIT = 56 * 1024 * 1024


def _params(*sem):
    return pltpu.CompilerParams(dimension_semantics=sem, vmem_limit_bytes=VMEM_LIMIT)


def _layer_norm(y, g, b):
    mu = jnp.mean(y, axis=-1, keepdims=True)
    d = y - mu
    var = jnp.mean(d * d, axis=-1, keepdims=True)
    return d * lax.rsqrt(var + LN_EPS) * g + b


def _gelu(x):
    return 0.5 * x * (1.0 + lax.erf(x * (2.0 ** -0.5)))


def _silu(x):
    return x * jax.nn.sigmoid(x)


def _linear_kernel(x_ref, w_ref, b_ref, o_ref, *, act):
    acc = jnp.dot(x_ref[...].astype(BF16), w_ref[...], preferred_element_type=F32)
    acc = acc + b_ref[...]
    if act == "gelu":
        acc = _gelu(acc)
    o_ref[...] = acc.astype(o_ref.dtype)


def _linear(x, w, b, *, act, tm, tn, out_dtype):
    m, k = x.shape
    n = w.shape[1]
    return pl.pallas_call(
        functools.partial(_linear_kernel, act=act),
        grid=(m // tm, n // tn),
        in_specs=[
            pl.BlockSpec((tm, k), lambda i, j: (i, 0)),
            pl.BlockSpec((k, tn), lambda i, j: (0, j)),
            pl.BlockSpec((1, tn), lambda i, j: (0, j)),
        ],
        out_specs=pl.BlockSpec((tm, tn), lambda i, j: (i, j)),
        out_shape=jax.ShapeDtypeStruct((m, n), out_dtype),
        name="linear_" + str(act),
        compiler_params=_params("parallel", "arbitrary"),
    )(x, w, b.reshape(1, n))


def _linear_res_ln_kernel(x_ref, w_ref, b_ref, res_ref, g_ref, beta_ref, o_ref):
    acc = jnp.dot(x_ref[...], w_ref[...], preferred_element_type=F32)
    y = DN_ALPHA * res_ref[...] + (acc + b_ref[...])
    o_ref[...] = _layer_norm(y, g_ref[...], beta_ref[...])


def _linear_res_ln(x, w, b, res, g, beta, *, tm):
    m, k = x.shape
    n = w.shape[1]
    row = lambda i: (i, 0)
    fixed = lambda i: (0, 0)
    return pl.pallas_call(
        _linear_res_ln_kernel,
        grid=(m // tm,),
        in_specs=[
            pl.BlockSpec((tm, k), row),
            pl.BlockSpec((k, n), fixed),
            pl.BlockSpec((1, n), fixed),
            pl.BlockSpec((tm, n), row),
            pl.BlockSpec((1, n), fixed),
            pl.BlockSpec((1, n), fixed),
        ],
        out_specs=pl.BlockSpec((tm, n), row),
        out_shape=jax.ShapeDtypeStruct((m, n), F32),
        name="linear_res_ln",
        compiler_params=_params("parallel"),
    )(x, w, b.reshape(1, n), res, g.reshape(1, n), beta.reshape(1, n))


def _gmlp_gate_kernel(z_ref, lng_ref, lnb_ref, wmix_ref, bst_ref, o_ref, *, tm):
    v = z_ref[:, A_HALF:].astype(F32)
    vn = _layer_norm(v, lng_ref[...], lnb_ref[...]).astype(BF16)
    for c in range(tm // A_CHUNK):
        rows = slice(c * A_CHUNK, (c + 1) * A_CHUNK)
        for g in range(A_GROUPS):
            cols = slice(g * 128, (g + 1) * 128)
            mixed = jnp.dot(wmix_ref[g], vn[rows, cols], preferred_element_type=F32)
            mixed = mixed + bst_ref[:, g:g + 1]
            u = z_ref[rows, cols].astype(F32)
            o_ref[rows, cols] = (u * mixed).astype(o_ref.dtype)


def _gmlp_gate(z, ln_g, ln_b, w_mix, b_s_t, *, tm):
    m = z.shape[0]
    return pl.pallas_call(
        functools.partial(_gmlp_gate_kernel, tm=tm),
        grid=(m // tm,),
        in_specs=[
            pl.BlockSpec((tm, 2 * A_HALF), lambda i: (i, 0)),
            pl.BlockSpec((1, A_HALF), lambda i: (0, 0)),
            pl.BlockSpec((1, A_HALF), lambda i: (0, 0)),
            pl.BlockSpec((A_GROUPS, A_CHUNK, A_CHUNK), lambda i: (0, 0, 0)),
            pl.BlockSpec((A_CHUNK, A_GROUPS), lambda i: (0, 0)),
        ],
        out_specs=pl.BlockSpec((tm, A_HALF), lambda i: (i, 0)),
        out_shape=jax.ShapeDtypeStruct((m, A_HALF), BF16),
        name="gmlp_gate",
        compiler_params=_params("parallel"),
    )(z, ln_g.reshape(1, A_HALF), ln_b.reshape(1, A_HALF), w_mix, b_s_t)


def _attn_kernel(sinks_ref, q_ref, kp_ref, kc_ref, vp_ref, vc_ref, o_ref, *, blocks_per_seq):
    n = pl.program_id(0) % blocks_per_seq
    qi = lax.broadcasted_iota(jnp.int32, (ATTN_BLOCK, 2 * ATTN_BLOCK), 0)
    kj = lax.broadcasted_iota(jnp.int32, (ATTN_BLOCK, 2 * ATTN_BLOCK), 1)
    dist = qi + ATTN_BLOCK - kj
    valid = (dist >= 0) & (dist < WINDOW) & ((kj >= ATTN_BLOCK) | (n > 0))
    dist_f = dist.astype(F32)
    scale = HEAD_DIM ** -0.5
    for g in range(N_KV_HEADS):
        kv_cols = slice(g * HEAD_DIM, (g + 1) * HEAD_DIM)
        kk = jnp.concatenate([kp_ref[:, kv_cols], kc_ref[:, kv_cols]], axis=0)
        vv = jnp.concatenate([vp_ref[:, kv_cols], vc_ref[:, kv_cols]], axis=0)
        heads = [g * Q_PER_KV + j for j in range(Q_PER_KV)]
        qg = jnp.concatenate([q_ref[:, h * HEAD_DIM:(h + 1) * HEAD_DIM] for h in heads], axis=0)
        s = lax.dot_general(qg, kk, (((1,), (1,)), ((), ())), preferred_element_type=F32)
        probs = []
        for j, h in enumerate(heads):
            slope = 2.0 ** (-8.0 * (h + 1) / N_Q_HEADS)
            sj = s[j * ATTN_BLOCK:(j + 1) * ATTN_BLOCK] * scale + (-slope) * dist_f
            sj = jnp.where(valid, sj, -jnp.inf)
            sink = sinks_ref[h]
            mx = jnp.maximum(jnp.max(sj, axis=-1, keepdims=True), sink)
            e = jnp.exp(sj - mx)
            den = jnp.sum(e, axis=-1, keepdims=True) + jnp.exp(sink - mx)
            probs.append((e / den).astype(BF16))
        p = jnp.concatenate(probs, axis=0)
        o = jnp.dot(p, vv, preferred_element_type=F32)
        for j, h in enumerate(heads):
            o_ref[:, h * HEAD_DIM:(h + 1) * HEAD_DIM] = (
                o[j * ATTN_BLOCK:(j + 1) * ATTN_BLOCK].astype(o_ref.dtype))


def _attention(qkv, sinks, *, seq_len):
    t = qkv.shape[0]
    blocks_per_seq = seq_len // ATTN_BLOCK
    kcol = Q_DIM // KV_DIM
    vcol = kcol + 1
    prev = lambda i: jnp.maximum(i - 1, 0)
    grid_spec = pltpu.PrefetchScalarGridSpec(
        num_scalar_prefetch=1,
        grid=(t // ATTN_BLOCK,),
        in_specs=[
            pl.BlockSpec((ATTN_BLOCK, Q_DIM), lambda i, s: (i, 0)),
            pl.BlockSpec((ATTN_BLOCK, KV_DIM), lambda i, s: (prev(i), kcol)),
            pl.BlockSpec((ATTN_BLOCK, KV_DIM), lambda i, s: (i, kcol)),
            pl.BlockSpec((ATTN_BLOCK, KV_DIM), lambda i, s: (prev(i), vcol)),
            pl.BlockSpec((ATTN_BLOCK, KV_DIM), lambda i, s: (i, vcol)),
        ],
        out_specs=pl.BlockSpec((ATTN_BLOCK, Q_DIM), lambda i, s: (i, 0)),
    )
    return pl.pallas_call(
        functools.partial(_attn_kernel, blocks_per_seq=blocks_per_seq),
        grid_spec=grid_spec,
        out_shape=jax.ShapeDtypeStruct((t, Q_DIM), BF16),
        name="swa_attention",
        compiler_params=_params("parallel"),
    )(sinks, qkv, qkv, qkv, qkv, qkv)


def _router_kernel(x_ref, wrt_ref, rb_ref, idx_ref, wts_ref, rank_ref, cnt_ref, carry_ref, *, tm):
    @pl.when(pl.program_id(0) == 0)
    def _():
        carry_ref[...] = jnp.zeros_like(carry_ref)

    logits = lax.dot_general(wrt_ref[...], x_ref[...], (((1,), (1,)), ((), ())),
                             precision=lax.Precision.HIGHEST, preferred_element_type=F32)
    sc = jax.nn.sigmoid(logits)
    sel = sc + rb_ref[...]
    neg = -jnp.inf
    iota_g = lax.broadcasted_iota(jnp.int32, (GROUP_SIZE, tm), 0).astype(F32)

    def first_argmax(v, iota, size):
        m = jnp.max(v, axis=0, keepdims=True)
        return jnp.min(jnp.where(v == m, iota, float(size)), axis=0, keepdims=True)

    group_rows = []
    for g in range(N_GROUPS):
        v = sel[g * GROUP_SIZE:(g + 1) * GROUP_SIZE]
        m1 = jnp.max(v, axis=0, keepdims=True)
        i1 = first_argmax(v, iota_g, GROUP_SIZE)
        m2 = jnp.max(jnp.where(iota_g == i1, neg, v), axis=0, keepdims=True)
        group_rows.append(m1 + m2)
    cur = jnp.concatenate(group_rows, axis=0)
    iota_n = lax.broadcasted_iota(jnp.int32, (N_GROUPS, tm), 0).astype(F32)
    gsel = jnp.zeros((N_GROUPS, tm), F32)
    for _ in range(TOPK_GROUPS):
        hit = iota_n == first_argmax(cur, iota_n, N_GROUPS)
        gsel = jnp.where(hit, 1.0, gsel)
        cur = jnp.where(hit, neg, cur)
    cur = jnp.concatenate(
        [jnp.where(gsel[g:g + 1] > 0.5, sel[g * GROUP_SIZE:(g + 1) * GROUP_SIZE], neg)
         for g in range(N_GROUPS)], axis=0)

    iota_e = lax.broadcasted_iota(jnp.int32, (N_EXPERTS, tm), 0).astype(F32)
    member = jnp.zeros((N_EXPERTS, tm), F32)
    hits, idx_rows, w_rows = [], [], []
    for _ in range(TOP_K):
        ii = first_argmax(cur, iota_e, N_EXPERTS)
        hit = iota_e == ii
        hits.append(hit)
        idx_rows.append(ii)
        w_rows.append(jnp.sum(jnp.where(hit, sc, 0.0), axis=0, keepdims=True))
        member = jnp.where(hit, 1.0, member)
        cur = jnp.where(hit, neg, cur)
    w = jnp.concatenate(w_rows, axis=0)
    w = w / (jnp.sum(w, axis=0, keepdims=True) + 1e-20) * ROUTED_SCALE

    r = lax.broadcasted_iota(jnp.int32, (tm, tm), 0)
    c = lax.broadcasted_iota(jnp.int32, (tm, tm), 1)
    upper = jnp.where(r < c, 1.0, 0.0).astype(BF16)
    before = jnp.dot(member.astype(BF16), upper, preferred_element_type=F32)
    rank_full = carry_ref[...] + before
    rank_rows = [jnp.sum(jnp.where(h, rank_full, 0.0), axis=0, keepdims=True) for h in hits]

    idx_ref[...] = jnp.concatenate(idx_rows, axis=0).astype(jnp.int32)
    wts_ref[...] = w
    rank_ref[...] = jnp.concatenate(rank_rows, axis=0).astype(jnp.int32)
    carry_ref[...] = carry_ref[...] + jnp.sum(member, axis=1, keepdims=True)
    cnt_ref[...] = jnp.broadcast_to(carry_ref[...], cnt_ref.shape)


def _router(x, w_router_t, router_bias, *, tm):
    t, d = x.shape
    tok = lambda i: (0, i)
    return pl.pallas_call(
        functools.partial(_router_kernel, tm=tm),
        grid=(t // tm,),
        in_specs=[
            pl.BlockSpec((tm, d), lambda i: (i, 0)),
            pl.BlockSpec((N_EXPERTS, d), lambda i: (0, 0)),
            pl.BlockSpec((N_EXPERTS, 1), lambda i: (0, 0)),
        ],
        out_specs=[
            pl.BlockSpec((TOP_K, tm), tok),
            pl.BlockSpec((TOP_K, tm), tok),
            pl.BlockSpec((TOP_K, tm), tok),
            pl.BlockSpec((N_EXPERTS, 128), lambda i: (0, 0)),
        ],
        out_shape=[
            jax.ShapeDtypeStruct((TOP_K, t), jnp.int32),
            jax.ShapeDtypeStruct((TOP_K, t), F32),
            jax.ShapeDtypeStruct((TOP_K, t), jnp.int32),
            jax.ShapeDtypeStruct((N_EXPERTS, 128), F32),
        ],
        scratch_shapes=[pltpu.VMEM((N_EXPERTS, 1), F32)],
        name="router",
        compiler_params=_params("arbitrary"),
    )(x, w_router_t, router_bias.reshape(N_EXPERTS, 1))


def _experts_kernel(be_ref, nu_ref, idx_hbm, x_hbm, wg_ref, wu_ref, wd_ref, y_hbm,
                    idx_smem, xbuf, ybuf, wg_bf, wu_bf, wd_bf, isem, gsem, ssem, *, tok_mask):
    b = pl.program_id(0)
    n_used = nu_ref[0]
    rows = MOE_ROWS

    def idx_copy(blk):
        slot = blk % 4
        return pltpu.make_async_copy(idx_hbm.at[blk], idx_smem.at[slot], isem.at[slot])

    def start_gathers(blk, slot):
        islot = blk % 4
        for r in range(rows):
            tok = idx_smem[islot, r] & tok_mask
            pltpu.make_async_copy(x_hbm.at[pl.ds(tok, 1)], xbuf.at[slot, pl.ds(r, 1)],
                                  gsem.at[slot]).start()

    def start_scatters(blk, slot):
        islot = blk % 4
        for r in range(rows):
            dst = idx_smem[islot, r]
            pltpu.make_async_copy(ybuf.at[slot, pl.ds(r, 1)], y_hbm.at[pl.ds(dst, 1)],
                                  ssem.at[slot]).start()

    def wait_rows(buf, sem, slot):
        pltpu.make_async_copy(buf.at[slot], buf.at[slot], sem.at[slot]).wait()

    @pl.when(b == 0)
    def _():
        ybuf[...] = jnp.zeros_like(ybuf)
        spare0 = y_hbm.shape[0] - 2 * rows
        for s in range(2):
            fill = pltpu.make_async_copy(ybuf.at[s], y_hbm.at[pl.ds(spare0 + s * rows, rows)],
                                         ssem.at[s])
            fill.start()
            fill.wait()
        idx_copy(0).start()

        @pl.when(n_used > 1)
        def _():
            idx_copy(1).start()

        idx_copy(0).wait()
        start_gathers(0, 0)

    def step(slot):
        @pl.when(b + 1 < n_used)
        def _():
            idx_copy(b + 1).wait()
            start_gathers(b + 1, 1 - slot)

        @pl.when(b + 2 < n_used)
        def _():
            idx_copy(b + 2).start()

        changed = jnp.logical_or(b == 0, be_ref[b] != be_ref[jnp.maximum(b - 1, 0)])

        @pl.when(changed)
        def _():
            wg_bf[...] = wg_ref[...].astype(BF16)
            wu_bf[...] = wu_ref[...].astype(BF16)
            wd_bf[...] = wd_ref[...].astype(BF16)

        wait_rows(xbuf, gsem, slot)

        @pl.when(b >= 2)
        def _():
            wait_rows(ybuf, ssem, slot)

        x = xbuf[slot].astype(BF16)
        gate = jnp.dot(x, wg_bf[...], preferred_element_type=F32)
        up = jnp.dot(x, wu_bf[...], preferred_element_type=F32)
        h = (_silu(gate) * up).astype(BF16)
        ybuf[slot] = jnp.dot(h, wd_bf[...], preferred_element_type=F32)
        start_scatters(b, slot)

        @pl.when(b == n_used - 1)
        def _():
            @pl.when(b >= 1)
            def _():
                wait_rows(ybuf, ssem, 1 - slot)

            wait_rows(ybuf, ssem, slot)

    for parity in range(2):
        pl.when(jnp.logical_and(b < n_used, b % 2 == parity))(functools.partial(step, parity))


def _experts(x, layer, w_gate, w_up, w_down, block_expert, n_used, slot_flat, *, n_out_rows):
    t, d = x.shape
    assert t & (t - 1) == 0, "token id is taken as the low bits of the flat (choice, token) index"
    n_blocks = slot_flat.shape[0]
    wspec_in = pl.BlockSpec((None, None, d, EXPERT_FF), lambda b, be, nu: (layer, be[b], 0, 0))
    wspec_out = pl.BlockSpec((None, None, EXPERT_FF, d), lambda b, be, nu: (layer, be[b], 0, 0))
    grid_spec = pltpu.PrefetchScalarGridSpec(
        num_scalar_prefetch=2,
        grid=(n_blocks,),
        in_specs=[
            pl.BlockSpec(memory_space=pl.ANY),
            pl.BlockSpec(memory_space=pl.ANY),
            wspec_in, wspec_in, wspec_out,
        ],
        out_specs=pl.BlockSpec(memory_space=pl.ANY),
        scratch_shapes=[
            pltpu.SMEM((4, MOE_ROWS), jnp.int32),
            pltpu.VMEM((2, MOE_ROWS, d), F32),
            pltpu.VMEM((2, MOE_ROWS, d), F32),
            pltpu.VMEM((d, EXPERT_FF), BF16),
            pltpu.VMEM((d, EXPERT_FF), BF16),
            pltpu.VMEM((EXPERT_FF, d), BF16),
            pltpu.SemaphoreType.DMA((4,)),
            pltpu.SemaphoreType.DMA((2,)),
            pltpu.SemaphoreType.DMA((2,)),
        ],
    )
    return pl.pallas_call(
        functools.partial(_experts_kernel, tok_mask=t - 1),
        grid_spec=grid_spec,
        out_shape=jax.ShapeDtypeStruct((n_out_rows, d), F32),
        name="routed_experts",
        compiler_params=_params("arbitrary"),
    )(block_expert, n_used, slot_flat, x, w_gate, w_up, w_down)


def _shared_kernel(x_ref, wg_ref, wu_ref, wd_ref, o_ref):
    x = x_ref[...]
    xb = x.astype(BF16)
    gate = jnp.dot(xb, wg_ref[...], preferred_element_type=F32)
    up = jnp.dot(xb, wu_ref[...], preferred_element_type=F32)
    h = (_silu(gate) * up).astype(BF16)
    o_ref[...] = DN_ALPHA * x + jnp.dot(h, wd_ref[...], preferred_element_type=F32)


def _shared(x, wg, wu, wd, *, tm):
    t, d = x.shape
    f = wg.shape[1]
    return pl.pallas_call(
        _shared_kernel,
        grid=(t // tm,),
        in_specs=[
            pl.BlockSpec((tm, d), lambda i: (i, 0)),
            pl.BlockSpec((d, f), lambda i: (0, 0)),
            pl.BlockSpec((d, f), lambda i: (0, 0)),
            pl.BlockSpec((f, d), lambda i: (0, 0)),
        ],
        out_specs=pl.BlockSpec((tm, d), lambda i: (i, 0)),
        out_shape=jax.ShapeDtypeStruct((t, d), F32),
        name="shared_expert",
        compiler_params=_params("parallel"),
    )(x, wg, wu, wd)


def _combine_kernel(s_ref, w_ref, *refs):
    y_refs = refs[:TOP_K]
    g_ref, beta_ref, o_ref = refs[TOP_K:]
    acc = s_ref[...]
    for k in range(TOP_K):
        acc = acc + w_ref[:, k:k + 1] * y_refs[k][...]
    o_ref[...] = _layer_norm(acc, g_ref[...], beta_ref[...])


def _combine(s, wts_t, y, g, beta, *, tm):
    t, d = s.shape
    blocks = t // tm
    y_specs = [pl.BlockSpec((tm, d), functools.partial(lambda i, k: (k * blocks + i, 0), k=k))
               for k in range(TOP_K)]
    return pl.pallas_call(
        _combine_kernel,
        grid=(blocks,),
        in_specs=[
            pl.BlockSpec((tm, d), lambda i: (i, 0)),
            pl.BlockSpec((tm, TOP_K), lambda i: (i, 0)),
            *y_specs,
            pl.BlockSpec((1, d), lambda i: (0, 0)),
            pl.BlockSpec((1, d), lambda i: (0, 0)),
        ],
        out_specs=pl.BlockSpec((tm, d), lambda i: (i, 0)),
        out_shape=jax.ShapeDtypeStruct((t, d), F32),
        name="moe_combine",
        compiler_params=_params("parallel"),
    )(s, wts_t, *([y] * TOP_K), g.reshape(1, d), beta.reshape(1, d))


def _dispatch_plan(idx, rank, counts, n_tok):
    n_assign = n_tok * TOP_K
    n_blocks = n_assign // MOE_ROWS + N_EXPERTS
    n_slots = n_blocks * MOE_ROWS
    blocks_per_expert = (counts + MOE_ROWS - 1) // MOE_ROWS
    block_end = jnp.cumsum(blocks_per_expert)
    row_start = (block_end - blocks_per_expert) * MOE_ROWS
    experts = jnp.arange(N_EXPERTS, dtype=jnp.int32)
    row_start_of = jnp.sum(jnp.where(idx[..., None] == experts, row_start, 0), axis=-1)
    dest = (row_start_of + rank).reshape(-1)
    slot = jnp.arange(n_slots, dtype=jnp.int32)
    spare = n_assign + ((slot // MOE_ROWS) % 2) * MOE_ROWS + slot % MOE_ROWS
    flat = jnp.arange(n_assign, dtype=jnp.int32)
    slot_flat = spare.at[dest].set(flat, unique_indices=True).reshape(n_blocks, MOE_ROWS)
    blocks = jnp.arange(n_blocks, dtype=jnp.int32)
    block_expert = jnp.minimum(
        jnp.sum((block_end[None, :] <= blocks[:, None]).astype(jnp.int32), axis=1), N_EXPERTS - 1)
    n_used = block_end[-1:].astype(jnp.int32)
    return block_expert, n_used, slot_flat, n_assign + 2 * MOE_ROWS


def _moe(x, layer, w_router, router_bias, w_gate, w_up, w_down, ws_gate, ws_up, ws_down, g, beta):
    n_tok = x.shape[0]
    idx, wts, rank, cnt = _router(x, w_router.T, router_bias, tm=512)
    counts = cnt[:, 0].astype(jnp.int32)
    block_expert, n_used, slot_flat, n_out_rows = _dispatch_plan(idx, rank, counts, n_tok)
    y = _experts(x, layer, w_gate, w_up, w_down, block_expert, n_used, slot_flat,
                 n_out_rows=n_out_rows)
    s = _shared(x, ws_gate.astype(BF16), ws_up.astype(BF16), ws_down.astype(BF16), tm=512)
    return _combine(s, wts.T, y, g, beta, tm=128)


def kernel(x, a_w_in, a_b_in, a_ln_g, a_ln_b, a_w_s, a_b_s, a_w_out, b_w_qkv, b_b_qkv, b_sinks,
           b_w_o, b_b_o, moe_w_router, moe_router_bias, moe_w_gate, moe_w_up, moe_w_down,
           moe_ws_gate, moe_ws_up, moe_ws_down, norm_g, norm_b):
    bsz, seq_len, d = x.shape
    h = x.reshape(bsz * seq_len, d)
    for i in range(DEPTH):
        j = i // 2
        if i % 2 == 0:
            z = _linear(h, a_w_in[j].astype(BF16), a_b_in[j], act="gelu", tm=512, tn=1024,
                        out_dtype=BF16)
            causal = jnp.tril(jnp.ones((A_CHUNK, A_CHUNK), dtype=bool))
            w_mix = jnp.where(causal[None], a_w_s[j], 0.0).astype(BF16)
            gated = _gmlp_gate(z, a_ln_g[j], a_ln_b[j], w_mix, a_b_s[j].T, tm=256)
            h = _linear_res_ln(gated, a_w_out[j].astype(BF16), jnp.zeros((d,), F32), h,
                               norm_g[i, 0], norm_b[i, 0], tm=512)
        else:
            qkv = _linear(h, b_w_qkv[j].astype(BF16), b_b_qkv[j], act=None, tm=512, tn=1280,
                          out_dtype=BF16)
            o = _attention(qkv, b_sinks[j], seq_len=seq_len)
            h = _linear_res_ln(o, b_w_o[j].astype(BF16), b_b_o[j], h,
                               norm_g[i, 0], norm_b[i, 0], tm=512)
        h = _moe(h, i, moe_w_router[i], moe_router_bias[i], moe_w_gate, moe_w_up, moe_w_down,
                 moe_ws_gate[i], moe_ws_up[i], moe_ws_down[i], norm_g[i, 1], norm_b[i, 1])
    return h.reshape(bsz, seq_len, d)
```

```python
import functools

import jax
import jax.numpy as jnp
from jax import lax
from jax.experimental import pallas as pl
from jax.experimental.pallas import tpu as pltpu

F32 = jnp.float32
BF16 = jnp.bfloat16

D_MODEL = 2048
DEPTH = 2
A_CHUNK = 128
A_HALF = D_MODEL
A_GROUPS = A_HALF // 128
HEAD_DIM = 64
N_Q_HEADS = D_MODEL // HEAD_DIM
N_KV_HEADS = N_Q_HEADS // 8
Q_PER_KV = N_Q_HEADS // N_KV_HEADS
Q_DIM = N_Q_HEADS * HEAD_DIM
KV_DIM = N_KV_HEADS * HEAD_DIM
QKV_DIM = Q_DIM + 2 * KV_DIM
WINDOW = 128
ATTN_BLOCK = 128
N_EXPERTS = 64
TOP_K = 8
N_GROUPS = 8
GROUP_SIZE = N_EXPERTS // N_GROUPS
TOPK_GROUPS = 4
EXPERT_FF = D_MODEL // 4
ROUTED_SCALE = 2.5
DN_ALPHA = (2 * DEPTH) ** 0.25
LN_EPS = 1e-5

MOE_ROWS = 256
VMEM_LIMIT = 56 * 1024 * 1024


def _params(*sem):
    return pltpu.CompilerParams(dimension_semantics=sem, vmem_limit_bytes=VMEM_LIMIT)


def _layer_norm(y, g, b):
    mu = jnp.mean(y, axis=-1, keepdims=True)
    d = y - mu
    var = jnp.mean(d * d, axis=-1, keepdims=True)
    return d * lax.rsqrt(var + LN_EPS) * g + b


def _gelu(x):
    return 0.5 * x * (1.0 + lax.erf(x * (2.0 ** -0.5)))


def _silu(x):
    return x * jax.nn.sigmoid(x)


LANES = 128
PACK_SUB = D_MODEL // (2 * LANES)
HIGH_HALF = -65536


def _store_packed_rows(p_ref, rows_f32):
    n = rows_f32.shape[0]
    bits = lax.bitcast_convert_type(rows_f32.astype(BF16).astype(F32), jnp.int32)
    for s in range(PACK_SUB):
        hi = bits[:, s * LANES:(s + 1) * LANES] & HIGH_HALF
        lo = lax.shift_right_logical(
            bits[:, D_MODEL // 2 + s * LANES:D_MODEL // 2 + (s + 1) * LANES], 16)
        p_ref[pl.ds(s, n, stride=PACK_SUB), :] = hi | lo


def _load_packed_rows(p_ref, n):
    his, los = [], []
    for s in range(PACK_SUB):
        w = p_ref[pl.ds(s, n, stride=PACK_SUB), :]
        his.append(lax.bitcast_convert_type(w & HIGH_HALF, F32))
        los.append(lax.bitcast_convert_type(lax.shift_left(w, 16), F32))
    return his + los


def _linear_kernel(x_ref, w_ref, b_ref, o_ref, *, act):
    acc = jnp.dot(x_ref[...].astype(BF16), w_ref[...], preferred_element_type=F32)
    acc = acc + b_ref[...]
    if act == "gelu":
        acc = _gelu(acc)
    o_ref[...] = acc.astype(o_ref.dtype)


def _linear(x, w, b, *, act, tm, tn, out_dtype):
    m, k = x.shape
    n = w.shape[1]
    return pl.pallas_call(
        functools.partial(_linear_kernel, act=act),
        grid=(m // tm, n // tn),
        in_specs=[
            pl.BlockSpec((tm, k), lambda i, j: (i, 0)),
            pl.BlockSpec((k, tn), lambda i, j: (0, j)),
            pl.BlockSpec((1, tn), lambda i, j: (0, j)),
        ],
        out_specs=pl.BlockSpec((tm, tn), lambda i, j: (i, j)),
        out_shape=jax.ShapeDtypeStruct((m, n), out_dtype),
        name="linear_" + str(act),
        compiler_params=_params("parallel", "arbitrary"),
    )(x, w, b.reshape(1, n))


def _linear_res_ln_kernel(x_ref, w_ref, b_ref, res_ref, g_ref, beta_ref, o_ref, p_ref):
    acc = jnp.dot(x_ref[...], w_ref[...], preferred_element_type=F32)
    y = DN_ALPHA * res_ref[...] + (acc + b_ref[...])
    out = _layer_norm(y, g_ref[...], beta_ref[...])
    o_ref[...] = out
    _store_packed_rows(p_ref, out)


def _linear_res_ln(x, w, b, res, g, beta, *, tm):
    m, k = x.shape
    n = w.shape[1]
    row = lambda i: (i, 0)
    fixed = lambda i: (0, 0)
    return pl.pallas_call(
        _linear_res_ln_kernel,
        grid=(m // tm,),
        in_specs=[
            pl.BlockSpec((tm, k), row),
            pl.BlockSpec((k, n), fixed),
            pl.BlockSpec((1, n), fixed),
            pl.BlockSpec((tm, n), row),
            pl.BlockSpec((1, n), fixed),
            pl.BlockSpec((1, n), fixed),
        ],
        out_specs=[pl.BlockSpec((tm, n), row), pl.BlockSpec((tm * PACK_SUB, LANES), row)],
        out_shape=[jax.ShapeDtypeStruct((m, n), F32),
                   jax.ShapeDtypeStruct((m * PACK_SUB, LANES), jnp.int32)],
        name="linear_res_ln",
        compiler_params=_params("parallel"),
    )(x, w, b.reshape(1, n), res, g.reshape(1, n), beta.reshape(1, n))


def _gmlp_gate_kernel(z_ref, lng_ref, lnb_ref, wmix_ref, bst_ref, o_ref, *, tm):
    v = z_ref[:, A_HALF:].astype(F32)
    vn = _layer_norm(v, lng_ref[...], lnb_ref[...]).astype(BF16)
    for c in range(tm // A_CHUNK):
        rows = slice(c * A_CHUNK, (c + 1) * A_CHUNK)
        for g in range(A_GROUPS):
            cols = slice(g * 128, (g + 1) * 128)
            mixed = jnp.dot(wmix_ref[g], vn[rows, cols], preferred_element_type=F32)
            mixed = mixed + bst_ref[:, g:g + 1]
            u = z_ref[rows, cols].astype(F32)
            o_ref[rows, cols] = (u * mixed).astype(o_ref.dtype)


def _gmlp_gate(z, ln_g, ln_b, w_mix, b_s_t, *, tm):
    m = z.shape[0]
    return pl.pallas_call(
        functools.partial(_gmlp_gate_kernel, tm=tm),
        grid=(m // tm,),
        in_specs=[
            pl.BlockSpec((tm, 2 * A_HALF), lambda i: (i, 0)),
            pl.BlockSpec((1, A_HALF), lambda i: (0, 0)),
            pl.BlockSpec((1, A_HALF), lambda i: (0, 0)),
            pl.BlockSpec((A_GROUPS, A_CHUNK, A_CHUNK), lambda i: (0, 0, 0)),
            pl.BlockSpec((A_CHUNK, A_GROUPS), lambda i: (0, 0)),
        ],
        out_specs=pl.BlockSpec((tm, A_HALF), lambda i: (i, 0)),
        out_shape=jax.ShapeDtypeStruct((m, A_HALF), BF16),
        name="gmlp_gate",
        compiler_params=_params("parallel"),
    )(z, ln_g.reshape(1, A_HALF), ln_b.reshape(1, A_HALF), w_mix, b_s_t)


def _attn_kernel(sinks_ref, q_ref, kp_ref, kc_ref, vp_ref, vc_ref, o_ref, *, blocks_per_seq):
    n = pl.program_id(0) % blocks_per_seq
    qi = lax.broadcasted_iota(jnp.int32, (ATTN_BLOCK, 2 * ATTN_BLOCK), 0)
    kj = lax.broadcasted_iota(jnp.int32, (ATTN_BLOCK, 2 * ATTN_BLOCK), 1)
    dist = qi + ATTN_BLOCK - kj
    valid = (dist >= 0) & (dist < WINDOW) & ((kj >= ATTN_BLOCK) | (n > 0))
    dist_f = dist.astype(F32)
    scale = HEAD_DIM ** -0.5
    for g in range(N_KV_HEADS):
        kv_cols = slice(g * HEAD_DIM, (g + 1) * HEAD_DIM)
        kk = jnp.concatenate([kp_ref[:, kv_cols], kc_ref[:, kv_cols]], axis=0)
        vv = jnp.concatenate([vp_ref[:, kv_cols], vc_ref[:, kv_cols]], axis=0)
        heads = [g * Q_PER_KV + j for j in range(Q_PER_KV)]
        qg = jnp.concatenate([q_ref[:, h * HEAD_DIM:(h + 1) * HEAD_DIM] for h in heads], axis=0)
        s = lax.dot_general(qg, kk, (((1,), (1,)), ((), ())), preferred_element_type=F32)
        probs = []
        for j, h in enumerate(heads):
            slope = 2.0 ** (-8.0 * (h + 1) / N_Q_HEADS)
            sj = s[j * ATTN_BLOCK:(j + 1) * ATTN_BLOCK] * scale + (-slope) * dist_f
            sj = jnp.where(valid, sj, -jnp.inf)
            sink = sinks_ref[h]
            mx = jnp.maximum(jnp.max(sj, axis=-1, keepdims=True), sink)
            e = jnp.exp(sj - mx)
            den = jnp.sum(e, axis=-1, keepdims=True) + jnp.exp(sink - mx)
            probs.append((e / den).astype(BF16))
        p = jnp.concatenate(probs, axis=0)
        o = jnp.dot(p, vv, preferred_element_type=F32)
        for j, h in enumerate(heads):
            o_ref[:, h * HEAD_DIM:(h + 1) * HEAD_DIM] = (
                o[j * ATTN_BLOCK:(j + 1) * ATTN_BLOCK].astype(o_ref.dtype))


def _attention(qkv, sinks, *, seq_len):
    t = qkv.shape[0]
    blocks_per_seq = seq_len // ATTN_BLOCK
    kcol = Q_DIM // KV_DIM
    vcol = kcol + 1
    prev = lambda i: jnp.maximum(i - 1, 0)
    grid_spec = pltpu.PrefetchScalarGridSpec(
        num_scalar_prefetch=1,
        grid=(t // ATTN_BLOCK,),
        in_specs=[
            pl.BlockSpec((ATTN_BLOCK, Q_DIM), lambda i, s: (i, 0)),
            pl.BlockSpec((ATTN_BLOCK, KV_DIM), lambda i, s: (prev(i), kcol)),
            pl.BlockSpec((ATTN_BLOCK, KV_DIM), lambda i, s: (i, kcol)),
            pl.BlockSpec((ATTN_BLOCK, KV_DIM), lambda i, s: (prev(i), vcol)),
            pl.BlockSpec((ATTN_BLOCK, KV_DIM), lambda i, s: (i, vcol)),
        ],
        out_specs=pl.BlockSpec((ATTN_BLOCK, Q_DIM), lambda i, s: (i, 0)),
    )
    return pl.pallas_call(
        functools.partial(_attn_kernel, blocks_per_seq=blocks_per_seq),
        grid_spec=grid_spec,
        out_shape=jax.ShapeDtypeStruct((t, Q_DIM), BF16),
        name="swa_attention",
        compiler_params=_params("parallel"),
    )(sinks, qkv, qkv, qkv, qkv, qkv)


def _router_kernel(x_ref, wrt_ref, rb_ref, idx_ref, wts_ref, rank_ref, cnt_ref, carry_ref, *, tm):
    @pl.when(pl.program_id(0) == 0)
    def _():
        carry_ref[...] = jnp.zeros_like(carry_ref)

    logits = lax.dot_general(wrt_ref[...], x_ref[...], (((1,), (1,)), ((), ())),
                             precision=lax.Precision.HIGHEST, preferred_element_type=F32)
    sc = jax.nn.sigmoid(logits)
    sel = sc + rb_ref[...]
    neg = -jnp.inf
    iota_g = lax.broadcasted_iota(jnp.int32, (GROUP_SIZE, tm), 0).astype(F32)

    def first_argmax(v, iota, size):
        m = jnp.max(v, axis=0, keepdims=True)
        return jnp.min(jnp.where(v == m, iota, float(size)), axis=0, keepdims=True)

    group_rows = []
    for g in range(N_GROUPS):
        v = sel[g * GROUP_SIZE:(g + 1) * GROUP_SIZE]
        m1 = jnp.max(v, axis=0, keepdims=True)
        i1 = first_argmax(v, iota_g, GROUP_SIZE)
        m2 = jnp.max(jnp.where(iota_g == i1, neg, v), axis=0, keepdims=True)
        group_rows.append(m1 + m2)
    cur = jnp.concatenate(group_rows, axis=0)
    iota_n = lax.broadcasted_iota(jnp.int32, (N_GROUPS, tm), 0).astype(F32)
    gsel = jnp.zeros((N_GROUPS, tm), F32)
    for _ in range(TOPK_GROUPS):
        hit = iota_n == first_argmax(cur, iota_n, N_GROUPS)
        gsel = jnp.where(hit, 1.0, gsel)
        cur = jnp.where(hit, neg, cur)
    cur = jnp.concatenate(
        [jnp.where(gsel[g:g + 1] > 0.5, sel[g * GROUP_SIZE:(g + 1) * GROUP_SIZE], neg)
         for g in range(N_GROUPS)], axis=0)

    iota_e = lax.broadcasted_iota(jnp.int32, (N_EXPERTS, tm), 0).astype(F32)
    member = jnp.zeros((N_EXPERTS, tm), F32)
    hits, idx_rows, w_rows = [], [], []
    for _ in range(TOP_K):
        ii = first_argmax(cur, iota_e, N_EXPERTS)
        hit = iota_e == ii
        hits.append(hit)
        idx_rows.append(ii)
        w_rows.append(jnp.sum(jnp.where(hit, sc, 0.0), axis=0, keepdims=True))
        member = jnp.where(hit, 1.0, member)
        cur = jnp.where(hit, neg, cur)
    w = jnp.concatenate(w_rows, axis=0)
    w = w / (jnp.sum(w, axis=0, keepdims=True) + 1e-20) * ROUTED_SCALE

    r = lax.broadcasted_iota(jnp.int32, (tm, tm), 0)
    c = lax.broadcasted_iota(jnp.int32, (tm, tm), 1)
    upper = jnp.where(r < c, 1.0, 0.0).astype(BF16)
    before = jnp.dot(member.astype(BF16), upper, preferred_element_type=F32)
    rank_full = carry_ref[...] + before
    rank_rows = [jnp.sum(jnp.where(h, rank_full, 0.0), axis=0, keepdims=True) for h in hits]

    idx_ref[...] = jnp.concatenate(idx_rows, axis=0).astype(jnp.int32)
    wts_ref[...] = w
    rank_ref[...] = jnp.concatenate(rank_rows, axis=0).astype(jnp.int32)
    carry_ref[...] = carry_ref[...] + jnp.sum(member, axis=1, keepdims=True)
    cnt_ref[...] = jnp.broadcast_to(carry_ref[...], cnt_ref.shape)


def _router(x, w_router_t, router_bias, *, tm):
    t, d = x.shape
    tok = lambda i: (0, i)
    return pl.pallas_call(
        functools.partial(_router_kernel, tm=tm),
        grid=(t // tm,),
        in_specs=[
            pl.BlockSpec((tm, d), lambda i: (i, 0)),
            pl.BlockSpec((N_EXPERTS, d), lambda i: (0, 0)),
            pl.BlockSpec((N_EXPERTS, 1), lambda i: (0, 0)),
        ],
        out_specs=[
            pl.BlockSpec((TOP_K, tm), tok),
            pl.BlockSpec((TOP_K, tm), tok),
            pl.BlockSpec((TOP_K, tm), tok),
            pl.BlockSpec((N_EXPERTS, 128), lambda i: (0, 0)),
        ],
        out_shape=[
            jax.ShapeDtypeStruct((TOP_K, t), jnp.int32),
            jax.ShapeDtypeStruct((TOP_K, t), F32),
            jax.ShapeDtypeStruct((TOP_K, t), jnp.int32),
            jax.ShapeDtypeStruct((N_EXPERTS, 128), F32),
        ],
        scratch_shapes=[pltpu.VMEM((N_EXPERTS, 1), F32)],
        name="router",
        compiler_params=_params("arbitrary"),
    )(x, w_router_t, router_bias.reshape(N_EXPERTS, 1))


def _experts_kernel(be_ref, nu_ref, plan_hbm, x_hbm, wg_ref, wu_ref, wd_ref, y_hbm,
                    plan_smem, xbuf0, xbuf1, ybuf0, ybuf1, wg_bf, wu_bf, wd_bf,
                    psem, gsem, ssem, *, tok_mask):
    b = pl.program_id(0)
    n_used = nu_ref[0]
    rows = MOE_ROWS
    xbuf = (xbuf0, xbuf1)
    ybuf = (ybuf0, ybuf1)

    def plan_copy(row, slot):
        return pltpu.make_async_copy(plan_hbm.at[row], plan_smem.at[slot], psem.at[slot])

    def start_gathers(pslot, slot):
        for r in range(rows):
            src = pl.multiple_of(plan_smem[pslot, r] & tok_mask, PACK_SUB)
            pltpu.make_async_copy(x_hbm.at[pl.ds(src, PACK_SUB)],
                                  xbuf[slot].at[pl.ds(r * PACK_SUB, PACK_SUB)],
                                  gsem.at[slot]).start()

    def start_scatters(pslot, slot):
        for r in range(rows):
            dst = pl.multiple_of(plan_smem[pslot, rows + r], PACK_SUB)
            pltpu.make_async_copy(ybuf[slot].at[pl.ds(r * PACK_SUB, PACK_SUB)],
                                  y_hbm.at[pl.ds(dst, PACK_SUB)],
                                  ssem.at[slot]).start()

    def wait_rows(buf, sem):
        pltpu.make_async_copy(buf, buf, sem).wait()

    @pl.when(b == 0)
    def _():
        ybuf0[...] = jnp.zeros_like(ybuf0)
        ybuf1[...] = jnp.zeros_like(ybuf1)
        spare0 = y_hbm.shape[0] - 2 * rows * PACK_SUB
        pltpu.make_async_copy(ybuf0, y_hbm.at[pl.ds(spare0, rows * PACK_SUB)], ssem.at[0]).start()
        plan_copy(0, 0).start()
        plan_copy(0, 0).wait()
        start_gathers(0, 0)
        plan_copy(1, 1).start()

    def step(slot):
        other = 1 - slot
        changed = jnp.logical_or(b == 0, be_ref[b] != be_ref[jnp.maximum(b - 1, 0)])

        @pl.when(changed)
        def _():
            wg_bf[...] = wg_ref[...].astype(BF16)
            wu_bf[...] = wu_ref[...].astype(BF16)
            wd_bf[...] = wd_ref[...].astype(BF16)

        wait_rows(xbuf[slot], gsem.at[slot])
        wait_rows(ybuf[slot], ssem.at[slot])
        plan_copy(b + 1, other).wait()
        start_gathers(other, other)
        start_scatters(other, other)
        plan_copy(b + 2, slot).start()

        x = jnp.concatenate([p.astype(BF16) for p in _load_packed_rows(xbuf[slot], rows)], axis=1)
        gate = jnp.dot(x, wg_bf[...], preferred_element_type=F32)
        up = jnp.dot(x, wu_bf[...], preferred_element_type=F32)
        h = (_silu(gate) * up).astype(BF16)
        _store_packed_rows(ybuf[slot], jnp.dot(h, wd_bf[...], preferred_element_type=F32))

        @pl.when(b == n_used - 1)
        def _():
            plan_copy(b + 2, slot).wait()
            start_scatters(slot, slot)
            wait_rows(xbuf[other], gsem.at[other])
            wait_rows(ybuf[other], ssem.at[other])
            wait_rows(ybuf[slot], ssem.at[slot])

    for parity in range(2):
        pl.when(jnp.logical_and(b < n_used, b % 2 == parity))(functools.partial(step, parity))


def _experts(x_packed, layer, w_gate, w_up, w_down, block_expert, n_used, plan, *, n_out_rows):
    t = x_packed.shape[0] // PACK_SUB
    d = D_MODEL
    assert t & (t - 1) == 0, "token id is taken as the low bits of the flat (choice, token) index"
    n_blocks = plan.shape[0] - 2
    wspec_in = pl.BlockSpec((None, None, d, EXPERT_FF), lambda b, be, nu: (layer, be[b], 0, 0))
    wspec_out = pl.BlockSpec((None, None, EXPERT_FF, d), lambda b, be, nu: (layer, be[b], 0, 0))
    row_buf = pltpu.VMEM((MOE_ROWS * PACK_SUB, LANES), jnp.int32)
    grid_spec = pltpu.PrefetchScalarGridSpec(
        num_scalar_prefetch=2,
        grid=(n_blocks,),
        in_specs=[
            pl.BlockSpec(memory_space=pl.ANY),
            pl.BlockSpec(memory_space=pl.ANY),
            wspec_in, wspec_in, wspec_out,
        ],
        out_specs=pl.BlockSpec(memory_space=pl.ANY),
        scratch_shapes=[
            pltpu.SMEM((2, 2 * MOE_ROWS), jnp.int32),
            row_buf, row_buf, row_buf, row_buf,
            pltpu.VMEM((d, EXPERT_FF), BF16),
            pltpu.VMEM((d, EXPERT_FF), BF16),
            pltpu.VMEM((EXPERT_FF, d), BF16),
            pltpu.SemaphoreType.DMA((2,)),
            pltpu.SemaphoreType.DMA((2,)),
            pltpu.SemaphoreType.DMA((2,)),
        ],
    )
    return pl.pallas_call(
        functools.partial(_experts_kernel, tok_mask=(t - 1) * PACK_SUB),
        grid_spec=grid_spec,
        out_shape=jax.ShapeDtypeStruct((n_out_rows * PACK_SUB, LANES), jnp.int32),
        name="routed_experts",
        compiler_params=_params("arbitrary"),
    )(block_expert, n_used, plan, x_packed, w_gate, w_up, w_down)


def _shared_kernel(x_ref, wg_ref, wu_ref, wd_ref, o_ref):
    x = x_ref[...]
    xb = x.astype(BF16)
    gate = jnp.dot(xb, wg_ref[...], preferred_element_type=F32)
    up = jnp.dot(xb, wu_ref[...], preferred_element_type=F32)
    h = (_silu(gate) * up).astype(BF16)
    o_ref[...] = DN_ALPHA * x + jnp.dot(h, wd_ref[...], preferred_element_type=F32)


def _shared(x, wg, wu, wd, *, tm):
    t, d = x.shape
    f = wg.shape[1]
    return pl.pallas_call(
        _shared_kernel,
        grid=(t // tm,),
        in_specs=[
            pl.BlockSpec((tm, d), lambda i: (i, 0)),
            pl.BlockSpec((d, f), lambda i: (0, 0)),
            pl.BlockSpec((d, f), lambda i: (0, 0)),
            pl.BlockSpec((f, d), lambda i: (0, 0)),
        ],
        out_specs=pl.BlockSpec((tm, d), lambda i: (i, 0)),
        out_shape=jax.ShapeDtypeStruct((t, d), F32),
        name="shared_expert",
        compiler_params=_params("parallel"),
    )(x, wg, wu, wd)


def _combine_kernel(s_ref, w_ref, *refs, tm):
    y_refs = refs[:TOP_K]
    g_ref, beta_ref, o_ref = refs[TOP_K:]
    pieces = [s_ref[:, c * LANES:(c + 1) * LANES] for c in range(2 * PACK_SUB)]
    for k in range(TOP_K):
        wk = w_ref[:, k:k + 1]
        yk = _load_packed_rows(y_refs[k], tm)
        pieces = [p + wk * y for p, y in zip(pieces, yk)]
    o_ref[...] = _layer_norm(jnp.concatenate(pieces, axis=1), g_ref[...], beta_ref[...])


def _combine(s, wts_t, y, g, beta, *, tm):
    t, d = s.shape
    blocks = t // tm
    y_specs = [pl.BlockSpec((tm * PACK_SUB, LANES),
                            functools.partial(lambda i, k: (k * blocks + i, 0), k=k))
               for k in range(TOP_K)]
    return pl.pallas_call(
        functools.partial(_combine_kernel, tm=tm),
        grid=(blocks,),
        in_specs=[
            pl.BlockSpec((tm, d), lambda i: (i, 0)),
            pl.BlockSpec((tm, TOP_K), lambda i: (i, 0)),
            *y_specs,
            pl.BlockSpec((1, d), lambda i: (0, 0)),
            pl.BlockSpec((1, d), lambda i: (0, 0)),
        ],
        out_specs=pl.BlockSpec((tm, d), lambda i: (i, 0)),
        out_shape=jax.ShapeDtypeStruct((t, d), F32),
        name="moe_combine",
        compiler_params=_params("parallel"),
    )(s, wts_t, *([y] * TOP_K), g.reshape(1, d), beta.reshape(1, d))


def _dispatch_plan(idx, rank, counts, n_tok):
    n_assign = n_tok * TOP_K
    n_blocks = n_assign // MOE_ROWS + N_EXPERTS
    n_slots = n_blocks * MOE_ROWS
    blocks_per_expert = (counts + MOE_ROWS - 1) // MOE_ROWS
    block_end = jnp.cumsum(blocks_per_expert)
    row_start = (block_end - blocks_per_expert) * MOE_ROWS
    experts = jnp.arange(N_EXPERTS, dtype=jnp.int32)
    row_start_of = jnp.sum(jnp.where(idx[..., None] == experts, row_start, 0), axis=-1)
    dest = (row_start_of + rank).reshape(-1)
    slot = jnp.arange(n_slots, dtype=jnp.int32)
    spare = n_assign + ((slot // MOE_ROWS) % 2) * MOE_ROWS + slot % MOE_ROWS
    flat = jnp.arange(n_assign, dtype=jnp.int32)
    slot_flat = spare.at[dest].set(flat, unique_indices=True).reshape(n_blocks, MOE_ROWS)
    rows = slot_flat * PACK_SUB
    pad = jnp.broadcast_to((n_assign + MOE_ROWS + slot[:MOE_ROWS]) * PACK_SUB, (2, MOE_ROWS))
    plan = jnp.concatenate([jnp.concatenate([rows, pad], axis=0),
                            jnp.concatenate([pad, rows], axis=0)], axis=1)
    blocks = jnp.arange(n_blocks, dtype=jnp.int32)
    block_expert = jnp.minimum(
        jnp.sum((block_end[None, :] <= blocks[:, None]).astype(jnp.int32), axis=1), N_EXPERTS - 1)
    n_used = block_end[-1:].astype(jnp.int32)
    return block_expert, n_used, plan, n_assign + 2 * MOE_ROWS


def _moe(x, x_packed, layer, w_router, router_bias, w_gate, w_up, w_down, ws_gate, ws_up, ws_down,
         g, beta):
    n_tok = x.shape[0]
    idx, wts, rank, cnt = _router(x, w_router.T, router_bias, tm=512)
    counts = cnt[:, 0].astype(jnp.int32)
    block_expert, n_used, plan, n_out_rows = _dispatch_plan(idx, rank, counts, n_tok)
    y = _experts(x_packed, layer, w_gate, w_up, w_down, block_expert, n_used, plan,
                 n_out_rows=n_out_rows)
    s = _shared(x, ws_gate.astype(BF16), ws_up.astype(BF16), ws_down.astype(BF16), tm=512)
    return _combine(s, wts.T, y, g, beta, tm=256)


def kernel(x, a_w_in, a_b_in, a_ln_g, a_ln_b, a_w_s, a_b_s, a_w_out, b_w_qkv, b_b_qkv, b_sinks,
           b_w_o, b_b_o, moe_w_router, moe_router_bias, moe_w_gate, moe_w_up, moe_w_down,
           moe_ws_gate, moe_ws_up, moe_ws_down, norm_g, norm_b):
    bsz, seq_len, d = x.shape
    h = x.reshape(bsz * seq_len, d)
    for i in range(DEPTH):
        j = i // 2
        if i % 2 == 0:
            z = _linear(h, a_w_in[j].astype(BF16), a_b_in[j], act="gelu", tm=512, tn=1024,
                        out_dtype=BF16)
            causal = jnp.tril(jnp.ones((A_CHUNK, A_CHUNK), dtype=bool))
            w_mix = jnp.where(causal[None], a_w_s[j], 0.0).astype(BF16)
            gated = _gmlp_gate(z, a_ln_g[j], a_ln_b[j], w_mix, a_b_s[j].T, tm=256)
            h, h_packed = _linear_res_ln(gated, a_w_out[j].astype(BF16), jnp.zeros((d,), F32), h,
                                         norm_g[i, 0], norm_b[i, 0], tm=512)
        else:
            qkv = _linear(h, b_w_qkv[j].astype(BF16), b_b_qkv[j], act=None, tm=512, tn=1280,
                          out_dtype=BF16)
            o = _attention(qkv, b_sinks[j], seq_len=seq_len)
            h, h_packed = _linear_res_ln(o, b_w_o[j].astype(BF16), b_b_o[j], h,
                                         norm_g[i, 0], norm_b[i, 0], tm=512)
        h = _moe(h, h_packed, i, moe_w_router[i], moe_router_bias[i], moe_w_gate, moe_w_up, moe_w_down,
                 moe_ws_gate[i], moe_ws_up[i], moe_ws_down[i], norm_g[i, 1], norm_b[i, 1])
    return h.reshape(bsz, seq_len, d)
```

```python
import functools

import jax
import jax.numpy as jnp
from jax import lax
from jax.experimental import pallas as pl
from jax.experimental.pallas import tpu as pltpu

F32 = jnp.float32
BF16 = jnp.bfloat16

D_MODEL = 2048
DEPTH = 2
A_CHUNK = 128
A_HALF = D_MODEL
A_GROUPS = A_HALF // 128
HEAD_DIM = 64
N_Q_HEADS = D_MODEL // HEAD_DIM
N_KV_HEADS = N_Q_HEADS // 8
Q_PER_KV = N_Q_HEADS // N_KV_HEADS
Q_DIM = N_Q_HEADS * HEAD_DIM
KV_DIM = N_KV_HEADS * HEAD_DIM
QKV_DIM = Q_DIM + 2 * KV_DIM
WINDOW = 128
ATTN_BLOCK = 128
N_EXPERTS = 64
TOP_K = 8
N_GROUPS = 8
GROUP_SIZE = N_EXPERTS // N_GROUPS
TOPK_GROUPS = 4
EXPERT_FF = D_MODEL // 4
ROUTED_SCALE = 2.5
DN_ALPHA = (2 * DEPTH) ** 0.25
LN_EPS = 1e-5

MOE_ROWS = 256
VMEM_LIMIT = 56 * 1024 * 1024


def _params(*sem):
    return pltpu.CompilerParams(dimension_semantics=sem, vmem_limit_bytes=VMEM_LIMIT)


def _layer_norm(y, g, b):
    mu = jnp.mean(y, axis=-1, keepdims=True)
    d = y - mu
    var = jnp.mean(d * d, axis=-1, keepdims=True)
    return d * lax.rsqrt(var + LN_EPS) * g + b


def _gelu(x):
    return 0.5 * x * (1.0 + lax.erf(x * (2.0 ** -0.5)))


def _silu(x):
    return x * jax.nn.sigmoid(x)


LANES = 128
PACK_SUB = D_MODEL // (2 * LANES)
HIGH_HALF = -65536


def _store_packed_rows(p_ref, rows_f32):
    n = rows_f32.shape[0]
    bits = lax.bitcast_convert_type(rows_f32.astype(BF16).astype(F32), jnp.int32)
    for s in range(PACK_SUB):
        hi = bits[:, s * LANES:(s + 1) * LANES] & HIGH_HALF
        lo = lax.shift_right_logical(
            bits[:, D_MODEL // 2 + s * LANES:D_MODEL // 2 + (s + 1) * LANES], 16)
        p_ref[pl.ds(s, n, stride=PACK_SUB), :] = hi | lo


def _load_packed_rows(p_ref, n):
    his, los = [], []
    for s in range(PACK_SUB):
        w = p_ref[pl.ds(s, n, stride=PACK_SUB), :]
        his.append(lax.bitcast_convert_type(w & HIGH_HALF, F32))
        los.append(lax.bitcast_convert_type(lax.shift_left(w, 16), F32))
    return his + los


def _linear_kernel(x_ref, w_ref, b_ref, o_ref, *, act):
    acc = jnp.dot(x_ref[...].astype(BF16), w_ref[...], preferred_element_type=F32)
    acc = acc + b_ref[...]
    if act == "gelu":
        acc = _gelu(acc)
    o_ref[...] = acc.astype(o_ref.dtype)


def _linear(x, w, b, *, act, tm, tn, out_dtype):
    m, k = x.shape
    n = w.shape[1]
    return pl.pallas_call(
        functools.partial(_linear_kernel, act=act),
        grid=(m // tm, n // tn),
        in_specs=[
            pl.BlockSpec((tm, k), lambda i, j: (i, 0)),
            pl.BlockSpec((k, tn), lambda i, j: (0, j)),
            pl.BlockSpec((1, tn), lambda i, j: (0, j)),
        ],
        out_specs=pl.BlockSpec((tm, tn), lambda i, j: (i, j)),
        out_shape=jax.ShapeDtypeStruct((m, n), out_dtype),
        name="linear_" + str(act),
        compiler_params=_params("parallel", "arbitrary"),
    )(x, w, b.reshape(1, n))


def _linear_res_ln_kernel(x_ref, w_ref, b_ref, res_ref, g_ref, beta_ref, o_ref, p_ref):
    acc = jnp.dot(x_ref[...], w_ref[...], preferred_element_type=F32)
    y = DN_ALPHA * res_ref[...] + (acc + b_ref[...])
    out = _layer_norm(y, g_ref[...], beta_ref[...])
    o_ref[...] = out
    _store_packed_rows(p_ref, out)


def _linear_res_ln(x, w, b, res, g, beta, *, tm):
    m, k = x.shape
    n = w.shape[1]
    row = lambda i: (i, 0)
    fixed = lambda i: (0, 0)
    return pl.pallas_call(
        _linear_res_ln_kernel,
        grid=(m // tm,),
        in_specs=[
            pl.BlockSpec((tm, k), row),
            pl.BlockSpec((k, n), fixed),
            pl.BlockSpec((1, n), fixed),
            pl.BlockSpec((tm, n), row),
            pl.BlockSpec((1, n), fixed),
            pl.BlockSpec((1, n), fixed),
        ],
        out_specs=[pl.BlockSpec((tm, n), row), pl.BlockSpec((tm * PACK_SUB, LANES), row)],
        out_shape=[jax.ShapeDtypeStruct((m, n), F32),
                   jax.ShapeDtypeStruct((m * PACK_SUB, LANES), jnp.int32)],
        name="linear_res_ln",
        compiler_params=_params("parallel"),
    )(x, w, b.reshape(1, n), res, g.reshape(1, n), beta.reshape(1, n))


def _gmlp_gate_kernel(z_ref, lng_ref, lnb_ref, wmix_ref, bst_ref, o_ref, *, tm):
    v = z_ref[:, A_HALF:].astype(F32)
    vn = _layer_norm(v, lng_ref[...], lnb_ref[...]).astype(BF16)
    for c in range(tm // A_CHUNK):
        rows = slice(c * A_CHUNK, (c + 1) * A_CHUNK)
        for g in range(A_GROUPS):
            cols = slice(g * 128, (g + 1) * 128)
            mixed = jnp.dot(wmix_ref[g], vn[rows, cols], preferred_element_type=F32)
            mixed = mixed + bst_ref[:, g:g + 1]
            u = z_ref[rows, cols].astype(F32)
            o_ref[rows, cols] = (u * mixed).astype(o_ref.dtype)


def _gmlp_gate(z, ln_g, ln_b, w_mix, b_s_t, *, tm):
    m = z.shape[0]
    return pl.pallas_call(
        functools.partial(_gmlp_gate_kernel, tm=tm),
        grid=(m // tm,),
        in_specs=[
            pl.BlockSpec((tm, 2 * A_HALF), lambda i: (i, 0)),
            pl.BlockSpec((1, A_HALF), lambda i: (0, 0)),
            pl.BlockSpec((1, A_HALF), lambda i: (0, 0)),
            pl.BlockSpec((A_GROUPS, A_CHUNK, A_CHUNK), lambda i: (0, 0, 0)),
            pl.BlockSpec((A_CHUNK, A_GROUPS), lambda i: (0, 0)),
        ],
        out_specs=pl.BlockSpec((tm, A_HALF), lambda i: (i, 0)),
        out_shape=jax.ShapeDtypeStruct((m, A_HALF), BF16),
        name="gmlp_gate",
        compiler_params=_params("parallel"),
    )(z, ln_g.reshape(1, A_HALF), ln_b.reshape(1, A_HALF), w_mix, b_s_t)


def _attn_kernel(sinks_ref, q_ref, kp_ref, kc_ref, vp_ref, vc_ref, bias_ref, o_ref, *,
                 blocks_per_seq):
    n = pl.program_id(0) % blocks_per_seq
    first = (n == 0).astype(jnp.int32)
    qi = lax.broadcasted_iota(jnp.int32, (ATTN_BLOCK, ATTN_BLOCK), 0)
    ci = lax.broadcasted_iota(jnp.int32, (ATTN_BLOCK, ATTN_BLOCK), 1)
    upper = ci > qi
    low_half = lax.broadcasted_iota(jnp.int32, (ATTN_BLOCK, 2 * HEAD_DIM), 1) < HEAD_DIM
    ones = jnp.ones((2 * ATTN_BLOCK, HEAD_DIM), BF16)
    for g in range(N_KV_HEADS):
        kv_cols = slice(g * HEAD_DIM, (g + 1) * HEAD_DIM)
        kk = jnp.concatenate([kp_ref[:, kv_cols], kc_ref[:, kv_cols]], axis=0)
        vv = jnp.concatenate([vp_ref[:, kv_cols], vc_ref[:, kv_cols]], axis=0)
        vv_ones = jnp.concatenate([vv, ones, ones, vv], axis=1)
        heads = [g * Q_PER_KV + j for j in range(Q_PER_KV)]
        qg = jnp.concatenate([q_ref[:, h * HEAD_DIM:(h + 1) * HEAD_DIM] for h in heads], axis=0)
        s = lax.dot_general(qg, kk, (((1,), (1,)), ((), ())), preferred_element_type=F32)
        es, maxes = [], []
        for j, h in enumerate(heads):
            sj = s[j * ATTN_BLOCK:(j + 1) * ATTN_BLOCK]
            sm = jnp.where(upper, sj[:, :ATTN_BLOCK], sj[:, ATTN_BLOCK:]) + bias_ref[first, h]
            mx = jnp.maximum(jnp.max(sm, axis=-1, keepdims=True), sinks_ref[h])
            e = jnp.exp(sm - mx)
            es.append(jnp.concatenate([jnp.where(upper, e, 0.0), jnp.where(upper, 0.0, e)],
                                      axis=1).astype(BF16))
            maxes.append(mx)
        o = jnp.dot(jnp.concatenate(es, axis=0), vv_ones, preferred_element_type=F32)
        for j in range(0, Q_PER_KV, 2):
            even = o[j * ATTN_BLOCK:(j + 1) * ATTN_BLOCK]
            odd = o[(j + 1) * ATTN_BLOCK:(j + 2) * ATTN_BLOCK]
            num = jnp.where(low_half, even[:, :2 * HEAD_DIM], odd[:, 2 * HEAD_DIM:])
            total = jnp.where(low_half, even[:, 2 * HEAD_DIM:], odd[:, :2 * HEAD_DIM])
            sink_e = jnp.where(low_half, jnp.exp(sinks_ref[heads[j]] - maxes[j]),
                               jnp.exp(sinks_ref[heads[j + 1]] - maxes[j + 1]))
            cols = slice(heads[j] * HEAD_DIM, (heads[j] + 2) * HEAD_DIM)
            o_ref[:, cols] = (num * (1.0 / (total + sink_e))).astype(o_ref.dtype)


def _attention_bias():
    qi = jnp.arange(ATTN_BLOCK, dtype=jnp.int32)[:, None]
    kc = jnp.arange(ATTN_BLOCK, dtype=jnp.int32)[None, :]
    upper = kc > qi
    dist = (qi - kc + jnp.where(upper, ATTN_BLOCK, 0)).astype(F32)
    head = jnp.arange(1, N_Q_HEADS + 1, dtype=F32)
    slopes = jnp.exp2(-8.0 * head / N_Q_HEADS)
    alibi = -slopes[:, None, None] * dist[None]
    return jnp.stack([alibi, jnp.where(upper[None], -jnp.inf, alibi)])


def _attention(qkv, sinks, *, seq_len):
    assert WINDOW == ATTN_BLOCK
    t = qkv.shape[0]
    blocks_per_seq = seq_len // ATTN_BLOCK
    kcol = Q_DIM // KV_DIM
    vcol = kcol + 1
    prev = lambda i: jnp.maximum(i - 1, 0)
    grid_spec = pltpu.PrefetchScalarGridSpec(
        num_scalar_prefetch=1,
        grid=(t // ATTN_BLOCK,),
        in_specs=[
            pl.BlockSpec((ATTN_BLOCK, Q_DIM), lambda i, s: (i, 0)),
            pl.BlockSpec((ATTN_BLOCK, KV_DIM), lambda i, s: (prev(i), kcol)),
            pl.BlockSpec((ATTN_BLOCK, KV_DIM), lambda i, s: (i, kcol)),
            pl.BlockSpec((ATTN_BLOCK, KV_DIM), lambda i, s: (prev(i), vcol)),
            pl.BlockSpec((ATTN_BLOCK, KV_DIM), lambda i, s: (i, vcol)),
            pl.BlockSpec((2, N_Q_HEADS, ATTN_BLOCK, ATTN_BLOCK), lambda i, s: (0, 0, 0, 0)),
        ],
        out_specs=pl.BlockSpec((ATTN_BLOCK, Q_DIM), lambda i, s: (i, 0)),
    )
    return pl.pallas_call(
        functools.partial(_attn_kernel, blocks_per_seq=blocks_per_seq),
        grid_spec=grid_spec,
        out_shape=jax.ShapeDtypeStruct((t, Q_DIM), BF16),
        name="swa_attention",
        compiler_params=_params("parallel"),
    )(sinks, qkv, qkv, qkv, qkv, qkv, _attention_bias())


def _router_kernel(x_ref, wr_ref, rb_ref, idx_ref, wts_ref, rank_ref, cnt_ref, carry_ref, *, tm):
    @pl.when(pl.program_id(0) == 0)
    def _():
        carry_ref[...] = jnp.zeros_like(carry_ref)

    x = x_ref[...]
    x_hi = x.astype(BF16)
    x_lo = (x - x_hi.astype(F32)).astype(BF16)
    w_split = wr_ref[...]
    by_hi = jnp.dot(x_hi, w_split, preferred_element_type=F32)
    by_lo = jnp.dot(x_lo, w_split[:, :LANES], preferred_element_type=F32)
    logits = by_hi[:, :LANES] + (by_hi[:, LANES:] + by_lo)
    sc = jax.nn.sigmoid(logits.T[:N_EXPERTS])
    sel = sc + rb_ref[...]
    neg = -jnp.inf
    iota_g = lax.broadcasted_iota(jnp.int32, (GROUP_SIZE, tm), 0).astype(F32)

    def first_argmax(v, iota, size):
        m = jnp.max(v, axis=0, keepdims=True)
        return jnp.min(jnp.where(v == m, iota, float(size)), axis=0, keepdims=True)

    group_rows = []
    for g in range(N_GROUPS):
        v = sel[g * GROUP_SIZE:(g + 1) * GROUP_SIZE]
        m1 = jnp.max(v, axis=0, keepdims=True)
        i1 = first_argmax(v, iota_g, GROUP_SIZE)
        m2 = jnp.max(jnp.where(iota_g == i1, neg, v), axis=0, keepdims=True)
        group_rows.append(m1 + m2)
    cur = jnp.concatenate(group_rows, axis=0)
    iota_n = lax.broadcasted_iota(jnp.int32, (N_GROUPS, tm), 0).astype(F32)
    gsel = jnp.zeros((N_GROUPS, tm), F32)
    for _ in range(TOPK_GROUPS):
        hit = iota_n == first_argmax(cur, iota_n, N_GROUPS)
        gsel = jnp.where(hit, 1.0, gsel)
        cur = jnp.where(hit, neg, cur)
    cur = jnp.concatenate(
        [jnp.where(gsel[g:g + 1] > 0.5, sel[g * GROUP_SIZE:(g + 1) * GROUP_SIZE], neg)
         for g in range(N_GROUPS)], axis=0)

    iota_e = lax.broadcasted_iota(jnp.int32, (N_EXPERTS, tm), 0).astype(F32)
    member = jnp.zeros((N_EXPERTS, tm), F32)
    hits, idx_rows, w_rows = [], [], []
    for _ in range(TOP_K):
        ii = first_argmax(cur, iota_e, N_EXPERTS)
        hit = iota_e == ii
        hits.append(hit)
        idx_rows.append(ii)
        w_rows.append(jnp.sum(jnp.where(hit, sc, 0.0), axis=0, keepdims=True))
        member = jnp.where(hit, 1.0, member)
        cur = jnp.where(hit, neg, cur)
    w = jnp.concatenate(w_rows, axis=0)
    w = w / (jnp.sum(w, axis=0, keepdims=True) + 1e-20) * ROUTED_SCALE

    r = lax.broadcasted_iota(jnp.int32, (tm, tm), 0)
    c = lax.broadcasted_iota(jnp.int32, (tm, tm), 1)
    upper = jnp.where(r < c, 1.0, 0.0).astype(BF16)
    before = jnp.dot(member.astype(BF16), upper, preferred_element_type=F32)
    rank_full = carry_ref[...] + before
    rank_rows = [jnp.sum(jnp.where(h, rank_full, 0.0), axis=0, keepdims=True) for h in hits]

    idx_ref[...] = jnp.concatenate(idx_rows, axis=0).astype(jnp.int32)
    wts_ref[...] = w
    rank_ref[...] = jnp.concatenate(rank_rows, axis=0).astype(jnp.int32)
    carry_ref[...] = carry_ref[...] + jnp.sum(member, axis=1, keepdims=True)
    cnt_ref[...] = jnp.broadcast_to(carry_ref[...], cnt_ref.shape)


def _router(x, w_router, router_bias, *, tm):
    t, d = x.shape
    tok = lambda i: (0, i)
    w_padded = jnp.pad(w_router, ((0, 0), (0, LANES - N_EXPERTS)))
    w_hi = w_padded.astype(BF16)
    w_lo = (w_padded - w_hi.astype(F32)).astype(BF16)
    w_split = jnp.concatenate([w_hi, w_lo], axis=1)
    return pl.pallas_call(
        functools.partial(_router_kernel, tm=tm),
        grid=(t // tm,),
        in_specs=[
            pl.BlockSpec((tm, d), lambda i: (i, 0)),
            pl.BlockSpec((d, 2 * LANES), lambda i: (0, 0)),
            pl.BlockSpec((N_EXPERTS, 1), lambda i: (0, 0)),
        ],
        out_specs=[
            pl.BlockSpec((TOP_K, tm), tok),
            pl.BlockSpec((TOP_K, tm), tok),
            pl.BlockSpec((TOP_K, tm), tok),
            pl.BlockSpec((N_EXPERTS, 128), lambda i: (0, 0)),
        ],
        out_shape=[
            jax.ShapeDtypeStruct((TOP_K, t), jnp.int32),
            jax.ShapeDtypeStruct((TOP_K, t), F32),
            jax.ShapeDtypeStruct((TOP_K, t), jnp.int32),
            jax.ShapeDtypeStruct((N_EXPERTS, 128), F32),
        ],
        scratch_shapes=[pltpu.VMEM((N_EXPERTS, 1), F32)],
        name="router",
        compiler_params=_params("arbitrary"),
    )(x, w_split, router_bias.reshape(N_EXPERTS, 1))


def _experts_kernel(be_ref, nu_ref, plan_hbm, x_hbm, wg_ref, wu_ref, wd_ref, y_hbm,
                    plan_smem, xbuf0, xbuf1, ybuf0, ybuf1, wg_bf, wu_bf, wd_bf,
                    psem, gsem, ssem, *, tok_mask):
    b = pl.program_id(0)
    n_used = nu_ref[0]
    rows = MOE_ROWS
    xbuf = (xbuf0, xbuf1)
    ybuf = (ybuf0, ybuf1)

    def plan_copy(row, slot):
        return pltpu.make_async_copy(plan_hbm.at[row], plan_smem.at[slot], psem.at[slot])

    def start_gathers(pslot, slot):
        for r in range(rows):
            src = pl.multiple_of(plan_smem[pslot, r] & tok_mask, PACK_SUB)
            pltpu.make_async_copy(x_hbm.at[pl.ds(src, PACK_SUB)],
                                  xbuf[slot].at[pl.ds(r * PACK_SUB, PACK_SUB)],
                                  gsem.at[slot]).start()

    def start_scatters(pslot, slot):
        for r in range(rows):
            dst = pl.multiple_of(plan_smem[pslot, rows + r], PACK_SUB)
            pltpu.make_async_copy(ybuf[slot].at[pl.ds(r * PACK_SUB, PACK_SUB)],
                                  y_hbm.at[pl.ds(dst, PACK_SUB)],
                                  ssem.at[slot]).start()

    def wait_rows(buf, sem):
        pltpu.make_async_copy(buf, buf, sem).wait()

    @pl.when(b == 0)
    def _():
        ybuf0[...] = jnp.zeros_like(ybuf0)
        ybuf1[...] = jnp.zeros_like(ybuf1)
        spare0 = y_hbm.shape[0] - 2 * rows * PACK_SUB
        pltpu.make_async_copy(ybuf0, y_hbm.at[pl.ds(spare0, rows * PACK_SUB)], ssem.at[0]).start()
        plan_copy(0, 0).start()
        plan_copy(0, 0).wait()
        start_gathers(0, 0)
        plan_copy(1, 1).start()

    def step(slot):
        other = 1 - slot
        changed = jnp.logical_or(b == 0, be_ref[b] != be_ref[jnp.maximum(b - 1, 0)])

        @pl.when(changed)
        def _():
            wg_bf[...] = wg_ref[...].astype(BF16)
            wu_bf[...] = wu_ref[...].astype(BF16)
            wd_bf[...] = wd_ref[...].astype(BF16)

        wait_rows(xbuf[slot], gsem.at[slot])
        wait_rows(ybuf[slot], ssem.at[slot])
        plan_copy(b + 1, other).wait()
        start_gathers(other, other)
        start_scatters(other, other)
        plan_copy(b + 2, slot).start()

        x = jnp.concatenate([p.astype(BF16) for p in _load_packed_rows(xbuf[slot], rows)], axis=1)
        gate = jnp.dot(x, wg_bf[...], preferred_element_type=F32)
        up = jnp.dot(x, wu_bf[...], preferred_element_type=F32)
        h = (_silu(gate) * up).astype(BF16)
        _store_packed_rows(ybuf[slot], jnp.dot(h, wd_bf[...], preferred_element_type=F32))

        @pl.when(b == n_used - 1)
        def _():
            plan_copy(b + 2, slot).wait()
            start_scatters(slot, slot)
            wait_rows(xbuf[other], gsem.at[other])
            wait_rows(ybuf[other], ssem.at[other])
            wait_rows(ybuf[slot], ssem.at[slot])

    for parity in range(2):
        pl.when(jnp.logical_and(b < n_used, b % 2 == parity))(functools.partial(step, parity))


def _experts(x_packed, layer, w_gate, w_up, w_down, block_expert, n_used, plan, *, n_out_rows):
    t = x_packed.shape[0] // PACK_SUB
    d = D_MODEL
    assert t & (t - 1) == 0, "token id is taken as the low bits of the flat (choice, token) index"
    n_blocks = plan.shape[0] - 2
    wspec_in = pl.BlockSpec((None, None, d, EXPERT_FF), lambda b, be, nu: (layer, be[b], 0, 0))
    wspec_out = pl.BlockSpec((None, None, EXPERT_FF, d), lambda b, be, nu: (layer, be[b], 0, 0))
    row_buf = pltpu.VMEM((MOE_ROWS * PACK_SUB, LANES), jnp.int32)
    grid_spec = pltpu.PrefetchScalarGridSpec(
        num_scalar_prefetch=2,
        grid=(n_blocks,),
        in_specs=[
            pl.BlockSpec(memory_space=pl.ANY),
            pl.BlockSpec(memory_space=pl.ANY),
            wspec_in, wspec_in, wspec_out,
        ],
        out_specs=pl.BlockSpec(memory_space=pl.ANY),
        scratch_shapes=[
            pltpu.SMEM((2, 2 * MOE_ROWS), jnp.int32),
            row_buf, row_buf, row_buf, row_buf,
            pltpu.VMEM((d, EXPERT_FF), BF16),
            pltpu.VMEM((d, EXPERT_FF), BF16),
            pltpu.VMEM((EXPERT_FF, d), BF16),
            pltpu.SemaphoreType.DMA((2,)),
            pltpu.SemaphoreType.DMA((2,)),
            pltpu.SemaphoreType.DMA((2,)),
        ],
    )
    return pl.pallas_call(
        functools.partial(_experts_kernel, tok_mask=(t - 1) * PACK_SUB),
        grid_spec=grid_spec,
        out_shape=jax.ShapeDtypeStruct((n_out_rows * PACK_SUB, LANES), jnp.int32),
        name="routed_experts",
        compiler_params=_params("arbitrary"),
    )(block_expert, n_used, plan, x_packed, w_gate, w_up, w_down)


def _shared_kernel(x_ref, wg_ref, wu_ref, wd_ref, o_ref):
    x = x_ref[...]
    xb = x.astype(BF16)
    gate = jnp.dot(xb, wg_ref[...], preferred_element_type=F32)
    up = jnp.dot(xb, wu_ref[...], preferred_element_type=F32)
    h = (_silu(gate) * up).astype(BF16)
    o_ref[...] = DN_ALPHA * x + jnp.dot(h, wd_ref[...], preferred_element_type=F32)


def _shared(x, wg, wu, wd, *, tm):
    t, d = x.shape
    f = wg.shape[1]
    return pl.pallas_call(
        _shared_kernel,
        grid=(t // tm,),
        in_specs=[
            pl.BlockSpec((tm, d), lambda i: (i, 0)),
            pl.BlockSpec((d, f), lambda i: (0, 0)),
            pl.BlockSpec((d, f), lambda i: (0, 0)),
            pl.BlockSpec((f, d), lambda i: (0, 0)),
        ],
        out_specs=pl.BlockSpec((tm, d), lambda i: (i, 0)),
        out_shape=jax.ShapeDtypeStruct((t, d), F32),
        name="shared_expert",
        compiler_params=_params("parallel"),
    )(x, wg, wu, wd)


def _combine_kernel(s_ref, w_ref, *refs, tm):
    y_refs = refs[:TOP_K]
    g_ref, beta_ref, o_ref = refs[TOP_K:]
    pieces = [s_ref[:, c * LANES:(c + 1) * LANES] for c in range(2 * PACK_SUB)]
    for k in range(TOP_K):
        wk = w_ref[:, k:k + 1]
        yk = _load_packed_rows(y_refs[k], tm)
        pieces = [p + wk * y for p, y in zip(pieces, yk)]
    o_ref[...] = _layer_norm(jnp.concatenate(pieces, axis=1), g_ref[...], beta_ref[...])


def _combine(s, wts_t, y, g, beta, *, tm):
    t, d = s.shape
    blocks = t // tm
    y_specs = [pl.BlockSpec((tm * PACK_SUB, LANES),
                            functools.partial(lambda i, k: (k * blocks + i, 0), k=k))
               for k in range(TOP_K)]
    return pl.pallas_call(
        functools.partial(_combine_kernel, tm=tm),
        grid=(blocks,),
        in_specs=[
            pl.BlockSpec((tm, d), lambda i: (i, 0)),
            pl.BlockSpec((tm, TOP_K), lambda i: (i, 0)),
            *y_specs,
            pl.BlockSpec((1, d), lambda i: (0, 0)),
            pl.BlockSpec((1, d), lambda i: (0, 0)),
        ],
        out_specs=pl.BlockSpec((tm, d), lambda i: (i, 0)),
        out_shape=jax.ShapeDtypeStruct((t, d), F32),
        name="moe_combine",
        compiler_params=_params("parallel"),
    )(s, wts_t, *([y] * TOP_K), g.reshape(1, d), beta.reshape(1, d))


def _dispatch_plan(idx, rank, counts, n_tok):
    n_assign = n_tok * TOP_K
    n_blocks = n_assign // MOE_ROWS + N_EXPERTS
    n_slots = n_blocks * MOE_ROWS
    blocks_per_expert = (counts + MOE_ROWS - 1) // MOE_ROWS
    block_end = jnp.cumsum(blocks_per_expert)
    row_start = (block_end - blocks_per_expert) * MOE_ROWS
    experts = jnp.arange(N_EXPERTS, dtype=jnp.int32)
    row_start_of = jnp.sum(jnp.where(idx[..., None] == experts, row_start, 0), axis=-1)
    dest = (row_start_of + rank).reshape(-1)
    slot = jnp.arange(n_slots, dtype=jnp.int32)
    spare = n_assign + ((slot // MOE_ROWS) % 2) * MOE_ROWS + slot % MOE_ROWS
    flat = jnp.arange(n_assign, dtype=jnp.int32)
    slot_flat = spare.at[dest].set(flat, unique_indices=True).reshape(n_blocks, MOE_ROWS)
    rows = slot_flat * PACK_SUB
    pad = jnp.broadcast_to((n_assign + MOE_ROWS + slot[:MOE_ROWS]) * PACK_SUB, (2, MOE_ROWS))
    plan = jnp.concatenate([jnp.concatenate([rows, pad], axis=0),
                            jnp.concatenate([pad, rows], axis=0)], axis=1)
    blocks = jnp.arange(n_blocks, dtype=jnp.int32)
    block_expert = jnp.minimum(
        jnp.sum((block_end[None, :] <= blocks[:, None]).astype(jnp.int32), axis=1), N_EXPERTS - 1)
    n_used = block_end[-1:].astype(jnp.int32)
    return block_expert, n_used, plan, n_assign + 2 * MOE_ROWS


def _moe(x, x_packed, layer, w_router, router_bias, w_gate, w_up, w_down, ws_gate, ws_up, ws_down,
         g, beta):
    n_tok = x.shape[0]
    idx, wts, rank, cnt = _router(x, w_router, router_bias, tm=512)
    counts = cnt[:, 0].astype(jnp.int32)
    block_expert, n_used, plan, n_out_rows = _dispatch_plan(idx, rank, counts, n_tok)
    y = _experts(x_packed, layer, w_gate, w_up, w_down, block_expert, n_used, plan,
                 n_out_rows=n_out_rows)
    s = _shared(x, ws_gate.astype(BF16), ws_up.astype(BF16), ws_down.astype(BF16), tm=512)
    return _combine(s, wts.T, y, g, beta, tm=256)


def kernel(x, a_w_in, a_b_in, a_ln_g, a_ln_b, a_w_s, a_b_s, a_w_out, b_w_qkv, b_b_qkv, b_sinks,
           b_w_o, b_b_o, moe_w_router, moe_router_bias, moe_w_gate, moe_w_up, moe_w_down,
           moe_ws_gate, moe_ws_up, moe_ws_down, norm_g, norm_b):
    bsz, seq_len, d = x.shape
    h = x.reshape(bsz * seq_len, d)
    for i in range(DEPTH):
        j = i // 2
        if i % 2 == 0:
            z = _linear(h, a_w_in[j].astype(BF16), a_b_in[j], act="gelu", tm=1024, tn=1024,
                        out_dtype=BF16)
            causal = jnp.tril(jnp.ones((A_CHUNK, A_CHUNK), dtype=bool))
            w_mix = jnp.where(causal[None], a_w_s[j], 0.0).astype(BF16)
            gated = _gmlp_gate(z, a_ln_g[j], a_ln_b[j], w_mix, a_b_s[j].T, tm=256)
            h, h_packed = _linear_res_ln(gated, a_w_out[j].astype(BF16), jnp.zeros((d,), F32), h,
                                         norm_g[i, 0], norm_b[i, 0], tm=512)
        else:
            q_scale = jnp.where(jnp.arange(QKV_DIM) < Q_DIM, HEAD_DIM ** -0.5, 1.0).astype(F32)
            qkv = _linear(h, (b_w_qkv[j] * q_scale).astype(BF16), b_b_qkv[j] * q_scale, act=None,
                          tm=1024, tn=1280, out_dtype=BF16)
            o = _attention(qkv, b_sinks[j], seq_len=seq_len)
            h, h_packed = _linear_res_ln(o, b_w_o[j].astype(BF16), b_b_o[j], h,
                                         norm_g[i, 0], norm_b[i, 0], tm=512)
        h = _moe(h, h_packed, i, moe_w_router[i], moe_router_bias[i], moe_w_gate, moe_w_up, moe_w_down,
                 moe_ws_gate[i], moe_ws_up[i], moe_ws_down[i], norm_g[i, 1], norm_b[i, 1])
    return h.reshape(bsz, seq_len, d)
```

```python
import functools

import jax
import jax.numpy as jnp
from jax import lax
from jax.experimental import pallas as pl
from jax.experimental.pallas import tpu as pltpu

F32 = jnp.float32
BF16 = jnp.bfloat16

D_MODEL = 2048
DEPTH = 2
A_CHUNK = 128
A_HALF = D_MODEL
A_GROUPS = A_HALF // 128
HEAD_DIM = 64
N_Q_HEADS = D_MODEL // HEAD_DIM
N_KV_HEADS = N_Q_HEADS // 8
Q_PER_KV = N_Q_HEADS // N_KV_HEADS
Q_DIM = N_Q_HEADS * HEAD_DIM
KV_DIM = N_KV_HEADS * HEAD_DIM
QKV_DIM = Q_DIM + 2 * KV_DIM
WINDOW = 128
ATTN_BLOCK = 128
N_EXPERTS = 64
TOP_K = 8
N_GROUPS = 8
GROUP_SIZE = N_EXPERTS // N_GROUPS
TOPK_GROUPS = 4
EXPERT_FF = D_MODEL // 4
ROUTED_SCALE = 2.5
DN_ALPHA = (2 * DEPTH) ** 0.25
LN_EPS = 1e-5

MOE_ROWS = 256
VMEM_LIMIT = 56 * 1024 * 1024


def _params(*sem):
    return pltpu.CompilerParams(dimension_semantics=sem, vmem_limit_bytes=VMEM_LIMIT)


def _layer_norm(y, g, b):
    mu = jnp.mean(y, axis=-1, keepdims=True)
    d = y - mu
    var = jnp.mean(d * d, axis=-1, keepdims=True)
    return d * lax.rsqrt(var + LN_EPS) * g + b


def _gelu(x):
    return 0.5 * x * (1.0 + lax.erf(x * (2.0 ** -0.5)))


def _silu(x):
    return x * jax.nn.sigmoid(x)


LANES = 128
PACK_SUB = D_MODEL // (2 * LANES)
HIGH_HALF = -65536


def _store_packed_rows(p_ref, rows_f32):
    n = rows_f32.shape[0]
    bits = lax.bitcast_convert_type(rows_f32.astype(BF16).astype(F32), jnp.int32)
    for s in range(PACK_SUB):
        hi = bits[:, s * LANES:(s + 1) * LANES] & HIGH_HALF
        lo = lax.shift_right_logical(
            bits[:, D_MODEL // 2 + s * LANES:D_MODEL // 2 + (s + 1) * LANES], 16)
        p_ref[pl.ds(s, n, stride=PACK_SUB), :] = hi | lo


def _load_packed_rows(p_ref, n):
    his, los = [], []
    for s in range(PACK_SUB):
        w = p_ref[pl.ds(s, n, stride=PACK_SUB), :]
        his.append(lax.bitcast_convert_type(w & HIGH_HALF, F32))
        los.append(lax.bitcast_convert_type(lax.shift_left(w, 16), F32))
    return his + los


def _linear_kernel(x_ref, w_ref, b_ref, o_ref, *, act):
    acc = jnp.dot(x_ref[...].astype(BF16), w_ref[...], preferred_element_type=F32)
    acc = acc + b_ref[...]
    if act == "gelu":
        acc = _gelu(acc)
    o_ref[...] = acc.astype(o_ref.dtype)


def _linear(x, w, b, *, act, tm, tn, out_dtype):
    m, k = x.shape
    n = w.shape[1]
    return pl.pallas_call(
        functools.partial(_linear_kernel, act=act),
        grid=(m // tm, n // tn),
        in_specs=[
            pl.BlockSpec((tm, k), lambda i, j: (i, 0)),
            pl.BlockSpec((k, tn), lambda i, j: (0, j)),
            pl.BlockSpec((1, tn), lambda i, j: (0, j)),
        ],
        out_specs=pl.BlockSpec((tm, tn), lambda i, j: (i, j)),
        out_shape=jax.ShapeDtypeStruct((m, n), out_dtype),
        name="linear_" + str(act),
        compiler_params=_params("parallel", "arbitrary"),
    )(x, w, b.reshape(1, n))


def _linear_res_ln_kernel(x_ref, w_ref, b_ref, res_ref, g_ref, beta_ref, o_ref, p_ref):
    acc = jnp.dot(x_ref[...], w_ref[...], preferred_element_type=F32)
    y = DN_ALPHA * res_ref[...] + (acc + b_ref[...])
    out = _layer_norm(y, g_ref[...], beta_ref[...])
    o_ref[...] = out
    _store_packed_rows(p_ref, out)


def _linear_res_ln(x, w, b, res, g, beta, *, tm):
    m, k = x.shape
    n = w.shape[1]
    row = lambda i: (i, 0)
    fixed = lambda i: (0, 0)
    return pl.pallas_call(
        _linear_res_ln_kernel,
        grid=(m // tm,),
        in_specs=[
            pl.BlockSpec((tm, k), row),
            pl.BlockSpec((k, n), fixed),
            pl.BlockSpec((1, n), fixed),
            pl.BlockSpec((tm, n), row),
            pl.BlockSpec((1, n), fixed),
            pl.BlockSpec((1, n), fixed),
        ],
        out_specs=[pl.BlockSpec((tm, n), row), pl.BlockSpec((tm * PACK_SUB, LANES), row)],
        out_shape=[jax.ShapeDtypeStruct((m, n), F32),
                   jax.ShapeDtypeStruct((m * PACK_SUB, LANES), jnp.int32)],
        name="linear_res_ln",
        compiler_params=_params("parallel"),
    )(x, w, b.reshape(1, n), res, g.reshape(1, n), beta.reshape(1, n))


def _gmlp_gate_kernel(z_ref, lng_ref, lnb_ref, wmix_ref, bst_ref, o_ref, *, tm):
    v = z_ref[:, A_HALF:].astype(F32)
    vn = _layer_norm(v, lng_ref[...], lnb_ref[...]).astype(BF16)
    for c in range(tm // A_CHUNK):
        rows = slice(c * A_CHUNK, (c + 1) * A_CHUNK)
        for g in range(A_GROUPS):
            cols = slice(g * 128, (g + 1) * 128)
            mixed = jnp.dot(wmix_ref[g], vn[rows, cols], preferred_element_type=F32)
            mixed = mixed + bst_ref[:, g:g + 1]
            u = z_ref[rows, cols].astype(F32)
            o_ref[rows, cols] = (u * mixed).astype(o_ref.dtype)


def _gmlp_gate(z, ln_g, ln_b, w_mix, b_s_t, *, tm):
    m = z.shape[0]
    return pl.pallas_call(
        functools.partial(_gmlp_gate_kernel, tm=tm),
        grid=(m // tm,),
        in_specs=[
            pl.BlockSpec((tm, 2 * A_HALF), lambda i: (i, 0)),
            pl.BlockSpec((1, A_HALF), lambda i: (0, 0)),
            pl.BlockSpec((1, A_HALF), lambda i: (0, 0)),
            pl.BlockSpec((A_GROUPS, A_CHUNK, A_CHUNK), lambda i: (0, 0, 0)),
            pl.BlockSpec((A_CHUNK, A_GROUPS), lambda i: (0, 0)),
        ],
        out_specs=pl.BlockSpec((tm, A_HALF), lambda i: (i, 0)),
        out_shape=jax.ShapeDtypeStruct((m, A_HALF), BF16),
        name="gmlp_gate",
        compiler_params=_params("parallel"),
    )(z, ln_g.reshape(1, A_HALF), ln_b.reshape(1, A_HALF), w_mix, b_s_t)


def _attn_kernel(sinks_ref, q_ref, kp_ref, kc_ref, vp_ref, vc_ref, bias_ref, o_ref, *,
                 blocks_per_seq):
    n = pl.program_id(0) % blocks_per_seq
    first = (n == 0).astype(jnp.int32)
    qi = lax.broadcasted_iota(jnp.int32, (ATTN_BLOCK, ATTN_BLOCK), 0)
    ci = lax.broadcasted_iota(jnp.int32, (ATTN_BLOCK, ATTN_BLOCK), 1)
    upper = ci > qi
    low_half = lax.broadcasted_iota(jnp.int32, (ATTN_BLOCK, 2 * HEAD_DIM), 1) < HEAD_DIM
    ones = jnp.ones((2 * ATTN_BLOCK, HEAD_DIM), BF16)
    for g in range(N_KV_HEADS):
        kv_cols = slice(g * HEAD_DIM, (g + 1) * HEAD_DIM)
        kk = jnp.concatenate([kp_ref[:, kv_cols], kc_ref[:, kv_cols]], axis=0)
        vv = jnp.concatenate([vp_ref[:, kv_cols], vc_ref[:, kv_cols]], axis=0)
        vv_ones = jnp.concatenate([vv, ones, ones, vv], axis=1)
        heads = [g * Q_PER_KV + j for j in range(Q_PER_KV)]
        qg = jnp.concatenate([q_ref[:, h * HEAD_DIM:(h + 1) * HEAD_DIM] for h in heads], axis=0)
        s = lax.dot_general(qg, kk, (((1,), (1,)), ((), ())), preferred_element_type=F32)
        es, maxes = [], []
        for j, h in enumerate(heads):
            sj = s[j * ATTN_BLOCK:(j + 1) * ATTN_BLOCK]
            sm = jnp.where(upper, sj[:, :ATTN_BLOCK], sj[:, ATTN_BLOCK:]) + bias_ref[first, h]
            mx = jnp.maximum(jnp.max(sm, axis=-1, keepdims=True), sinks_ref[h])
            e = jnp.exp(sm - mx)
            es.append(jnp.concatenate([jnp.where(upper, e, 0.0), jnp.where(upper, 0.0, e)],
                                      axis=1).astype(BF16))
            maxes.append(mx)
        o = jnp.dot(jnp.concatenate(es, axis=0), vv_ones, preferred_element_type=F32)
        for j in range(0, Q_PER_KV, 2):
            even = o[j * ATTN_BLOCK:(j + 1) * ATTN_BLOCK]
            odd = o[(j + 1) * ATTN_BLOCK:(j + 2) * ATTN_BLOCK]
            num = jnp.where(low_half, even[:, :2 * HEAD_DIM], odd[:, 2 * HEAD_DIM:])
            total = jnp.where(low_half, even[:, 2 * HEAD_DIM:], odd[:, :2 * HEAD_DIM])
            sink_e = jnp.where(low_half, jnp.exp(sinks_ref[heads[j]] - maxes[j]),
                               jnp.exp(sinks_ref[heads[j + 1]] - maxes[j + 1]))
            cols = slice(heads[j] * HEAD_DIM, (heads[j] + 2) * HEAD_DIM)
            o_ref[:, cols] = (num * (1.0 / (total + sink_e))).astype(o_ref.dtype)


def _attention_bias():
    qi = jnp.arange(ATTN_BLOCK, dtype=jnp.int32)[:, None]
    kc = jnp.arange(ATTN_BLOCK, dtype=jnp.int32)[None, :]
    upper = kc > qi
    dist = (qi - kc + jnp.where(upper, ATTN_BLOCK, 0)).astype(F32)
    head = jnp.arange(1, N_Q_HEADS + 1, dtype=F32)
    slopes = jnp.exp2(-8.0 * head / N_Q_HEADS)
    alibi = -slopes[:, None, None] * dist[None]
    return jnp.stack([alibi, jnp.where(upper[None], -jnp.inf, alibi)])


def _attention(qkv, sinks, *, seq_len):
    assert WINDOW == ATTN_BLOCK
    t = qkv.shape[0]
    blocks_per_seq = seq_len // ATTN_BLOCK
    kcol = Q_DIM // KV_DIM
    vcol = kcol + 1
    prev = lambda i: jnp.maximum(i - 1, 0)
    grid_spec = pltpu.PrefetchScalarGridSpec(
        num_scalar_prefetch=1,
        grid=(t // ATTN_BLOCK,),
        in_specs=[
            pl.BlockSpec((ATTN_BLOCK, Q_DIM), lambda i, s: (i, 0)),
            pl.BlockSpec((ATTN_BLOCK, KV_DIM), lambda i, s: (prev(i), kcol)),
            pl.BlockSpec((ATTN_BLOCK, KV_DIM), lambda i, s: (i, kcol)),
            pl.BlockSpec((ATTN_BLOCK, KV_DIM), lambda i, s: (prev(i), vcol)),
            pl.BlockSpec((ATTN_BLOCK, KV_DIM), lambda i, s: (i, vcol)),
            pl.BlockSpec((2, N_Q_HEADS, ATTN_BLOCK, ATTN_BLOCK), lambda i, s: (0, 0, 0, 0)),
        ],
        out_specs=pl.BlockSpec((ATTN_BLOCK, Q_DIM), lambda i, s: (i, 0)),
    )
    return pl.pallas_call(
        functools.partial(_attn_kernel, blocks_per_seq=blocks_per_seq),
        grid_spec=grid_spec,
        out_shape=jax.ShapeDtypeStruct((t, Q_DIM), BF16),
        name="swa_attention",
        compiler_params=_params("parallel"),
    )(sinks, qkv, qkv, qkv, qkv, qkv, _attention_bias())


def _router_kernel(x_ref, wr_ref, rb_ref, idx_ref, wts_ref, rank_ref, cnt_ref, carry_ref, *, tm):
    @pl.when(pl.program_id(0) == 0)
    def _():
        carry_ref[...] = jnp.zeros_like(carry_ref)

    x = x_ref[...]
    x_hi = x.astype(BF16)
    x_lo = (x - x_hi.astype(F32)).astype(BF16)
    w_split = wr_ref[...]
    by_hi = jnp.dot(x_hi, w_split, preferred_element_type=F32)
    by_lo = jnp.dot(x_lo, w_split[:, :LANES], preferred_element_type=F32)
    logits = by_hi[:, :LANES] + (by_hi[:, LANES:] + by_lo)
    sc = jax.nn.sigmoid(logits.T[:N_EXPERTS])
    sel = sc + rb_ref[...]
    neg = -jnp.inf
    iota_g = lax.broadcasted_iota(jnp.int32, (GROUP_SIZE, tm), 0).astype(F32)

    def first_argmax(v, iota, size):
        m = jnp.max(v, axis=0, keepdims=True)
        return jnp.min(jnp.where(v == m, iota, float(size)), axis=0, keepdims=True)

    group_rows = []
    for g in range(N_GROUPS):
        v = sel[g * GROUP_SIZE:(g + 1) * GROUP_SIZE]
        m1 = jnp.max(v, axis=0, keepdims=True)
        i1 = first_argmax(v, iota_g, GROUP_SIZE)
        m2 = jnp.max(jnp.where(iota_g == i1, neg, v), axis=0, keepdims=True)
        group_rows.append(m1 + m2)
    cur = jnp.concatenate(group_rows, axis=0)
    iota_n = lax.broadcasted_iota(jnp.int32, (N_GROUPS, tm), 0).astype(F32)
    gsel = jnp.zeros((N_GROUPS, tm), F32)
    for _ in range(TOPK_GROUPS):
        hit = iota_n == first_argmax(cur, iota_n, N_GROUPS)
        gsel = jnp.where(hit, 1.0, gsel)
        cur = jnp.where(hit, neg, cur)
    cur = jnp.concatenate(
        [jnp.where(gsel[g:g + 1] > 0.5, sel[g * GROUP_SIZE:(g + 1) * GROUP_SIZE], neg)
         for g in range(N_GROUPS)], axis=0)

    iota_e = lax.broadcasted_iota(jnp.int32, (N_EXPERTS, tm), 0).astype(F32)
    member = jnp.zeros((N_EXPERTS, tm), F32)
    hits, idx_rows, w_rows = [], [], []
    for _ in range(TOP_K):
        ii = first_argmax(cur, iota_e, N_EXPERTS)
        hit = iota_e == ii
        hits.append(hit)
        idx_rows.append(ii)
        w_rows.append(jnp.sum(jnp.where(hit, sc, 0.0), axis=0, keepdims=True))
        member = jnp.where(hit, 1.0, member)
        cur = jnp.where(hit, neg, cur)
    w = jnp.concatenate(w_rows, axis=0)
    w = w / (jnp.sum(w, axis=0, keepdims=True) + 1e-20) * ROUTED_SCALE

    r = lax.broadcasted_iota(jnp.int32, (tm, tm), 0)
    c = lax.broadcasted_iota(jnp.int32, (tm, tm), 1)
    upper = jnp.where(r < c, 1.0, 0.0).astype(BF16)
    before = jnp.dot(member.astype(BF16), upper, preferred_element_type=F32)
    rank_full = carry_ref[...] + before
    rank_rows = [jnp.sum(jnp.where(h, rank_full, 0.0), axis=0, keepdims=True) for h in hits]

    idx_ref[...] = jnp.concatenate(idx_rows, axis=0).astype(jnp.int32)
    wts_ref[...] = w
    rank_ref[...] = jnp.concatenate(rank_rows, axis=0).astype(jnp.int32)
    carry_ref[...] = carry_ref[...] + jnp.sum(member, axis=1, keepdims=True)
    cnt_ref[...] = jnp.broadcast_to(carry_ref[...], cnt_ref.shape)


def _router(x, w_router, router_bias, *, tm):
    t, d = x.shape
    tok = lambda i: (0, i)
    w_padded = jnp.pad(w_router, ((0, 0), (0, LANES - N_EXPERTS)))
    w_hi = w_padded.astype(BF16)
    w_lo = (w_padded - w_hi.astype(F32)).astype(BF16)
    w_split = jnp.concatenate([w_hi, w_lo], axis=1)
    return pl.pallas_call(
        functools.partial(_router_kernel, tm=tm),
        grid=(t // tm,),
        in_specs=[
            pl.BlockSpec((tm, d), lambda i: (i, 0)),
            pl.BlockSpec((d, 2 * LANES), lambda i: (0, 0)),
            pl.BlockSpec((N_EXPERTS, 1), lambda i: (0, 0)),
        ],
        out_specs=[
            pl.BlockSpec((TOP_K, tm), tok),
            pl.BlockSpec((TOP_K, tm), tok),
            pl.BlockSpec((TOP_K, tm), tok),
            pl.BlockSpec((N_EXPERTS, 128), lambda i: (0, 0)),
        ],
        out_shape=[
            jax.ShapeDtypeStruct((TOP_K, t), jnp.int32),
            jax.ShapeDtypeStruct((TOP_K, t), F32),
            jax.ShapeDtypeStruct((TOP_K, t), jnp.int32),
            jax.ShapeDtypeStruct((N_EXPERTS, 128), F32),
        ],
        scratch_shapes=[pltpu.VMEM((N_EXPERTS, 1), F32)],
        name="router",
        compiler_params=_params("arbitrary"),
    )(x, w_split, router_bias.reshape(N_EXPERTS, 1))


def _experts_kernel(be_ref, nu_ref, plan_hbm, x_hbm, wg_ref, wu_ref, wd_ref, y_hbm,
                    plan_smem, xbuf0, xbuf1, ybuf0, ybuf1, wg_bf, wu_bf, wd_bf,
                    psem, gsem, ssem, *, tok_mask):
    b = pl.program_id(0)
    n_used = nu_ref[0]
    rows = MOE_ROWS
    xbuf = (xbuf0, xbuf1)
    ybuf = (ybuf0, ybuf1)

    def plan_copy(row, slot):
        return pltpu.make_async_copy(plan_hbm.at[row], plan_smem.at[slot], psem.at[slot])

    def start_gathers(pslot, slot):
        for r in range(rows):
            src = pl.multiple_of(plan_smem[pslot, r] & tok_mask, PACK_SUB)
            pltpu.make_async_copy(x_hbm.at[pl.ds(src, PACK_SUB)],
                                  xbuf[slot].at[pl.ds(r * PACK_SUB, PACK_SUB)],
                                  gsem.at[slot]).start()

    def start_scatters(pslot, slot):
        for r in range(rows):
            dst = pl.multiple_of(plan_smem[pslot, rows + r], PACK_SUB)
            pltpu.make_async_copy(ybuf[slot].at[pl.ds(r * PACK_SUB, PACK_SUB)],
                                  y_hbm.at[pl.ds(dst, PACK_SUB)],
                                  ssem.at[slot]).start(priority=1)

    def wait_rows(buf, sem):
        pltpu.make_async_copy(buf, buf, sem).wait()

    @pl.when(b == 0)
    def _():
        ybuf0[...] = jnp.zeros_like(ybuf0)
        ybuf1[...] = jnp.zeros_like(ybuf1)
        spare0 = y_hbm.shape[0] - 2 * rows * PACK_SUB
        pltpu.make_async_copy(ybuf0, y_hbm.at[pl.ds(spare0, rows * PACK_SUB)], ssem.at[0]).start()
        plan_copy(0, 0).start()
        plan_copy(0, 0).wait()
        start_gathers(0, 0)
        plan_copy(1, 1).start()

    def step(slot):
        other = 1 - slot
        changed = jnp.logical_or(b == 0, be_ref[b] != be_ref[jnp.maximum(b - 1, 0)])

        @pl.when(changed)
        def _():
            wg_bf[...] = wg_ref[...].astype(BF16)
            wu_bf[...] = wu_ref[...].astype(BF16)
            wd_bf[...] = wd_ref[...].astype(BF16)

        wait_rows(xbuf[slot], gsem.at[slot])
        wait_rows(ybuf[slot], ssem.at[slot])
        plan_copy(b + 1, other).wait()
        start_gathers(other, other)
        start_scatters(other, other)
        plan_copy(b + 2, slot).start()

        x = jnp.concatenate([p.astype(BF16) for p in _load_packed_rows(xbuf[slot], rows)], axis=1)
        gate = jnp.dot(x, wg_bf[...], preferred_element_type=F32)
        up = jnp.dot(x, wu_bf[...], preferred_element_type=F32)
        h = (_silu(gate) * up).astype(BF16)
        _store_packed_rows(ybuf[slot], jnp.dot(h, wd_bf[...], preferred_element_type=F32))

        @pl.when(b == n_used - 1)
        def _():
            plan_copy(b + 2, slot).wait()
            start_scatters(slot, slot)
            wait_rows(xbuf[other], gsem.at[other])
            wait_rows(ybuf[other], ssem.at[other])
            wait_rows(ybuf[slot], ssem.at[slot])

    for parity in range(2):
        pl.when(jnp.logical_and(b < n_used, b % 2 == parity))(functools.partial(step, parity))


def _experts(x_packed, layer, w_gate, w_up, w_down, block_expert, n_used, plan, *, n_out_rows):
    t = x_packed.shape[0] // PACK_SUB
    d = D_MODEL
    assert t & (t - 1) == 0, "token id is taken as the low bits of the flat (choice, token) index"
    n_blocks = plan.shape[0] - 2
    wspec_in = pl.BlockSpec((None, None, d, EXPERT_FF), lambda b, be, nu: (layer, be[b], 0, 0))
    wspec_out = pl.BlockSpec((None, None, EXPERT_FF, d), lambda b, be, nu: (layer, be[b], 0, 0))
    row_buf = pltpu.VMEM((MOE_ROWS * PACK_SUB, LANES), jnp.int32)
    grid_spec = pltpu.PrefetchScalarGridSpec(
        num_scalar_prefetch=2,
        grid=(n_blocks,),
        in_specs=[
            pl.BlockSpec(memory_space=pl.ANY),
            pl.BlockSpec(memory_space=pl.ANY),
            wspec_in, wspec_in, wspec_out,
        ],
        out_specs=pl.BlockSpec(memory_space=pl.ANY),
        scratch_shapes=[
            pltpu.SMEM((2, 2 * MOE_ROWS), jnp.int32),
            row_buf, row_buf, row_buf, row_buf,
            pltpu.VMEM((d, EXPERT_FF), BF16),
            pltpu.VMEM((d, EXPERT_FF), BF16),
            pltpu.VMEM((EXPERT_FF, d), BF16),
            pltpu.SemaphoreType.DMA((2,)),
            pltpu.SemaphoreType.DMA((2,)),
            pltpu.SemaphoreType.DMA((2,)),
        ],
    )
    return pl.pallas_call(
        functools.partial(_experts_kernel, tok_mask=(t - 1) * PACK_SUB),
        grid_spec=grid_spec,
        out_shape=jax.ShapeDtypeStruct((n_out_rows * PACK_SUB, LANES), jnp.int32),
        name="routed_experts",
        compiler_params=_params("arbitrary"),
    )(block_expert, n_used, plan, x_packed, w_gate, w_up, w_down)


def _shared_kernel(x_ref, wg_ref, wu_ref, wd_ref, o_ref):
    x = x_ref[...]
    xb = x.astype(BF16)
    gate = jnp.dot(xb, wg_ref[...], preferred_element_type=F32)
    up = jnp.dot(xb, wu_ref[...], preferred_element_type=F32)
    h = (_silu(gate) * up).astype(BF16)
    o_ref[...] = DN_ALPHA * x + jnp.dot(h, wd_ref[...], preferred_element_type=F32)


def _shared(x, wg, wu, wd, *, tm):
    t, d = x.shape
    f = wg.shape[1]
    return pl.pallas_call(
        _shared_kernel,
        grid=(t // tm,),
        in_specs=[
            pl.BlockSpec((tm, d), lambda i: (i, 0)),
            pl.BlockSpec((d, f), lambda i: (0, 0)),
            pl.BlockSpec((d, f), lambda i: (0, 0)),
            pl.BlockSpec((f, d), lambda i: (0, 0)),
        ],
        out_specs=pl.BlockSpec((tm, d), lambda i: (i, 0)),
        out_shape=jax.ShapeDtypeStruct((t, d), F32),
        name="shared_expert",
        compiler_params=_params("parallel"),
    )(x, wg, wu, wd)


def _combine_kernel(s_ref, w_ref, *refs, tm):
    y_refs = refs[:TOP_K]
    g_ref, beta_ref, o_ref = refs[TOP_K:]
    pieces = [s_ref[:, c * LANES:(c + 1) * LANES] for c in range(2 * PACK_SUB)]
    for k in range(TOP_K):
        wk = w_ref[:, k:k + 1]
        yk = _load_packed_rows(y_refs[k], tm)
        pieces = [p + wk * y for p, y in zip(pieces, yk)]
    o_ref[...] = _layer_norm(jnp.concatenate(pieces, axis=1), g_ref[...], beta_ref[...])


def _combine(s, wts_t, y, g, beta, *, tm):
    t, d = s.shape
    blocks = t // tm
    y_specs = [pl.BlockSpec((tm * PACK_SUB, LANES),
                            functools.partial(lambda i, k: (k * blocks + i, 0), k=k))
               for k in range(TOP_K)]
    return pl.pallas_call(
        functools.partial(_combine_kernel, tm=tm),
        grid=(blocks,),
        in_specs=[
            pl.BlockSpec((tm, d), lambda i: (i, 0)),
            pl.BlockSpec((tm, TOP_K), lambda i: (i, 0)),
            *y_specs,
            pl.BlockSpec((1, d), lambda i: (0, 0)),
            pl.BlockSpec((1, d), lambda i: (0, 0)),
        ],
        out_specs=pl.BlockSpec((tm, d), lambda i: (i, 0)),
        out_shape=jax.ShapeDtypeStruct((t, d), F32),
        name="moe_combine",
        compiler_params=_params("parallel"),
    )(s, wts_t, *([y] * TOP_K), g.reshape(1, d), beta.reshape(1, d))


def _dispatch_plan(idx, rank, counts, n_tok):
    n_assign = n_tok * TOP_K
    n_blocks = n_assign // MOE_ROWS + N_EXPERTS
    n_slots = n_blocks * MOE_ROWS
    blocks_per_expert = (counts + MOE_ROWS - 1) // MOE_ROWS
    block_end = jnp.cumsum(blocks_per_expert)
    row_start = (block_end - blocks_per_expert) * MOE_ROWS
    experts = jnp.arange(N_EXPERTS, dtype=jnp.int32)
    row_start_of = jnp.sum(jnp.where(idx[..., None] == experts, row_start, 0), axis=-1)
    dest = (row_start_of + rank).reshape(-1)
    slot = jnp.arange(n_slots, dtype=jnp.int32)
    spare = n_assign + ((slot // MOE_ROWS) % 2) * MOE_ROWS + slot % MOE_ROWS
    flat = jnp.arange(n_assign, dtype=jnp.int32)
    slot_flat = spare.at[dest].set(flat, unique_indices=True).reshape(n_blocks, MOE_ROWS)
    rows = slot_flat * PACK_SUB
    pad = jnp.broadcast_to((n_assign + MOE_ROWS + slot[:MOE_ROWS]) * PACK_SUB, (2, MOE_ROWS))
    plan = jnp.concatenate([jnp.concatenate([rows, pad], axis=0),
                            jnp.concatenate([pad, rows], axis=0)], axis=1)
    blocks = jnp.arange(n_blocks, dtype=jnp.int32)
    block_expert = jnp.minimum(
        jnp.sum((block_end[None, :] <= blocks[:, None]).astype(jnp.int32), axis=1), N_EXPERTS - 1)
    n_used = block_end[-1:].astype(jnp.int32)
    return block_expert, n_used, plan, n_assign + 2 * MOE_ROWS


def _moe(x, x_packed, layer, w_router, router_bias, w_gate, w_up, w_down, ws_gate, ws_up, ws_down,
         g, beta):
    n_tok = x.shape[0]
    idx, wts, rank, cnt = _router(x, w_router, router_bias, tm=512)
    counts = cnt[:, 0].astype(jnp.int32)
    block_expert, n_used, plan, n_out_rows = _dispatch_plan(idx, rank, counts, n_tok)
    y = _experts(x_packed, layer, w_gate, w_up, w_down, block_expert, n_used, plan,
                 n_out_rows=n_out_rows)
    s = _shared(x, ws_gate.astype(BF16), ws_up.astype(BF16), ws_down.astype(BF16), tm=512)
    return _combine(s, wts.T, y, g, beta, tm=256)


def kernel(x, a_w_in, a_b_in, a_ln_g, a_ln_b, a_w_s, a_b_s, a_w_out, b_w_qkv, b_b_qkv, b_sinks,
           b_w_o, b_b_o, moe_w_router, moe_router_bias, moe_w_gate, moe_w_up, moe_w_down,
           moe_ws_gate, moe_ws_up, moe_ws_down, norm_g, norm_b):
    bsz, seq_len, d = x.shape
    h = x.reshape(bsz * seq_len, d)
    for i in range(DEPTH):
        j = i // 2
        if i % 2 == 0:
            z = _linear(h, a_w_in[j].astype(BF16), a_b_in[j], act="gelu", tm=1024, tn=1024,
                        out_dtype=BF16)
            causal = jnp.tril(jnp.ones((A_CHUNK, A_CHUNK), dtype=bool))
            w_mix = jnp.where(causal[None], a_w_s[j], 0.0).astype(BF16)
            gated = _gmlp_gate(z, a_ln_g[j], a_ln_b[j], w_mix, a_b_s[j].T, tm=256)
            h, h_packed = _linear_res_ln(gated, a_w_out[j].astype(BF16), jnp.zeros((d,), F32), h,
                                         norm_g[i, 0], norm_b[i, 0], tm=512)
        else:
            q_scale = jnp.where(jnp.arange(QKV_DIM) < Q_DIM, HEAD_DIM ** -0.5, 1.0).astype(F32)
            qkv = _linear(h, (b_w_qkv[j] * q_scale).astype(BF16), b_b_qkv[j] * q_scale, act=None,
                          tm=1024, tn=1280, out_dtype=BF16)
            o = _attention(qkv, b_sinks[j], seq_len=seq_len)
            h, h_packed = _linear_res_ln(o, b_w_o[j].astype(BF16), b_b_o[j], h,
                                         norm_g[i, 0], norm_b[i, 0], tm=512)
        h = _moe(h, h_packed, i, moe_w_router[i], moe_router_bias[i], moe_w_gate, moe_w_up, moe_w_down,
                 moe_ws_gate[i], moe_ws_up[i], moe_ws_down[i], norm_g[i, 1], norm_b[i, 1])
    return h.reshape(bsz, seq_len, d)
```

```python
import functools

import jax
import jax.numpy as jnp
from jax import lax
from jax.experimental import pallas as pl
from jax.experimental.pallas import tpu as pltpu

F32 = jnp.float32
BF16 = jnp.bfloat16

D_MODEL = 2048
DEPTH = 2
A_CHUNK = 128
A_HALF = D_MODEL
A_GROUPS = A_HALF // 128
HEAD_DIM = 64
N_Q_HEADS = D_MODEL // HEAD_DIM
N_KV_HEADS = N_Q_HEADS // 8
Q_PER_KV = N_Q_HEADS // N_KV_HEADS
Q_DIM = N_Q_HEADS * HEAD_DIM
KV_DIM = N_KV_HEADS * HEAD_DIM
QKV_DIM = Q_DIM + 2 * KV_DIM
WINDOW = 128
ATTN_BLOCK = 128
N_EXPERTS = 64
TOP_K = 8
N_GROUPS = 8
GROUP_SIZE = N_EXPERTS // N_GROUPS
TOPK_GROUPS = 4
EXPERT_FF = D_MODEL // 4
ROUTED_SCALE = 2.5
DN_ALPHA = (2 * DEPTH) ** 0.25
LN_EPS = 1e-5

MXU_ROWS = 256
MOE_ROWS = MXU_ROWS
VMEM_LIMIT = 56 * 1024 * 1024


def _params(*sem):
    return pltpu.CompilerParams(dimension_semantics=sem, vmem_limit_bytes=VMEM_LIMIT)


def _layer_norm(y, g, b):
    mu = jnp.mean(y, axis=-1, keepdims=True)
    d = y - mu
    var = jnp.mean(d * d, axis=-1, keepdims=True)
    return d * lax.rsqrt(var + LN_EPS) * g + b


def _gelu(x):
    return 0.5 * x * (1.0 + lax.erf(x * (2.0 ** -0.5)))


def _silu(x):
    return x * jax.nn.sigmoid(x)


LANES = 128
PACK_SUB = D_MODEL // (2 * LANES)
HIGH_HALF = -65536


def _store_packed_rows(p_ref, rows_f32, first_row=0):
    n = rows_f32.shape[0]
    base = first_row * PACK_SUB
    bits = lax.bitcast_convert_type(rows_f32.astype(BF16).astype(F32), jnp.int32)
    for s in range(PACK_SUB):
        hi = bits[:, s * LANES:(s + 1) * LANES] & HIGH_HALF
        lo = lax.shift_right_logical(
            bits[:, D_MODEL // 2 + s * LANES:D_MODEL // 2 + (s + 1) * LANES], 16)
        p_ref[pl.ds(base + s, n, stride=PACK_SUB), :] = hi | lo


def _load_packed_rows(p_ref, n):
    his, los = [], []
    for s in range(PACK_SUB):
        w = p_ref[pl.ds(s, n, stride=PACK_SUB), :]
        his.append(lax.bitcast_convert_type(w & HIGH_HALF, F32))
        los.append(lax.bitcast_convert_type(lax.shift_left(w, 16), F32))
    return his + los


def _linear_kernel(x_ref, w_ref, b_ref, o_ref, *, act):
    acc = jnp.dot(x_ref[...].astype(BF16), w_ref[...], preferred_element_type=F32)
    acc = acc + b_ref[...]
    if act == "gelu":
        acc = _gelu(acc)
    o_ref[...] = acc.astype(o_ref.dtype)


def _linear(x, w, b, *, act, tm, tn, out_dtype):
    m, k = x.shape
    n = w.shape[1]
    return pl.pallas_call(
        functools.partial(_linear_kernel, act=act),
        grid=(m // tm, n // tn),
        in_specs=[
            pl.BlockSpec((tm, k), lambda i, j: (i, 0)),
            pl.BlockSpec((k, tn), lambda i, j: (0, j)),
            pl.BlockSpec((1, tn), lambda i, j: (0, j)),
        ],
        out_specs=pl.BlockSpec((tm, tn), lambda i, j: (i, j)),
        out_shape=jax.ShapeDtypeStruct((m, n), out_dtype),
        name="linear_" + str(act),
        compiler_params=_params("parallel", "arbitrary"),
    )(x, w, b.reshape(1, n))


def _linear_res_ln_kernel(x_ref, w_ref, b_ref, res_ref, g_ref, beta_ref, o_ref, p_ref):
    sub = min(MXU_ROWS, x_ref.shape[0])
    for r0 in range(0, x_ref.shape[0], sub):
        rows = slice(r0, r0 + sub)
        acc = jnp.dot(x_ref[rows, :], w_ref[...], preferred_element_type=F32)
        y = DN_ALPHA * res_ref[rows, :] + (acc + b_ref[...])
        out = _layer_norm(y, g_ref[...], beta_ref[...])
        o_ref[rows, :] = out
        _store_packed_rows(p_ref, out, first_row=r0)


def _linear_res_ln(x, w, b, res, g, beta, *, tm):
    m, k = x.shape
    n = w.shape[1]
    row = lambda i: (i, 0)
    fixed = lambda i: (0, 0)
    return pl.pallas_call(
        _linear_res_ln_kernel,
        grid=(m // tm,),
        in_specs=[
            pl.BlockSpec((tm, k), row),
            pl.BlockSpec((k, n), fixed),
            pl.BlockSpec((1, n), fixed),
            pl.BlockSpec((tm, n), row),
            pl.BlockSpec((1, n), fixed),
            pl.BlockSpec((1, n), fixed),
        ],
        out_specs=[pl.BlockSpec((tm, n), row), pl.BlockSpec((tm * PACK_SUB, LANES), row)],
        out_shape=[jax.ShapeDtypeStruct((m, n), F32),
                   jax.ShapeDtypeStruct((m * PACK_SUB, LANES), jnp.int32)],
        name="linear_res_ln",
        compiler_params=_params("parallel"),
    )(x, w, b.reshape(1, n), res, g.reshape(1, n), beta.reshape(1, n))


def _gmlp_gate_kernel(z_ref, lng_ref, lnb_ref, wmix_ref, bst_ref, o_ref, *, tm):
    v = z_ref[:, A_HALF:].astype(F32)
    vn = _layer_norm(v, lng_ref[...], lnb_ref[...]).astype(BF16)
    for c in range(tm // A_CHUNK):
        rows = slice(c * A_CHUNK, (c + 1) * A_CHUNK)
        for g in range(A_GROUPS):
            cols = slice(g * 128, (g + 1) * 128)
            mixed = jnp.dot(wmix_ref[g], vn[rows, cols], preferred_element_type=F32)
            mixed = mixed + bst_ref[:, g:g + 1]
            u = z_ref[rows, cols].astype(F32)
            o_ref[rows, cols] = (u * mixed).astype(o_ref.dtype)


def _gmlp_gate(z, ln_g, ln_b, w_mix, b_s_t, *, tm):
    m = z.shape[0]
    return pl.pallas_call(
        functools.partial(_gmlp_gate_kernel, tm=tm),
        grid=(m // tm,),
        in_specs=[
            pl.BlockSpec((tm, 2 * A_HALF), lambda i: (i, 0)),
            pl.BlockSpec((1, A_HALF), lambda i: (0, 0)),
            pl.BlockSpec((1, A_HALF), lambda i: (0, 0)),
            pl.BlockSpec((A_GROUPS, A_CHUNK, A_CHUNK), lambda i: (0, 0, 0)),
            pl.BlockSpec((A_CHUNK, A_GROUPS), lambda i: (0, 0)),
        ],
        out_specs=pl.BlockSpec((tm, A_HALF), lambda i: (i, 0)),
        out_shape=jax.ShapeDtypeStruct((m, A_HALF), BF16),
        name="gmlp_gate",
        compiler_params=_params("parallel"),
    )(z, ln_g.reshape(1, A_HALF), ln_b.reshape(1, A_HALF), w_mix, b_s_t)


def _attn_kernel(sinks_ref, q_ref, kp_ref, kc_ref, vp_ref, vc_ref, bias_ref, o_ref, *,
                 blocks_per_seq):
    n = pl.program_id(0) % blocks_per_seq
    first = (n == 0).astype(jnp.int32)
    qi = lax.broadcasted_iota(jnp.int32, (ATTN_BLOCK, ATTN_BLOCK), 0)
    ci = lax.broadcasted_iota(jnp.int32, (ATTN_BLOCK, ATTN_BLOCK), 1)
    upper = ci > qi
    low_half = lax.broadcasted_iota(jnp.int32, (ATTN_BLOCK, 2 * HEAD_DIM), 1) < HEAD_DIM
    ones = jnp.ones((2 * ATTN_BLOCK, HEAD_DIM), BF16)
    for g in range(N_KV_HEADS):
        kv_cols = slice(g * HEAD_DIM, (g + 1) * HEAD_DIM)
        kk = jnp.concatenate([kp_ref[:, kv_cols], kc_ref[:, kv_cols]], axis=0)
        vv = jnp.concatenate([vp_ref[:, kv_cols], vc_ref[:, kv_cols]], axis=0)
        vv_ones = jnp.concatenate([vv, ones, ones, vv], axis=1)
        heads = [g * Q_PER_KV + j for j in range(Q_PER_KV)]
        qg = jnp.concatenate([q_ref[:, h * HEAD_DIM:(h + 1) * HEAD_DIM] for h in heads], axis=0)
        s = lax.dot_general(qg, kk, (((1,), (1,)), ((), ())), preferred_element_type=F32)
        es, maxes = [], []
        for j, h in enumerate(heads):
            sj = s[j * ATTN_BLOCK:(j + 1) * ATTN_BLOCK]
            sm = jnp.where(upper, sj[:, :ATTN_BLOCK], sj[:, ATTN_BLOCK:]) + bias_ref[first, h]
            mx = jnp.maximum(jnp.max(sm, axis=-1, keepdims=True), sinks_ref[h])
            e = jnp.exp(sm - mx)
            es.append(jnp.concatenate([jnp.where(upper, e, 0.0), jnp.where(upper, 0.0, e)],
                                      axis=1).astype(BF16))
            maxes.append(mx)
        o = jnp.dot(jnp.concatenate(es, axis=0), vv_ones, preferred_element_type=F32)
        for j in range(0, Q_PER_KV, 2):
            even = o[j * ATTN_BLOCK:(j + 1) * ATTN_BLOCK]
            odd = o[(j + 1) * ATTN_BLOCK:(j + 2) * ATTN_BLOCK]
            num = jnp.where(low_half, even[:, :2 * HEAD_DIM], odd[:, 2 * HEAD_DIM:])
            total = jnp.where(low_half, even[:, 2 * HEAD_DIM:], odd[:, :2 * HEAD_DIM])
            sink_e = jnp.where(low_half, jnp.exp(sinks_ref[heads[j]] - maxes[j]),
                               jnp.exp(sinks_ref[heads[j + 1]] - maxes[j + 1]))
            cols = slice(heads[j] * HEAD_DIM, (heads[j] + 2) * HEAD_DIM)
            o_ref[:, cols] = (num * (1.0 / (total + sink_e))).astype(o_ref.dtype)


def _attention_bias():
    qi = jnp.arange(ATTN_BLOCK, dtype=jnp.int32)[:, None]
    kc = jnp.arange(ATTN_BLOCK, dtype=jnp.int32)[None, :]
    upper = kc > qi
    dist = (qi - kc + jnp.where(upper, ATTN_BLOCK, 0)).astype(F32)
    head = jnp.arange(1, N_Q_HEADS + 1, dtype=F32)
    slopes = jnp.exp2(-8.0 * head / N_Q_HEADS)
    alibi = -slopes[:, None, None] * dist[None]
    return jnp.stack([alibi, jnp.where(upper[None], -jnp.inf, alibi)])


def _attention(qkv, sinks, *, seq_len):
    assert WINDOW == ATTN_BLOCK
    t = qkv.shape[0]
    blocks_per_seq = seq_len // ATTN_BLOCK
    kcol = Q_DIM // KV_DIM
    vcol = kcol + 1
    prev = lambda i: jnp.maximum(i - 1, 0)
    grid_spec = pltpu.PrefetchScalarGridSpec(
        num_scalar_prefetch=1,
        grid=(t // ATTN_BLOCK,),
        in_specs=[
            pl.BlockSpec((ATTN_BLOCK, Q_DIM), lambda i, s: (i, 0)),
            pl.BlockSpec((ATTN_BLOCK, KV_DIM), lambda i, s: (prev(i), kcol)),
            pl.BlockSpec((ATTN_BLOCK, KV_DIM), lambda i, s: (i, kcol)),
            pl.BlockSpec((ATTN_BLOCK, KV_DIM), lambda i, s: (prev(i), vcol)),
            pl.BlockSpec((ATTN_BLOCK, KV_DIM), lambda i, s: (i, vcol)),
            pl.BlockSpec((2, N_Q_HEADS, ATTN_BLOCK, ATTN_BLOCK), lambda i, s: (0, 0, 0, 0)),
        ],
        out_specs=pl.BlockSpec((ATTN_BLOCK, Q_DIM), lambda i, s: (i, 0)),
    )
    return pl.pallas_call(
        functools.partial(_attn_kernel, blocks_per_seq=blocks_per_seq),
        grid_spec=grid_spec,
        out_shape=jax.ShapeDtypeStruct((t, Q_DIM), BF16),
        name="swa_attention",
        compiler_params=_params("parallel"),
    )(sinks, qkv, qkv, qkv, qkv, qkv, _attention_bias())


def _router_kernel(x_ref, wr_ref, rb_ref, idx_ref, wts_ref, rank_ref, cnt_ref, carry_ref, *, tm):
    @pl.when(pl.program_id(0) == 0)
    def _():
        carry_ref[...] = jnp.zeros_like(carry_ref)

    x = x_ref[...]
    x_hi = x.astype(BF16)
    x_lo = (x - x_hi.astype(F32)).astype(BF16)
    w_split = wr_ref[...]
    by_hi = jnp.dot(x_hi, w_split, preferred_element_type=F32)
    by_lo = jnp.dot(x_lo, w_split[:, :LANES], preferred_element_type=F32)
    logits = by_hi[:, :LANES] + (by_hi[:, LANES:] + by_lo)
    sc = jax.nn.sigmoid(logits.T[:N_EXPERTS])
    sel = sc + rb_ref[...]
    neg = -jnp.inf
    iota_g = lax.broadcasted_iota(jnp.int32, (GROUP_SIZE, tm), 0).astype(F32)

    def first_argmax(v, iota, size):
        m = jnp.max(v, axis=0, keepdims=True)
        return jnp.min(jnp.where(v == m, iota, float(size)), axis=0, keepdims=True)

    group_rows = []
    for g in range(N_GROUPS):
        v = sel[g * GROUP_SIZE:(g + 1) * GROUP_SIZE]
        m1 = jnp.max(v, axis=0, keepdims=True)
        i1 = first_argmax(v, iota_g, GROUP_SIZE)
        m2 = jnp.max(jnp.where(iota_g == i1, neg, v), axis=0, keepdims=True)
        group_rows.append(m1 + m2)
    cur = jnp.concatenate(group_rows, axis=0)
    iota_n = lax.broadcasted_iota(jnp.int32, (N_GROUPS, tm), 0).astype(F32)
    gsel = jnp.zeros((N_GROUPS, tm), F32)
    for _ in range(TOPK_GROUPS):
        hit = iota_n == first_argmax(cur, iota_n, N_GROUPS)
        gsel = jnp.where(hit, 1.0, gsel)
        cur = jnp.where(hit, neg, cur)
    cur = jnp.concatenate(
        [jnp.where(gsel[g:g + 1] > 0.5, sel[g * GROUP_SIZE:(g + 1) * GROUP_SIZE], neg)
         for g in range(N_GROUPS)], axis=0)

    iota_e = lax.broadcasted_iota(jnp.int32, (N_EXPERTS, tm), 0).astype(F32)
    member = jnp.zeros((N_EXPERTS, tm), F32)
    hits, idx_rows, w_rows = [], [], []
    for _ in range(TOP_K):
        ii = first_argmax(cur, iota_e, N_EXPERTS)
        hit = iota_e == ii
        hits.append(hit)
        idx_rows.append(ii)
        w_rows.append(jnp.sum(jnp.where(hit, sc, 0.0), axis=0, keepdims=True))
        member = jnp.where(hit, 1.0, member)
        cur = jnp.where(hit, neg, cur)
    w = jnp.concatenate(w_rows, axis=0)
    w = w / (jnp.sum(w, axis=0, keepdims=True) + 1e-20) * ROUTED_SCALE

    r = lax.broadcasted_iota(jnp.int32, (tm, tm), 0)
    c = lax.broadcasted_iota(jnp.int32, (tm, tm), 1)
    upper = jnp.where(r < c, 1.0, 0.0).astype(BF16)
    before = jnp.dot(member.astype(BF16), upper, preferred_element_type=F32)
    rank_full = carry_ref[...] + before
    rank_rows = [jnp.sum(jnp.where(h, rank_full, 0.0), axis=0, keepdims=True) for h in hits]

    idx_ref[...] = jnp.concatenate(idx_rows, axis=0).astype(jnp.int32)
    wts_ref[...] = w
    rank_ref[...] = jnp.concatenate(rank_rows, axis=0).astype(jnp.int32)
    carry_ref[...] = carry_ref[...] + jnp.sum(member, axis=1, keepdims=True)
    cnt_ref[...] = jnp.broadcast_to(carry_ref[...], cnt_ref.shape)


def _router(x, w_router, router_bias, *, tm):
    t, d = x.shape
    tok = lambda i: (0, i)
    w_padded = jnp.pad(w_router, ((0, 0), (0, LANES - N_EXPERTS)))
    w_hi = w_padded.astype(BF16)
    w_lo = (w_padded - w_hi.astype(F32)).astype(BF16)
    w_split = jnp.concatenate([w_hi, w_lo], axis=1)
    return pl.pallas_call(
        functools.partial(_router_kernel, tm=tm),
        grid=(t // tm,),
        in_specs=[
            pl.BlockSpec((tm, d), lambda i: (i, 0)),
            pl.BlockSpec((d, 2 * LANES), lambda i: (0, 0)),
            pl.BlockSpec((N_EXPERTS, 1), lambda i: (0, 0)),
        ],
        out_specs=[
            pl.BlockSpec((TOP_K, tm), tok),
            pl.BlockSpec((TOP_K, tm), tok),
            pl.BlockSpec((TOP_K, tm), tok),
            pl.BlockSpec((N_EXPERTS, 128), lambda i: (0, 0)),
        ],
        out_shape=[
            jax.ShapeDtypeStruct((TOP_K, t), jnp.int32),
            jax.ShapeDtypeStruct((TOP_K, t), F32),
            jax.ShapeDtypeStruct((TOP_K, t), jnp.int32),
            jax.ShapeDtypeStruct((N_EXPERTS, 128), F32),
        ],
        scratch_shapes=[pltpu.VMEM((N_EXPERTS, 1), F32)],
        name="router",
        compiler_params=_params("arbitrary"),
    )(x, w_split, router_bias.reshape(N_EXPERTS, 1))


def _experts_kernel(be_ref, nu_ref, plan_hbm, x_hbm, wg_ref, wu_ref, wd_ref, y_hbm,
                    plan_smem, xbuf0, xbuf1, ybuf0, ybuf1, wg_bf, wu_bf, wd_bf,
                    psem, gsem, ssem, *, tok_mask):
    b = pl.program_id(0)
    n_used = nu_ref[0]
    rows = MOE_ROWS
    xbuf = (xbuf0, xbuf1)
    ybuf = (ybuf0, ybuf1)

    def plan_copy(row, slot):
        return pltpu.make_async_copy(plan_hbm.at[row], plan_smem.at[slot], psem.at[slot])

    def start_gathers(pslot, slot):
        for r in range(rows):
            src = pl.multiple_of(plan_smem[pslot, r] & tok_mask, PACK_SUB)
            pltpu.make_async_copy(x_hbm.at[pl.ds(src, PACK_SUB)],
                                  xbuf[slot].at[pl.ds(r * PACK_SUB, PACK_SUB)],
                                  gsem.at[slot]).start()

    def start_scatters(pslot, slot):
        for r in range(rows):
            dst = pl.multiple_of(plan_smem[pslot, rows + r], PACK_SUB)
            pltpu.make_async_copy(ybuf[slot].at[pl.ds(r * PACK_SUB, PACK_SUB)],
                                  y_hbm.at[pl.ds(dst, PACK_SUB)],
                                  ssem.at[slot]).start(priority=1)

    def wait_rows(buf, sem):
        pltpu.make_async_copy(buf, buf, sem).wait()

    @pl.when(b == 0)
    def _():
        ybuf0[...] = jnp.zeros_like(ybuf0)
        ybuf1[...] = jnp.zeros_like(ybuf1)
        spare0 = y_hbm.shape[0] - 2 * rows * PACK_SUB
        pltpu.make_async_copy(ybuf0, y_hbm.at[pl.ds(spare0, rows * PACK_SUB)], ssem.at[0]).start()
        plan_copy(0, 0).start()
        plan_copy(0, 0).wait()
        start_gathers(0, 0)
        plan_copy(1, 1).start()

    def step(slot):
        other = 1 - slot
        changed = jnp.logical_or(b == 0, be_ref[b] != be_ref[jnp.maximum(b - 1, 0)])

        @pl.when(changed)
        def _():
            wg_bf[...] = wg_ref[...].astype(BF16)
            wu_bf[...] = wu_ref[...].astype(BF16)
            wd_bf[...] = wd_ref[...].astype(BF16)

        wait_rows(xbuf[slot], gsem.at[slot])
        wait_rows(ybuf[slot], ssem.at[slot])
        plan_copy(b + 1, other).wait()
        start_gathers(other, other)
        start_scatters(other, other)
        plan_copy(b + 2, slot).start()

        x = jnp.concatenate([p.astype(BF16) for p in _load_packed_rows(xbuf[slot], rows)], axis=1)
        gate = jnp.dot(x, wg_bf[...], preferred_element_type=F32)
        up = jnp.dot(x, wu_bf[...], preferred_element_type=F32)
        h = (_silu(gate) * up).astype(BF16)
        _store_packed_rows(ybuf[slot], jnp.dot(h, wd_bf[...], preferred_element_type=F32))

        @pl.when(b == n_used - 1)
        def _():
            plan_copy(b + 2, slot).wait()
            start_scatters(slot, slot)
            wait_rows(xbuf[other], gsem.at[other])
            wait_rows(ybuf[other], ssem.at[other])
            wait_rows(ybuf[slot], ssem.at[slot])

    for parity in range(2):
        pl.when(jnp.logical_and(b < n_used, b % 2 == parity))(functools.partial(step, parity))


def _experts(x_packed, layer, w_gate, w_up, w_down, block_expert, n_used, plan, *, n_out_rows):
    t = x_packed.shape[0] // PACK_SUB
    d = D_MODEL
    assert t & (t - 1) == 0, "token id is taken as the low bits of the flat (choice, token) index"
    n_blocks = plan.shape[0] - 2
    wspec_in = pl.BlockSpec((None, None, d, EXPERT_FF), lambda b, be, nu: (layer, be[b], 0, 0))
    wspec_out = pl.BlockSpec((None, None, EXPERT_FF, d), lambda b, be, nu: (layer, be[b], 0, 0))
    row_buf = pltpu.VMEM((MOE_ROWS * PACK_SUB, LANES), jnp.int32)
    grid_spec = pltpu.PrefetchScalarGridSpec(
        num_scalar_prefetch=2,
        grid=(n_blocks,),
        in_specs=[
            pl.BlockSpec(memory_space=pl.ANY),
            pl.BlockSpec(memory_space=pl.ANY),
            wspec_in, wspec_in, wspec_out,
        ],
        out_specs=pl.BlockSpec(memory_space=pl.ANY),
        scratch_shapes=[
            pltpu.SMEM((2, 2 * MOE_ROWS), jnp.int32),
            row_buf, row_buf, row_buf, row_buf,
            pltpu.VMEM((d, EXPERT_FF), BF16),
            pltpu.VMEM((d, EXPERT_FF), BF16),
            pltpu.VMEM((EXPERT_FF, d), BF16),
            pltpu.SemaphoreType.DMA((2,)),
            pltpu.SemaphoreType.DMA((2,)),
            pltpu.SemaphoreType.DMA((2,)),
        ],
    )
    return pl.pallas_call(
        functools.partial(_experts_kernel, tok_mask=(t - 1) * PACK_SUB),
        grid_spec=grid_spec,
        out_shape=jax.ShapeDtypeStruct((n_out_rows * PACK_SUB, LANES), jnp.int32),
        name="routed_experts",
        compiler_params=_params("arbitrary"),
    )(block_expert, n_used, plan, x_packed, w_gate, w_up, w_down)


def _shared_kernel(x_ref, wg_ref, wu_ref, wd_ref, o_ref):
    x = x_ref[...]
    xb = x.astype(BF16)
    gate = jnp.dot(xb, wg_ref[...], preferred_element_type=F32)
    up = jnp.dot(xb, wu_ref[...], preferred_element_type=F32)
    h = (_silu(gate) * up).astype(BF16)
    o_ref[...] = DN_ALPHA * x + jnp.dot(h, wd_ref[...], preferred_element_type=F32)


def _shared(x, wg, wu, wd, *, tm):
    t, d = x.shape
    f = wg.shape[1]
    return pl.pallas_call(
        _shared_kernel,
        grid=(t // tm,),
        in_specs=[
            pl.BlockSpec((tm, d), lambda i: (i, 0)),
            pl.BlockSpec((d, f), lambda i: (0, 0)),
            pl.BlockSpec((d, f), lambda i: (0, 0)),
            pl.BlockSpec((f, d), lambda i: (0, 0)),
        ],
        out_specs=pl.BlockSpec((tm, d), lambda i: (i, 0)),
        out_shape=jax.ShapeDtypeStruct((t, d), F32),
        name="shared_expert",
        compiler_params=_params("parallel"),
    )(x, wg, wu, wd)


def _combine_kernel(s_ref, w_ref, *refs, tm):
    y_refs = refs[:TOP_K]
    g_ref, beta_ref, o_ref = refs[TOP_K:]
    pieces = [s_ref[:, c * LANES:(c + 1) * LANES] for c in range(2 * PACK_SUB)]
    for k in range(TOP_K):
        wk = w_ref[:, k:k + 1]
        yk = _load_packed_rows(y_refs[k], tm)
        pieces = [p + wk * y for p, y in zip(pieces, yk)]
    o_ref[...] = _layer_norm(jnp.concatenate(pieces, axis=1), g_ref[...], beta_ref[...])


def _combine(s, wts_t, y, g, beta, *, tm):
    t, d = s.shape
    blocks = t // tm
    y_specs = [pl.BlockSpec((tm * PACK_SUB, LANES),
                            functools.partial(lambda i, k: (k * blocks + i, 0), k=k))
               for k in range(TOP_K)]
    return pl.pallas_call(
        functools.partial(_combine_kernel, tm=tm),
        grid=(blocks,),
        in_specs=[
            pl.BlockSpec((tm, d), lambda i: (i, 0)),
            pl.BlockSpec((tm, TOP_K), lambda i: (i, 0)),
            *y_specs,
            pl.BlockSpec((1, d), lambda i: (0, 0)),
            pl.BlockSpec((1, d), lambda i: (0, 0)),
        ],
        out_specs=pl.BlockSpec((tm, d), lambda i: (i, 0)),
        out_shape=jax.ShapeDtypeStruct((t, d), F32),
        name="moe_combine",
        compiler_params=_params("parallel"),
    )(s, wts_t, *([y] * TOP_K), g.reshape(1, d), beta.reshape(1, d))


def _dispatch_plan(idx, rank, counts, n_tok):
    n_assign = n_tok * TOP_K
    n_blocks = n_assign // MOE_ROWS + N_EXPERTS
    n_slots = n_blocks * MOE_ROWS
    blocks_per_expert = (counts + MOE_ROWS - 1) // MOE_ROWS
    block_end = jnp.cumsum(blocks_per_expert)
    row_start = (block_end - blocks_per_expert) * MOE_ROWS
    experts = jnp.arange(N_EXPERTS, dtype=jnp.int32)
    row_start_of = jnp.sum(jnp.where(idx[..., None] == experts, row_start, 0), axis=-1)
    dest = (row_start_of + rank).reshape(-1)
    slot = jnp.arange(n_slots, dtype=jnp.int32)
    n_pad = n_slots - n_assign
    pad_counts = jnp.concatenate([blocks_per_expert * MOE_ROWS - counts,
                                  n_slots - block_end[-1:] * MOE_ROWS])
    pad_first = jnp.concatenate([row_start + counts, block_end[-1:] * MOE_ROWS])
    pad_end = jnp.cumsum(pad_counts)
    j = jnp.arange(n_pad, dtype=jnp.int32)
    pad_begin = pad_end - pad_counts
    in_owner = (j[:, None] >= pad_begin[None, :]) & (j[:, None] < pad_end[None, :])
    pad_slot = jnp.sum(jnp.where(in_owner, (pad_first - pad_begin)[None, :] + j[:, None], 0),
                       axis=1)
    spare = n_assign + ((pad_slot // MOE_ROWS) % 2) * MOE_ROWS + pad_slot % MOE_ROWS
    flat = jnp.arange(n_assign, dtype=jnp.int32)
    _, slot_flat = lax.sort_key_val(jnp.concatenate([dest, pad_slot]),
                                    jnp.concatenate([flat, spare]))
    slot_flat = slot_flat.reshape(n_blocks, MOE_ROWS)
    rows = slot_flat * PACK_SUB
    pad = jnp.broadcast_to((n_assign + MOE_ROWS + slot[:MOE_ROWS]) * PACK_SUB, (2, MOE_ROWS))
    plan = jnp.concatenate([jnp.concatenate([rows, pad], axis=0),
                            jnp.concatenate([pad, rows], axis=0)], axis=1)
    blocks = jnp.arange(n_blocks, dtype=jnp.int32)
    block_expert = jnp.minimum(
        jnp.sum((block_end[None, :] <= blocks[:, None]).astype(jnp.int32), axis=1), N_EXPERTS - 1)
    n_used = block_end[-1:].astype(jnp.int32)
    return block_expert, n_used, plan, n_assign + 2 * MOE_ROWS


def _moe(x, x_packed, layer, w_router, router_bias, w_gate, w_up, w_down, ws_gate, ws_up, ws_down,
         g, beta):
    n_tok = x.shape[0]
    idx, wts, rank, cnt = _router(x, w_router, router_bias, tm=512)
    counts = cnt[:, 0].astype(jnp.int32)
    block_expert, n_used, plan, n_out_rows = _dispatch_plan(idx, rank, counts, n_tok)
    y = _experts(x_packed, layer, w_gate, w_up, w_down, block_expert, n_used, plan,
                 n_out_rows=n_out_rows)
    s = _shared(x, ws_gate.astype(BF16), ws_up.astype(BF16), ws_down.astype(BF16), tm=512)
    return _combine(s, wts.T, y, g, beta, tm=256)


def kernel(x, a_w_in, a_b_in, a_ln_g, a_ln_b, a_w_s, a_b_s, a_w_out, b_w_qkv, b_b_qkv, b_sinks,
           b_w_o, b_b_o, moe_w_router, moe_router_bias, moe_w_gate, moe_w_up, moe_w_down,
           moe_ws_gate, moe_ws_up, moe_ws_down, norm_g, norm_b):
    bsz, seq_len, d = x.shape
    h = x.reshape(bsz * seq_len, d)
    for i in range(DEPTH):
        j = i // 2
        if i % 2 == 0:
            z = _linear(h, a_w_in[j].astype(BF16), a_b_in[j], act="gelu", tm=1024, tn=1024,
                        out_dtype=BF16)
            causal = jnp.tril(jnp.ones((A_CHUNK, A_CHUNK), dtype=bool))
            w_mix = jnp.where(causal[None], a_w_s[j], 0.0).astype(BF16)
            gated = _gmlp_gate(z, a_ln_g[j], a_ln_b[j], w_mix, a_b_s[j].T, tm=256)
            h, h_packed = _linear_res_ln(gated, a_w_out[j].astype(BF16), jnp.zeros((d,), F32), h,
                                         norm_g[i, 0], norm_b[i, 0], tm=512)
        else:
            q_scale = jnp.where(jnp.arange(QKV_DIM) < Q_DIM, HEAD_DIM ** -0.5, 1.0).astype(F32)
            qkv = _linear(h, (b_w_qkv[j] * q_scale).astype(BF16), b_b_qkv[j] * q_scale, act=None,
                          tm=1024, tn=1280, out_dtype=BF16)
            o = _attention(qkv, b_sinks[j], seq_len=seq_len)
            h, h_packed = _linear_res_ln(o, b_w_o[j].astype(BF16), b_b_o[j], h,
                                         norm_g[i, 0], norm_b[i, 0], tm=512)
        h = _moe(h, h_packed, i, moe_w_router[i], moe_router_bias[i], moe_w_gate, moe_w_up, moe_w_down,
                 moe_ws_gate[i], moe_ws_up[i], moe_ws_down[i], norm_g[i, 1], norm_b[i, 1])
    return h.reshape(bsz, seq_len, d)
```

```python
import functools

import jax
import jax.numpy as jnp
from jax import lax
from jax.experimental import pallas as pl
from jax.experimental.pallas import tpu as pltpu

F32 = jnp.float32
BF16 = jnp.bfloat16

D_MODEL = 2048
DEPTH = 2
A_CHUNK = 128
A_HALF = D_MODEL
A_GROUPS = A_HALF // 128
HEAD_DIM = 64
N_Q_HEADS = D_MODEL // HEAD_DIM
N_KV_HEADS = N_Q_HEADS // 8
Q_PER_KV = N_Q_HEADS // N_KV_HEADS
Q_DIM = N_Q_HEADS * HEAD_DIM
KV_DIM = N_KV_HEADS * HEAD_DIM
QKV_DIM = Q_DIM + 2 * KV_DIM
WINDOW = 128
ATTN_BLOCK = 128
N_EXPERTS = 64
TOP_K = 8
N_GROUPS = 8
GROUP_SIZE = N_EXPERTS // N_GROUPS
TOPK_GROUPS = 4
EXPERT_FF = D_MODEL // 4
ROUTED_SCALE = 2.5
DN_ALPHA = (2 * DEPTH) ** 0.25
LN_EPS = 1e-5

MXU_ROWS = 256
MOE_ROWS = MXU_ROWS
VMEM_LIMIT = 56 * 1024 * 1024


def _params(*sem):
    return pltpu.CompilerParams(dimension_semantics=sem, vmem_limit_bytes=VMEM_LIMIT)


def _layer_norm(y, g, b):
    mu = jnp.mean(y, axis=-1, keepdims=True)
    d = y - mu
    var = jnp.mean(d * d, axis=-1, keepdims=True)
    return d * lax.rsqrt(var + LN_EPS) * g + b


def _gelu(x):
    return 0.5 * x * (1.0 + lax.erf(x * (2.0 ** -0.5)))


def _silu(x):
    return x * jax.nn.sigmoid(x)


LANES = 128
PACK_SUB = D_MODEL // (2 * LANES)
HIGH_HALF = -65536


def _store_packed_rows(p_ref, rows_f32, first_row=0):
    n = rows_f32.shape[0]
    base = first_row * PACK_SUB
    bits = lax.bitcast_convert_type(rows_f32.astype(BF16).astype(F32), jnp.int32)
    for s in range(PACK_SUB):
        hi = bits[:, s * LANES:(s + 1) * LANES] & HIGH_HALF
        lo = lax.shift_right_logical(
            bits[:, D_MODEL // 2 + s * LANES:D_MODEL // 2 + (s + 1) * LANES], 16)
        p_ref[pl.ds(base + s, n, stride=PACK_SUB), :] = hi | lo


def _load_packed_rows(p_ref, n):
    his, los = [], []
    for s in range(PACK_SUB):
        w = p_ref[pl.ds(s, n, stride=PACK_SUB), :]
        his.append(lax.bitcast_convert_type(w & HIGH_HALF, F32))
        los.append(lax.bitcast_convert_type(lax.shift_left(w, 16), F32))
    return his + los


def _linear_kernel(x_ref, w_ref, b_ref, o_ref, *, act):
    acc = jnp.dot(x_ref[...].astype(BF16), w_ref[...], preferred_element_type=F32)
    acc = acc + b_ref[...]
    if act == "gelu":
        acc = _gelu(acc)
    o_ref[...] = acc.astype(o_ref.dtype)


def _linear(x, w, b, *, act, tm, tn, out_dtype):
    m, k = x.shape
    n = w.shape[1]
    return pl.pallas_call(
        functools.partial(_linear_kernel, act=act),
        grid=(m // tm, n // tn),
        in_specs=[
            pl.BlockSpec((tm, k), lambda i, j: (i, 0)),
            pl.BlockSpec((k, tn), lambda i, j: (0, j)),
            pl.BlockSpec((1, tn), lambda i, j: (0, j)),
        ],
        out_specs=pl.BlockSpec((tm, tn), lambda i, j: (i, j)),
        out_shape=jax.ShapeDtypeStruct((m, n), out_dtype),
        name="linear_" + str(act),
        compiler_params=_params("parallel", "arbitrary"),
    )(x, w, b.reshape(1, n))


def _linear_res_ln_kernel(x_ref, w_ref, b_ref, res_ref, g_ref, beta_ref, o_ref, p_ref):
    sub = min(MXU_ROWS, x_ref.shape[0])
    for r0 in range(0, x_ref.shape[0], sub):
        rows = slice(r0, r0 + sub)
        acc = jnp.dot(x_ref[rows, :], w_ref[...], preferred_element_type=F32)
        y = DN_ALPHA * res_ref[rows, :] + (acc + b_ref[...])
        out = _layer_norm(y, g_ref[...], beta_ref[...])
        o_ref[rows, :] = out
        _store_packed_rows(p_ref, out, first_row=r0)


def _linear_res_ln(x, w, b, res, g, beta, *, tm):
    m, k = x.shape
    n = w.shape[1]
    row = lambda i: (i, 0)
    fixed = lambda i: (0, 0)
    return pl.pallas_call(
        _linear_res_ln_kernel,
        grid=(m // tm,),
        in_specs=[
            pl.BlockSpec((tm, k), row),
            pl.BlockSpec((k, n), fixed),
            pl.BlockSpec((1, n), fixed),
            pl.BlockSpec((tm, n), row),
            pl.BlockSpec((1, n), fixed),
            pl.BlockSpec((1, n), fixed),
        ],
        out_specs=[pl.BlockSpec((tm, n), row), pl.BlockSpec((tm * PACK_SUB, LANES), row)],
        out_shape=[jax.ShapeDtypeStruct((m, n), F32),
                   jax.ShapeDtypeStruct((m * PACK_SUB, LANES), jnp.int32)],
        name="linear_res_ln",
        compiler_params=_params("parallel"),
    )(x, w, b.reshape(1, n), res, g.reshape(1, n), beta.reshape(1, n))


def _gmlp_gate_kernel(z_ref, lng_ref, lnb_ref, wmix_ref, bst_ref, o_ref, *, tm):
    v = z_ref[:, A_HALF:].astype(F32)
    vn = _layer_norm(v, lng_ref[...], lnb_ref[...]).astype(BF16)
    for c in range(tm // A_CHUNK):
        rows = slice(c * A_CHUNK, (c + 1) * A_CHUNK)
        for g in range(A_GROUPS):
            cols = slice(g * 128, (g + 1) * 128)
            mixed = jnp.dot(wmix_ref[g], vn[rows, cols], preferred_element_type=F32)
            mixed = mixed + bst_ref[:, g:g + 1]
            u = z_ref[rows, cols].astype(F32)
            o_ref[rows, cols] = (u * mixed).astype(o_ref.dtype)


def _gmlp_gate(z, ln_g, ln_b, w_mix, b_s_t, *, tm):
    m = z.shape[0]
    return pl.pallas_call(
        functools.partial(_gmlp_gate_kernel, tm=tm),
        grid=(m // tm,),
        in_specs=[
            pl.BlockSpec((tm, 2 * A_HALF), lambda i: (i, 0)),
            pl.BlockSpec((1, A_HALF), lambda i: (0, 0)),
            pl.BlockSpec((1, A_HALF), lambda i: (0, 0)),
            pl.BlockSpec((A_GROUPS, A_CHUNK, A_CHUNK), lambda i: (0, 0, 0)),
            pl.BlockSpec((A_CHUNK, A_GROUPS), lambda i: (0, 0)),
        ],
        out_specs=pl.BlockSpec((tm, A_HALF), lambda i: (i, 0)),
        out_shape=jax.ShapeDtypeStruct((m, A_HALF), BF16),
        name="gmlp_gate",
        compiler_params=_params("parallel"),
    )(z, ln_g.reshape(1, A_HALF), ln_b.reshape(1, A_HALF), w_mix, b_s_t)


def _attn_kernel(sinks_ref, q_ref, kp_ref, kc_ref, vp_ref, vc_ref, bias_ref, o_ref, *,
                 blocks_per_seq):
    n = pl.program_id(0) % blocks_per_seq
    first = (n == 0).astype(jnp.int32)
    qi = lax.broadcasted_iota(jnp.int32, (ATTN_BLOCK, ATTN_BLOCK), 0)
    ci = lax.broadcasted_iota(jnp.int32, (ATTN_BLOCK, ATTN_BLOCK), 1)
    upper = ci > qi
    low_half = lax.broadcasted_iota(jnp.int32, (ATTN_BLOCK, 2 * HEAD_DIM), 1) < HEAD_DIM
    ones = jnp.ones((2 * ATTN_BLOCK, HEAD_DIM), BF16)
    for g in range(N_KV_HEADS):
        kv_cols = slice(g * HEAD_DIM, (g + 1) * HEAD_DIM)
        kk = jnp.concatenate([kp_ref[:, kv_cols], kc_ref[:, kv_cols]], axis=0)
        vv = jnp.concatenate([vp_ref[:, kv_cols], vc_ref[:, kv_cols]], axis=0)
        vv_ones = jnp.concatenate([vv, ones, ones, vv], axis=1)
        heads = [g * Q_PER_KV + j for j in range(Q_PER_KV)]
        qg = jnp.concatenate([q_ref[:, h * HEAD_DIM:(h + 1) * HEAD_DIM] for h in heads], axis=0)
        s = lax.dot_general(qg, kk, (((1,), (1,)), ((), ())), preferred_element_type=F32)
        es, maxes = [], []
        for j, h in enumerate(heads):
            sj = s[j * ATTN_BLOCK:(j + 1) * ATTN_BLOCK]
            sm = jnp.where(upper, sj[:, :ATTN_BLOCK], sj[:, ATTN_BLOCK:]) + bias_ref[first, h]
            mx = jnp.maximum(jnp.max(sm, axis=-1, keepdims=True), sinks_ref[h])
            e = jnp.exp(sm - mx)
            es.append(jnp.concatenate([jnp.where(upper, e, 0.0), jnp.where(upper, 0.0, e)],
                                      axis=1).astype(BF16))
            maxes.append(mx)
        o = jnp.dot(jnp.concatenate(es, axis=0), vv_ones, preferred_element_type=F32)
        for j in range(0, Q_PER_KV, 2):
            even = o[j * ATTN_BLOCK:(j + 1) * ATTN_BLOCK]
            odd = o[(j + 1) * ATTN_BLOCK:(j + 2) * ATTN_BLOCK]
            num = jnp.where(low_half, even[:, :2 * HEAD_DIM], odd[:, 2 * HEAD_DIM:])
            total = jnp.where(low_half, even[:, 2 * HEAD_DIM:], odd[:, :2 * HEAD_DIM])
            sink_e = jnp.where(low_half, jnp.exp(sinks_ref[heads[j]] - maxes[j]),
                               jnp.exp(sinks_ref[heads[j + 1]] - maxes[j + 1]))
            cols = slice(heads[j] * HEAD_DIM, (heads[j] + 2) * HEAD_DIM)
            o_ref[:, cols] = (num * (1.0 / (total + sink_e))).astype(o_ref.dtype)


def _attention_bias():
    qi = jnp.arange(ATTN_BLOCK, dtype=jnp.int32)[:, None]
    kc = jnp.arange(ATTN_BLOCK, dtype=jnp.int32)[None, :]
    upper = kc > qi
    dist = (qi - kc + jnp.where(upper, ATTN_BLOCK, 0)).astype(F32)
    head = jnp.arange(1, N_Q_HEADS + 1, dtype=F32)
    slopes = jnp.exp2(-8.0 * head / N_Q_HEADS)
    alibi = -slopes[:, None, None] * dist[None]
    return jnp.stack([alibi, jnp.where(upper[None], -jnp.inf, alibi)])


def _attention(qkv, sinks, *, seq_len):
    assert WINDOW == ATTN_BLOCK
    t = qkv.shape[0]
    blocks_per_seq = seq_len // ATTN_BLOCK
    kcol = Q_DIM // KV_DIM
    vcol = kcol + 1
    prev = lambda i: jnp.maximum(i - 1, 0)
    grid_spec = pltpu.PrefetchScalarGridSpec(
        num_scalar_prefetch=1,
        grid=(t // ATTN_BLOCK,),
        in_specs=[
            pl.BlockSpec((ATTN_BLOCK, Q_DIM), lambda i, s: (i, 0)),
            pl.BlockSpec((ATTN_BLOCK, KV_DIM), lambda i, s: (prev(i), kcol)),
            pl.BlockSpec((ATTN_BLOCK, KV_DIM), lambda i, s: (i, kcol)),
            pl.BlockSpec((ATTN_BLOCK, KV_DIM), lambda i, s: (prev(i), vcol)),
            pl.BlockSpec((ATTN_BLOCK, KV_DIM), lambda i, s: (i, vcol)),
            pl.BlockSpec((2, N_Q_HEADS, ATTN_BLOCK, ATTN_BLOCK), lambda i, s: (0, 0, 0, 0)),
        ],
        out_specs=pl.BlockSpec((ATTN_BLOCK, Q_DIM), lambda i, s: (i, 0)),
    )
    return pl.pallas_call(
        functools.partial(_attn_kernel, blocks_per_seq=blocks_per_seq),
        grid_spec=grid_spec,
        out_shape=jax.ShapeDtypeStruct((t, Q_DIM), BF16),
        name="swa_attention",
        compiler_params=_params("parallel"),
    )(sinks, qkv, qkv, qkv, qkv, qkv, _attention_bias())


def _router_kernel(x_ref, wr_ref, rb_ref, idx_ref, wts_ref, rank_ref, cnt_ref, carry_ref, *, tm):
    @pl.when(pl.program_id(0) == 0)
    def _():
        carry_ref[...] = jnp.zeros_like(carry_ref)

    x = x_ref[...]
    x_hi = x.astype(BF16)
    x_lo = (x - x_hi.astype(F32)).astype(BF16)
    w_split = wr_ref[...]
    by_hi = jnp.dot(x_hi, w_split, preferred_element_type=F32)
    by_lo = jnp.dot(x_lo, w_split[:, :LANES], preferred_element_type=F32)
    logits = by_hi[:, :LANES] + (by_hi[:, LANES:] + by_lo)
    sc = jax.nn.sigmoid(logits.T[:N_EXPERTS])
    sel = sc + rb_ref[...]
    neg = -jnp.inf
    iota_g = lax.broadcasted_iota(jnp.int32, (GROUP_SIZE, tm), 0).astype(F32)

    def first_argmax(v, iota, size):
        m = jnp.max(v, axis=0, keepdims=True)
        return jnp.min(jnp.where(v == m, iota, float(size)), axis=0, keepdims=True)

    group_rows = []
    for g in range(N_GROUPS):
        v = sel[g * GROUP_SIZE:(g + 1) * GROUP_SIZE]
        m1 = jnp.max(v, axis=0, keepdims=True)
        i1 = first_argmax(v, iota_g, GROUP_SIZE)
        m2 = jnp.max(jnp.where(iota_g == i1, neg, v), axis=0, keepdims=True)
        group_rows.append(m1 + m2)
    cur = jnp.concatenate(group_rows, axis=0)
    iota_n = lax.broadcasted_iota(jnp.int32, (N_GROUPS, tm), 0).astype(F32)
    gsel = jnp.zeros((N_GROUPS, tm), F32)
    for _ in range(TOPK_GROUPS):
        hit = iota_n == first_argmax(cur, iota_n, N_GROUPS)
        gsel = jnp.where(hit, 1.0, gsel)
        cur = jnp.where(hit, neg, cur)
    cur = jnp.concatenate(
        [jnp.where(gsel[g:g + 1] > 0.5, sel[g * GROUP_SIZE:(g + 1) * GROUP_SIZE], neg)
         for g in range(N_GROUPS)], axis=0)

    iota_e = lax.broadcasted_iota(jnp.int32, (N_EXPERTS, tm), 0).astype(F32)
    member = jnp.zeros((N_EXPERTS, tm), F32)
    hits, idx_rows, w_rows = [], [], []
    for _ in range(TOP_K):
        ii = first_argmax(cur, iota_e, N_EXPERTS)
        hit = iota_e == ii
        hits.append(hit)
        idx_rows.append(ii)
        w_rows.append(jnp.sum(jnp.where(hit, sc, 0.0), axis=0, keepdims=True))
        member = jnp.where(hit, 1.0, member)
        cur = jnp.where(hit, neg, cur)
    w = jnp.concatenate(w_rows, axis=0)
    w = w / (jnp.sum(w, axis=0, keepdims=True) + 1e-20) * ROUTED_SCALE

    r = lax.broadcasted_iota(jnp.int32, (tm, tm), 0)
    c = lax.broadcasted_iota(jnp.int32, (tm, tm), 1)
    upper = jnp.where(r < c, 1.0, 0.0).astype(BF16)
    before = jnp.dot(member.astype(BF16), upper, preferred_element_type=F32)
    rank_full = carry_ref[...] + before
    rank_rows = [jnp.sum(jnp.where(h, rank_full, 0.0), axis=0, keepdims=True) for h in hits]

    idx_ref[...] = jnp.concatenate(idx_rows, axis=0).astype(jnp.int32)
    wts_ref[...] = w
    rank_ref[...] = jnp.concatenate(rank_rows, axis=0).astype(jnp.int32)
    carry_ref[...] = carry_ref[...] + jnp.sum(member, axis=1, keepdims=True)
    cnt_ref[...] = jnp.broadcast_to(carry_ref[...], cnt_ref.shape)


def _router(x, w_router, router_bias, *, tm):
    t, d = x.shape
    tok = lambda i: (0, i)
    w_padded = jnp.pad(w_router, ((0, 0), (0, LANES - N_EXPERTS)))
    w_hi = w_padded.astype(BF16)
    w_lo = (w_padded - w_hi.astype(F32)).astype(BF16)
    w_split = jnp.concatenate([w_hi, w_lo], axis=1)
    return pl.pallas_call(
        functools.partial(_router_kernel, tm=tm),
        grid=(t // tm,),
        in_specs=[
            pl.BlockSpec((tm, d), lambda i: (i, 0)),
            pl.BlockSpec((d, 2 * LANES), lambda i: (0, 0)),
            pl.BlockSpec((N_EXPERTS, 1), lambda i: (0, 0)),
        ],
        out_specs=[
            pl.BlockSpec((TOP_K, tm), tok),
            pl.BlockSpec((TOP_K, tm), tok),
            pl.BlockSpec((TOP_K, tm), tok),
            pl.BlockSpec((N_EXPERTS, 128), lambda i: (0, 0)),
        ],
        out_shape=[
            jax.ShapeDtypeStruct((TOP_K, t), jnp.int32),
            jax.ShapeDtypeStruct((TOP_K, t), F32),
            jax.ShapeDtypeStruct((TOP_K, t), jnp.int32),
            jax.ShapeDtypeStruct((N_EXPERTS, 128), F32),
        ],
        scratch_shapes=[pltpu.VMEM((N_EXPERTS, 1), F32)],
        name="router",
        compiler_params=_params("arbitrary"),
    )(x, w_split, router_bias.reshape(N_EXPERTS, 1))


def _experts_kernel(be_ref, ne_ref, nu_ref, plan_hbm, x_hbm, wg_hbm, wu_hbm, wd_hbm, y_hbm,
                    plan_smem, xbuf0, xbuf1, ybuf0, ybuf1, wg_f32, wu_f32, wd_f32,
                    wg_bf, wu_bf, wd_bf, psem, gsem, ssem, wsem, *, tok_mask, layer):
    b = pl.program_id(0)
    n_used = nu_ref[0]
    rows = MOE_ROWS
    xbuf = (xbuf0, xbuf1)
    ybuf = (ybuf0, ybuf1)

    def weight_copies(expert):
        return [pltpu.make_async_copy(w_hbm.at[layer, expert], stage, wsem.at[i])
                for i, (w_hbm, stage) in enumerate(
                    ((wg_hbm, wg_f32), (wu_hbm, wu_f32), (wd_hbm, wd_f32)))]

    def plan_copy(row, slot):
        return pltpu.make_async_copy(plan_hbm.at[row], plan_smem.at[slot], psem.at[slot])

    def start_gathers(pslot, slot):
        for r in range(rows):
            src = pl.multiple_of(plan_smem[pslot, r] & tok_mask, PACK_SUB)
            pltpu.make_async_copy(x_hbm.at[pl.ds(src, PACK_SUB)],
                                  xbuf[slot].at[pl.ds(r * PACK_SUB, PACK_SUB)],
                                  gsem.at[slot]).start()

    def start_scatters(pslot, slot):
        for r in range(rows):
            dst = pl.multiple_of(plan_smem[pslot, rows + r], PACK_SUB)
            pltpu.make_async_copy(ybuf[slot].at[pl.ds(r * PACK_SUB, PACK_SUB)],
                                  y_hbm.at[pl.ds(dst, PACK_SUB)],
                                  ssem.at[slot]).start(priority=1)

    def wait_rows(buf, sem):
        pltpu.make_async_copy(buf, buf, sem).wait()

    @pl.when(b == 0)
    def _():
        ybuf0[...] = jnp.zeros_like(ybuf0)
        ybuf1[...] = jnp.zeros_like(ybuf1)
        spare0 = y_hbm.shape[0] - 2 * rows * PACK_SUB
        pltpu.make_async_copy(ybuf0, y_hbm.at[pl.ds(spare0, rows * PACK_SUB)], ssem.at[0]).start()
        plan_copy(0, 0).start()
        for copy in weight_copies(be_ref[0]):
            copy.start()
        plan_copy(0, 0).wait()
        start_gathers(0, 0)
        plan_copy(1, 1).start()

    def step(slot):
        other = 1 - slot
        changed = jnp.logical_or(b == 0, be_ref[b] != be_ref[jnp.maximum(b - 1, 0)])

        @pl.when(changed)
        def _():
            for copy in weight_copies(be_ref[b]):
                copy.wait()
            wg_bf[...] = wg_f32[...].astype(BF16)
            wu_bf[...] = wu_f32[...].astype(BF16)
            wd_bf[...] = wd_f32[...].astype(BF16)
            upcoming = ne_ref[b]

            @pl.when(upcoming >= 0)
            def _():
                for copy in weight_copies(upcoming):
                    copy.start()

        wait_rows(xbuf[slot], gsem.at[slot])
        wait_rows(ybuf[slot], ssem.at[slot])
        plan_copy(b + 1, other).wait()
        start_gathers(other, other)
        start_scatters(other, other)
        plan_copy(b + 2, slot).start()

        x = jnp.concatenate([p.astype(BF16) for p in _load_packed_rows(xbuf[slot], rows)], axis=1)
        gate = jnp.dot(x, wg_bf[...], preferred_element_type=F32)
        up = jnp.dot(x, wu_bf[...], preferred_element_type=F32)
        h = (_silu(gate) * up).astype(BF16)
        _store_packed_rows(ybuf[slot], jnp.dot(h, wd_bf[...], preferred_element_type=F32))

        @pl.when(b == n_used - 1)
        def _():
            plan_copy(b + 2, slot).wait()
            start_scatters(slot, slot)
            wait_rows(xbuf[other], gsem.at[other])
            wait_rows(ybuf[other], ssem.at[other])
            wait_rows(ybuf[slot], ssem.at[slot])

    for parity in range(2):
        pl.when(jnp.logical_and(b < n_used, b % 2 == parity))(functools.partial(step, parity))


def _experts(x_packed, layer, w_gate, w_up, w_down, block_expert, next_expert, n_used, plan, *,
             n_out_rows):
    t = x_packed.shape[0] // PACK_SUB
    d = D_MODEL
    assert t & (t - 1) == 0, "token id is taken as the low bits of the flat (choice, token) index"
    n_blocks = plan.shape[0] - 2
    row_buf = pltpu.VMEM((MOE_ROWS * PACK_SUB, LANES), jnp.int32)
    grid_spec = pltpu.PrefetchScalarGridSpec(
        num_scalar_prefetch=3,
        grid=(n_blocks,),
        in_specs=[pl.BlockSpec(memory_space=pl.ANY)] * 5,
        out_specs=pl.BlockSpec(memory_space=pl.ANY),
        scratch_shapes=[
            pltpu.SMEM((2, 2 * MOE_ROWS), jnp.int32),
            row_buf, row_buf, row_buf, row_buf,
            pltpu.VMEM((d, EXPERT_FF), F32),
            pltpu.VMEM((d, EXPERT_FF), F32),
            pltpu.VMEM((EXPERT_FF, d), F32),
            pltpu.VMEM((d, EXPERT_FF), BF16),
            pltpu.VMEM((d, EXPERT_FF), BF16),
            pltpu.VMEM((EXPERT_FF, d), BF16),
            pltpu.SemaphoreType.DMA((2,)),
            pltpu.SemaphoreType.DMA((2,)),
            pltpu.SemaphoreType.DMA((2,)),
            pltpu.SemaphoreType.DMA((3,)),
        ],
    )
    return pl.pallas_call(
        functools.partial(_experts_kernel, tok_mask=(t - 1) * PACK_SUB, layer=layer),
        grid_spec=grid_spec,
        out_shape=jax.ShapeDtypeStruct((n_out_rows * PACK_SUB, LANES), jnp.int32),
        name="routed_experts",
        compiler_params=_params("arbitrary"),
    )(block_expert, next_expert, n_used, plan, x_packed, w_gate, w_up, w_down)


def _combine_kernel(x_ref, wg_ref, wu_ref, wd_ref, w_ref, *refs, tm):
    y_refs = refs[:TOP_K]
    g_ref, beta_ref, o_ref = refs[TOP_K:]
    x = x_ref[...]
    xb = x.astype(BF16)
    gate = jnp.dot(xb, wg_ref[...], preferred_element_type=F32)
    up = jnp.dot(xb, wu_ref[...], preferred_element_type=F32)
    h = (_silu(gate) * up).astype(BF16)
    s = DN_ALPHA * x + jnp.dot(h, wd_ref[...], preferred_element_type=F32)
    pieces = [s[:, c * LANES:(c + 1) * LANES] for c in range(2 * PACK_SUB)]
    for k in range(TOP_K):
        wk = w_ref[:, k:k + 1]
        yk = _load_packed_rows(y_refs[k], tm)
        pieces = [p + wk * y for p, y in zip(pieces, yk)]
    o_ref[...] = _layer_norm(jnp.concatenate(pieces, axis=1), g_ref[...], beta_ref[...])


def _combine(x, ws_gate, ws_up, ws_down, wts_t, y, g, beta, *, tm):
    t, d = x.shape
    f = ws_gate.shape[1]
    blocks = t // tm
    fixed = lambda i: (0, 0)
    y_specs = [pl.BlockSpec((tm * PACK_SUB, LANES),
                            functools.partial(lambda i, k: (k * blocks + i, 0), k=k))
               for k in range(TOP_K)]
    return pl.pallas_call(
        functools.partial(_combine_kernel, tm=tm),
        grid=(blocks,),
        in_specs=[
            pl.BlockSpec((tm, d), lambda i: (i, 0)),
            pl.BlockSpec((d, f), fixed),
            pl.BlockSpec((d, f), fixed),
            pl.BlockSpec((f, d), fixed),
            pl.BlockSpec((tm, TOP_K), lambda i: (i, 0)),
            *y_specs,
            pl.BlockSpec((1, d), fixed),
            pl.BlockSpec((1, d), fixed),
        ],
        out_specs=pl.BlockSpec((tm, d), lambda i: (i, 0)),
        out_shape=jax.ShapeDtypeStruct((t, d), F32),
        name="moe_combine",
        compiler_params=_params("parallel"),
    )(x, ws_gate, ws_up, ws_down, wts_t, *([y] * TOP_K), g.reshape(1, d), beta.reshape(1, d))


def _dispatch_plan(idx, rank, counts, n_tok):
    n_assign = n_tok * TOP_K
    n_blocks = n_assign // MOE_ROWS + N_EXPERTS
    n_slots = n_blocks * MOE_ROWS
    blocks_per_expert = (counts + MOE_ROWS - 1) // MOE_ROWS
    block_end = jnp.cumsum(blocks_per_expert)
    row_start = (block_end - blocks_per_expert) * MOE_ROWS
    experts = jnp.arange(N_EXPERTS, dtype=jnp.int32)
    row_start_of = jnp.sum(jnp.where(idx[..., None] == experts, row_start, 0), axis=-1)
    dest = (row_start_of + rank).reshape(-1)
    slot = jnp.arange(n_slots, dtype=jnp.int32)
    n_pad = n_slots - n_assign
    pad_counts = jnp.concatenate([blocks_per_expert * MOE_ROWS - counts,
                                  n_slots - block_end[-1:] * MOE_ROWS])
    pad_first = jnp.concatenate([row_start + counts, block_end[-1:] * MOE_ROWS])
    pad_end = jnp.cumsum(pad_counts)
    j = jnp.arange(n_pad, dtype=jnp.int32)
    pad_begin = pad_end - pad_counts
    in_owner = (j[:, None] >= pad_begin[None, :]) & (j[:, None] < pad_end[None, :])
    pad_slot = jnp.sum(jnp.where(in_owner, (pad_first - pad_begin)[None, :] + j[:, None], 0),
                       axis=1)
    spare = n_assign + ((pad_slot // MOE_ROWS) % 2) * MOE_ROWS + pad_slot % MOE_ROWS
    flat = jnp.arange(n_assign, dtype=jnp.int32)
    _, slot_flat = lax.sort_key_val(jnp.concatenate([dest, pad_slot]),
                                    jnp.concatenate([flat, spare]))
    slot_flat = slot_flat.reshape(n_blocks, MOE_ROWS)
    rows = slot_flat * PACK_SUB
    pad = jnp.broadcast_to((n_assign + MOE_ROWS + slot[:MOE_ROWS]) * PACK_SUB, (2, MOE_ROWS))
    plan = jnp.concatenate([jnp.concatenate([rows, pad], axis=0),
                            jnp.concatenate([pad, rows], axis=0)], axis=1)
    blocks = jnp.arange(n_blocks, dtype=jnp.int32)
    block_expert = jnp.minimum(
        jnp.sum((block_end[None, :] <= blocks[:, None]).astype(jnp.int32), axis=1), N_EXPERTS - 1)
    n_used = block_end[-1:].astype(jnp.int32)
    later_used = (experts[None, :] > experts[:, None]) & (blocks_per_expert[None, :] > 0)
    following = jnp.min(jnp.where(later_used, experts[None, :], N_EXPERTS), axis=1)
    following = jnp.where(following == N_EXPERTS, -1, following)
    next_expert = jnp.sum(jnp.where(block_expert[:, None] == experts[None, :], following, 0),
                          axis=1).astype(jnp.int32)
    return block_expert, next_expert, n_used, plan, n_assign + 2 * MOE_ROWS


def _moe(x, x_packed, layer, w_router, router_bias, w_gate, w_up, w_down, ws_gate, ws_up, ws_down,
         g, beta):
    n_tok = x.shape[0]
    idx, wts, rank, cnt = _router(x, w_router, router_bias, tm=512)
    counts = cnt[:, 0].astype(jnp.int32)
    block_expert, next_expert, n_used, plan, n_out_rows = _dispatch_plan(idx, rank, counts, n_tok)
    y = _experts(x_packed, layer, w_gate, w_up, w_down, block_expert, next_expert, n_used, plan,
                 n_out_rows=n_out_rows)
    return _combine(x, ws_gate.astype(BF16), ws_up.astype(BF16), ws_down.astype(BF16), wts.T, y,
                    g, beta, tm=256)


def kernel(x, a_w_in, a_b_in, a_ln_g, a_ln_b, a_w_s, a_b_s, a_w_out, b_w_qkv, b_b_qkv, b_sinks,
           b_w_o, b_b_o, moe_w_router, moe_router_bias, moe_w_gate, moe_w_up, moe_w_down,
           moe_ws_gate, moe_ws_up, moe_ws_down, norm_g, norm_b):
    bsz, seq_len, d = x.shape
    h = x.reshape(bsz * seq_len, d)
    for i in range(DEPTH):
        j = i // 2
        if i % 2 == 0:
            z = _linear(h, a_w_in[j].astype(BF16), a_b_in[j], act="gelu", tm=1024, tn=1024,
                        out_dtype=BF16)
            causal = jnp.tril(jnp.ones((A_CHUNK, A_CHUNK), dtype=bool))
            w_mix = jnp.where(causal[None], a_w_s[j], 0.0).astype(BF16)
            gated = _gmlp_gate(z, a_ln_g[j], a_ln_b[j], w_mix, a_b_s[j].T, tm=256)
            h, h_packed = _linear_res_ln(gated, a_w_out[j].astype(BF16), jnp.zeros((d,), F32), h,
                                         norm_g[i, 0], norm_b[i, 0], tm=512)
        else:
            q_scale = jnp.where(jnp.arange(QKV_DIM) < Q_DIM, HEAD_DIM ** -0.5, 1.0).astype(F32)
            qkv = _linear(h, (b_w_qkv[j] * q_scale).astype(BF16), b_b_qkv[j] * q_scale, act=None,
                          tm=1024, tn=1280, out_dtype=BF16)
            o = _attention(qkv, b_sinks[j], seq_len=seq_len)
            h, h_packed = _linear_res_ln(o, b_w_o[j].astype(BF16), b_b_o[j], h,
                                         norm_g[i, 0], norm_b[i, 0], tm=512)
        h = _moe(h, h_packed, i, moe_w_router[i], moe_router_bias[i], moe_w_gate, moe_w_up, moe_w_down,
                 moe_ws_gate[i], moe_ws_up[i], moe_ws_down[i], norm_g[i, 1], norm_b[i, 1])
    return h.reshape(bsz, seq_len, d)
```

```python
import functools

import jax
import jax.numpy as jnp
from jax import lax
from jax.experimental import pallas as pl
from jax.experimental.pallas import tpu as pltpu

F32 = jnp.float32
BF16 = jnp.bfloat16

D_MODEL = 2048
DEPTH = 2
A_CHUNK = 128
A_HALF = D_MODEL
A_GROUPS = A_HALF // 128
HEAD_DIM = 64
N_Q_HEADS = D_MODEL // HEAD_DIM
N_KV_HEADS = N_Q_HEADS // 8
Q_PER_KV = N_Q_HEADS // N_KV_HEADS
Q_DIM = N_Q_HEADS * HEAD_DIM
KV_DIM = N_KV_HEADS * HEAD_DIM
QKV_DIM = Q_DIM + 2 * KV_DIM
WINDOW = 128
ATTN_BLOCK = 128
N_EXPERTS = 64
TOP_K = 8
N_GROUPS = 8
GROUP_SIZE = N_EXPERTS // N_GROUPS
TOPK_GROUPS = 4
EXPERT_FF = D_MODEL // 4
ROUTED_SCALE = 2.5
DN_ALPHA = (2 * DEPTH) ** 0.25
LN_EPS = 1e-5

MXU_ROWS = 256
MOE_ROWS = 2 * MXU_ROWS
VMEM_LIMIT = 56 * 1024 * 1024


def _params(*sem):
    return pltpu.CompilerParams(dimension_semantics=sem, vmem_limit_bytes=VMEM_LIMIT)


def _layer_norm(y, g, b):
    mu = jnp.mean(y, axis=-1, keepdims=True)
    d = y - mu
    var = jnp.mean(d * d, axis=-1, keepdims=True)
    return d * lax.rsqrt(var + LN_EPS) * g + b


def _gelu(x):
    return 0.5 * x * (1.0 + lax.erf(x * (2.0 ** -0.5)))


def _silu(x):
    return x * jax.nn.sigmoid(x)


LANES = 128
PACK_SUB = D_MODEL // (2 * LANES)
HIGH_HALF = -65536


def _store_packed_rows(p_ref, rows_f32, first_row=0):
    n = rows_f32.shape[0]
    base = first_row * PACK_SUB
    bits = lax.bitcast_convert_type(rows_f32.astype(BF16).astype(F32), jnp.int32)
    for s in range(PACK_SUB):
        hi = bits[:, s * LANES:(s + 1) * LANES] & HIGH_HALF
        lo = lax.shift_right_logical(
            bits[:, D_MODEL // 2 + s * LANES:D_MODEL // 2 + (s + 1) * LANES], 16)
        p_ref[pl.ds(base + s, n, stride=PACK_SUB), :] = hi | lo


def _load_packed_rows(p_ref, n):
    his, los = [], []
    for s in range(PACK_SUB):
        w = p_ref[pl.ds(s, n, stride=PACK_SUB), :]
        his.append(lax.bitcast_convert_type(w & HIGH_HALF, F32))
        los.append(lax.bitcast_convert_type(lax.shift_left(w, 16), F32))
    return his + los


def _linear_kernel(x_ref, w_ref, b_ref, o_ref, *, act):
    acc = jnp.dot(x_ref[...].astype(BF16), w_ref[...], preferred_element_type=F32)
    acc = acc + b_ref[...]
    if act == "gelu":
        acc = _gelu(acc)
    o_ref[...] = acc.astype(o_ref.dtype)


def _linear(x, w, b, *, act, tm, tn, out_dtype):
    m, k = x.shape
    n = w.shape[1]
    return pl.pallas_call(
        functools.partial(_linear_kernel, act=act),
        grid=(m // tm, n // tn),
        in_specs=[
            pl.BlockSpec((tm, k), lambda i, j: (i, 0)),
            pl.BlockSpec((k, tn), lambda i, j: (0, j)),
            pl.BlockSpec((1, tn), lambda i, j: (0, j)),
        ],
        out_specs=pl.BlockSpec((tm, tn), lambda i, j: (i, j)),
        out_shape=jax.ShapeDtypeStruct((m, n), out_dtype),
        name="linear_" + str(act),
        compiler_params=_params("parallel", "arbitrary"),
    )(x, w, b.reshape(1, n))


def _linear_res_ln_kernel(x_ref, w_ref, b_ref, res_ref, g_ref, beta_ref, o_ref, p_ref):
    sub = min(MXU_ROWS, x_ref.shape[0])
    for r0 in range(0, x_ref.shape[0], sub):
        rows = slice(r0, r0 + sub)
        acc = jnp.dot(x_ref[rows, :], w_ref[...], preferred_element_type=F32)
        y = DN_ALPHA * res_ref[rows, :] + (acc + b_ref[...])
        out = _layer_norm(y, g_ref[...], beta_ref[...])
        o_ref[rows, :] = out
        _store_packed_rows(p_ref, out, first_row=r0)


def _linear_res_ln(x, w, b, res, g, beta, *, tm):
    m, k = x.shape
    n = w.shape[1]
    row = lambda i: (i, 0)
    fixed = lambda i: (0, 0)
    return pl.pallas_call(
        _linear_res_ln_kernel,
        grid=(m // tm,),
        in_specs=[
            pl.BlockSpec((tm, k), row),
            pl.BlockSpec((k, n), fixed),
            pl.BlockSpec((1, n), fixed),
            pl.BlockSpec((tm, n), row),
            pl.BlockSpec((1, n), fixed),
            pl.BlockSpec((1, n), fixed),
        ],
        out_specs=[pl.BlockSpec((tm, n), row), pl.BlockSpec((tm * PACK_SUB, LANES), row)],
        out_shape=[jax.ShapeDtypeStruct((m, n), F32),
                   jax.ShapeDtypeStruct((m * PACK_SUB, LANES), jnp.int32)],
        name="linear_res_ln",
        compiler_params=_params("parallel"),
    )(x, w, b.reshape(1, n), res, g.reshape(1, n), beta.reshape(1, n))


def _gmlp_gate_kernel(z_ref, lng_ref, lnb_ref, wmix_ref, bst_ref, o_ref, *, tm):
    v = z_ref[:, A_HALF:].astype(F32)
    vn = _layer_norm(v, lng_ref[...], lnb_ref[...]).astype(BF16)
    for c in range(tm // A_CHUNK):
        rows = slice(c * A_CHUNK, (c + 1) * A_CHUNK)
        for g in range(A_GROUPS):
            cols = slice(g * 128, (g + 1) * 128)
            mixed = jnp.dot(wmix_ref[g], vn[rows, cols], preferred_element_type=F32)
            mixed = mixed + bst_ref[:, g:g + 1]
            u = z_ref[rows, cols].astype(F32)
            o_ref[rows, cols] = (u * mixed).astype(o_ref.dtype)


def _gmlp_gate(z, ln_g, ln_b, w_mix, b_s_t, *, tm):
    m = z.shape[0]
    return pl.pallas_call(
        functools.partial(_gmlp_gate_kernel, tm=tm),
        grid=(m // tm,),
        in_specs=[
            pl.BlockSpec((tm, 2 * A_HALF), lambda i: (i, 0)),
            pl.BlockSpec((1, A_HALF), lambda i: (0, 0)),
            pl.BlockSpec((1, A_HALF), lambda i: (0, 0)),
            pl.BlockSpec((A_GROUPS, A_CHUNK, A_CHUNK), lambda i: (0, 0, 0)),
            pl.BlockSpec((A_CHUNK, A_GROUPS), lambda i: (0, 0)),
        ],
        out_specs=pl.BlockSpec((tm, A_HALF), lambda i: (i, 0)),
        out_shape=jax.ShapeDtypeStruct((m, A_HALF), BF16),
        name="gmlp_gate",
        compiler_params=_params("parallel"),
    )(z, ln_g.reshape(1, A_HALF), ln_b.reshape(1, A_HALF), w_mix, b_s_t)


def _attn_kernel(sinks_ref, q_ref, kp_ref, kc_ref, vp_ref, vc_ref, bias_ref, o_ref, *,
                 blocks_per_seq):
    n = pl.program_id(0) % blocks_per_seq
    first = (n == 0).astype(jnp.int32)
    qi = lax.broadcasted_iota(jnp.int32, (ATTN_BLOCK, ATTN_BLOCK), 0)
    ci = lax.broadcasted_iota(jnp.int32, (ATTN_BLOCK, ATTN_BLOCK), 1)
    upper = ci > qi
    low_half = lax.broadcasted_iota(jnp.int32, (ATTN_BLOCK, 2 * HEAD_DIM), 1) < HEAD_DIM
    ones = jnp.ones((2 * ATTN_BLOCK, HEAD_DIM), BF16)
    for g in range(N_KV_HEADS):
        kv_cols = slice(g * HEAD_DIM, (g + 1) * HEAD_DIM)
        kk = jnp.concatenate([kp_ref[:, kv_cols], kc_ref[:, kv_cols]], axis=0)
        vv = jnp.concatenate([vp_ref[:, kv_cols], vc_ref[:, kv_cols]], axis=0)
        vv_ones = jnp.concatenate([vv, ones, ones, vv], axis=1)
        heads = [g * Q_PER_KV + j for j in range(Q_PER_KV)]
        qg = jnp.concatenate([q_ref[:, h * HEAD_DIM:(h + 1) * HEAD_DIM] for h in heads], axis=0)
        s = lax.dot_general(qg, kk, (((1,), (1,)), ((), ())), preferred_element_type=F32)
        es, maxes = [], []
        for j, h in enumerate(heads):
            sj = s[j * ATTN_BLOCK:(j + 1) * ATTN_BLOCK]
            sm = jnp.where(upper, sj[:, :ATTN_BLOCK], sj[:, ATTN_BLOCK:]) + bias_ref[first, h]
            mx = jnp.maximum(jnp.max(sm, axis=-1, keepdims=True), sinks_ref[h])
            e = jnp.exp(sm - mx)
            es.append(jnp.concatenate([jnp.where(upper, e, 0.0), jnp.where(upper, 0.0, e)],
                                      axis=1).astype(BF16))
            maxes.append(mx)
        o = jnp.dot(jnp.concatenate(es, axis=0), vv_ones, preferred_element_type=F32)
        for j in range(0, Q_PER_KV, 2):
            even = o[j * ATTN_BLOCK:(j + 1) * ATTN_BLOCK]
            odd = o[(j + 1) * ATTN_BLOCK:(j + 2) * ATTN_BLOCK]
            num = jnp.where(low_half, even[:, :2 * HEAD_DIM], odd[:, 2 * HEAD_DIM:])
            total = jnp.where(low_half, even[:, 2 * HEAD_DIM:], odd[:, :2 * HEAD_DIM])
            sink_e = jnp.where(low_half, jnp.exp(sinks_ref[heads[j]] - maxes[j]),
                               jnp.exp(sinks_ref[heads[j + 1]] - maxes[j + 1]))
            cols = slice(heads[j] * HEAD_DIM, (heads[j] + 2) * HEAD_DIM)
            o_ref[:, cols] = (num * (1.0 / (total + sink_e))).astype(o_ref.dtype)


def _attention_bias():
    qi = jnp.arange(ATTN_BLOCK, dtype=jnp.int32)[:, None]
    kc = jnp.arange(ATTN_BLOCK, dtype=jnp.int32)[None, :]
    upper = kc > qi
    dist = (qi - kc + jnp.where(upper, ATTN_BLOCK, 0)).astype(F32)
    head = jnp.arange(1, N_Q_HEADS + 1, dtype=F32)
    slopes = jnp.exp2(-8.0 * head / N_Q_HEADS)
    alibi = -slopes[:, None, None] * dist[None]
    return jnp.stack([alibi, jnp.where(upper[None], -jnp.inf, alibi)])


def _attention(qkv, sinks, *, seq_len):
    assert WINDOW == ATTN_BLOCK
    t = qkv.shape[0]
    blocks_per_seq = seq_len // ATTN_BLOCK
    kcol = Q_DIM // KV_DIM
    vcol = kcol + 1
    prev = lambda i: jnp.maximum(i - 1, 0)
    grid_spec = pltpu.PrefetchScalarGridSpec(
        num_scalar_prefetch=1,
        grid=(t // ATTN_BLOCK,),
        in_specs=[
            pl.BlockSpec((ATTN_BLOCK, Q_DIM), lambda i, s: (i, 0)),
            pl.BlockSpec((ATTN_BLOCK, KV_DIM), lambda i, s: (prev(i), kcol)),
            pl.BlockSpec((ATTN_BLOCK, KV_DIM), lambda i, s: (i, kcol)),
            pl.BlockSpec((ATTN_BLOCK, KV_DIM), lambda i, s: (prev(i), vcol)),
            pl.BlockSpec((ATTN_BLOCK, KV_DIM), lambda i, s: (i, vcol)),
            pl.BlockSpec((2, N_Q_HEADS, ATTN_BLOCK, ATTN_BLOCK), lambda i, s: (0, 0, 0, 0)),
        ],
        out_specs=pl.BlockSpec((ATTN_BLOCK, Q_DIM), lambda i, s: (i, 0)),
    )
    return pl.pallas_call(
        functools.partial(_attn_kernel, blocks_per_seq=blocks_per_seq),
        grid_spec=grid_spec,
        out_shape=jax.ShapeDtypeStruct((t, Q_DIM), BF16),
        name="swa_attention",
        compiler_params=_params("parallel"),
    )(sinks, qkv, qkv, qkv, qkv, qkv, _attention_bias())


def _router_kernel(x_ref, wr_ref, rb_ref, idx_ref, wts_ref, rank_ref, cnt_ref, carry_ref, *, tm):
    @pl.when(pl.program_id(0) == 0)
    def _():
        carry_ref[...] = jnp.zeros_like(carry_ref)

    x = x_ref[...]
    x_hi = x.astype(BF16)
    x_lo = (x - x_hi.astype(F32)).astype(BF16)
    w_split = wr_ref[...]
    by_hi = jnp.dot(x_hi, w_split, preferred_element_type=F32)
    by_lo = jnp.dot(x_lo, w_split[:, :LANES], preferred_element_type=F32)
    logits = by_hi[:, :LANES] + (by_hi[:, LANES:] + by_lo)
    sc = jax.nn.sigmoid(logits.T[:N_EXPERTS])
    sel = sc + rb_ref[...]
    neg = -jnp.inf
    iota_g = lax.broadcasted_iota(jnp.int32, (GROUP_SIZE, tm), 0).astype(F32)

    def first_argmax(v, iota, size):
        m = jnp.max(v, axis=0, keepdims=True)
        return jnp.min(jnp.where(v == m, iota, float(size)), axis=0, keepdims=True)

    group_rows = []
    for g in range(N_GROUPS):
        v = sel[g * GROUP_SIZE:(g + 1) * GROUP_SIZE]
        m1 = jnp.max(v, axis=0, keepdims=True)
        i1 = first_argmax(v, iota_g, GROUP_SIZE)
        m2 = jnp.max(jnp.where(iota_g == i1, neg, v), axis=0, keepdims=True)
        group_rows.append(m1 + m2)
    cur = jnp.concatenate(group_rows, axis=0)
    iota_n = lax.broadcasted_iota(jnp.int32, (N_GROUPS, tm), 0).astype(F32)
    gsel = jnp.zeros((N_GROUPS, tm), F32)
    for _ in range(TOPK_GROUPS):
        hit = iota_n == first_argmax(cur, iota_n, N_GROUPS)
        gsel = jnp.where(hit, 1.0, gsel)
        cur = jnp.where(hit, neg, cur)
    cur = jnp.concatenate(
        [jnp.where(gsel[g:g + 1] > 0.5, sel[g * GROUP_SIZE:(g + 1) * GROUP_SIZE], neg)
         for g in range(N_GROUPS)], axis=0)

    iota_e = lax.broadcasted_iota(jnp.int32, (N_EXPERTS, tm), 0).astype(F32)
    member = jnp.zeros((N_EXPERTS, tm), F32)
    hits, idx_rows, w_rows = [], [], []
    for _ in range(TOP_K):
        ii = first_argmax(cur, iota_e, N_EXPERTS)
        hit = iota_e == ii
        hits.append(hit)
        idx_rows.append(ii)
        w_rows.append(jnp.sum(jnp.where(hit, sc, 0.0), axis=0, keepdims=True))
        member = jnp.where(hit, 1.0, member)
        cur = jnp.where(hit, neg, cur)
    w = jnp.concatenate(w_rows, axis=0)
    w = w / (jnp.sum(w, axis=0, keepdims=True) + 1e-20) * ROUTED_SCALE

    r = lax.broadcasted_iota(jnp.int32, (tm, tm), 0)
    c = lax.broadcasted_iota(jnp.int32, (tm, tm), 1)
    upper = jnp.where(r < c, 1.0, 0.0).astype(BF16)
    before = jnp.dot(member.astype(BF16), upper, preferred_element_type=F32)
    rank_full = carry_ref[...] + before
    rank_rows = [jnp.sum(jnp.where(h, rank_full, 0.0), axis=0, keepdims=True) for h in hits]

    idx_ref[...] = jnp.concatenate(idx_rows, axis=0).astype(jnp.int32)
    wts_ref[...] = w
    rank_ref[...] = jnp.concatenate(rank_rows, axis=0).astype(jnp.int32)
    carry_ref[...] = carry_ref[...] + jnp.sum(member, axis=1, keepdims=True)
    cnt_ref[...] = jnp.broadcast_to(carry_ref[...], cnt_ref.shape)


def _router(x, w_router, router_bias, *, tm):
    t, d = x.shape
    tok = lambda i: (0, i)
    w_padded = jnp.pad(w_router, ((0, 0), (0, LANES - N_EXPERTS)))
    w_hi = w_padded.astype(BF16)
    w_lo = (w_padded - w_hi.astype(F32)).astype(BF16)
    w_split = jnp.concatenate([w_hi, w_lo], axis=1)
    return pl.pallas_call(
        functools.partial(_router_kernel, tm=tm),
        grid=(t // tm,),
        in_specs=[
            pl.BlockSpec((tm, d), lambda i: (i, 0)),
            pl.BlockSpec((d, 2 * LANES), lambda i: (0, 0)),
            pl.BlockSpec((N_EXPERTS, 1), lambda i: (0, 0)),
        ],
        out_specs=[
            pl.BlockSpec((TOP_K, tm), tok),
            pl.BlockSpec((TOP_K, tm), tok),
            pl.BlockSpec((TOP_K, tm), tok),
            pl.BlockSpec((N_EXPERTS, 128), lambda i: (0, 0)),
        ],
        out_shape=[
            jax.ShapeDtypeStruct((TOP_K, t), jnp.int32),
            jax.ShapeDtypeStruct((TOP_K, t), F32),
            jax.ShapeDtypeStruct((TOP_K, t), jnp.int32),
            jax.ShapeDtypeStruct((N_EXPERTS, 128), F32),
        ],
        scratch_shapes=[pltpu.VMEM((N_EXPERTS, 1), F32)],
        name="router",
        compiler_params=_params("arbitrary"),
    )(x, w_split, router_bias.reshape(N_EXPERTS, 1))


def _experts_kernel(be_ref, ne_ref, nu_ref, plan_hbm, x_hbm, wg_hbm, wu_hbm, wd_hbm, y_hbm,
                    plan_smem, xbuf0, xbuf1, ybuf0, ybuf1, wg_f32, wu_f32, wd_f32,
                    wg_bf, wu_bf, wd_bf, psem, gsem, ssem, wsem, *, tok_mask, layer):
    b = pl.program_id(0)
    n_used = nu_ref[0]
    rows = MOE_ROWS
    xbuf = (xbuf0, xbuf1)
    ybuf = (ybuf0, ybuf1)

    def weight_copies(expert):
        return [pltpu.make_async_copy(w_hbm.at[layer, expert], stage, wsem.at[i])
                for i, (w_hbm, stage) in enumerate(
                    ((wg_hbm, wg_f32), (wu_hbm, wu_f32), (wd_hbm, wd_f32)))]

    def plan_copy(row, slot):
        return pltpu.make_async_copy(plan_hbm.at[row], plan_smem.at[slot], psem.at[slot])

    def start_gathers(pslot, slot):
        for r in range(rows):
            src = pl.multiple_of(plan_smem[pslot, r] & tok_mask, PACK_SUB)
            pltpu.make_async_copy(x_hbm.at[pl.ds(src, PACK_SUB)],
                                  xbuf[slot].at[pl.ds(r * PACK_SUB, PACK_SUB)],
                                  gsem.at[slot]).start()

    def start_scatters(pslot, slot):
        for r in range(rows):
            dst = pl.multiple_of(plan_smem[pslot, rows + r], PACK_SUB)
            pltpu.make_async_copy(ybuf[slot].at[pl.ds(r * PACK_SUB, PACK_SUB)],
                                  y_hbm.at[pl.ds(dst, PACK_SUB)],
                                  ssem.at[slot]).start(priority=1)

    def wait_rows(buf, sem):
        pltpu.make_async_copy(buf, buf, sem).wait()

    @pl.when(b == 0)
    def _():
        ybuf0[...] = jnp.zeros_like(ybuf0)
        ybuf1[...] = jnp.zeros_like(ybuf1)
        spare0 = y_hbm.shape[0] - 2 * rows * PACK_SUB
        pltpu.make_async_copy(ybuf0, y_hbm.at[pl.ds(spare0, rows * PACK_SUB)], ssem.at[0]).start()
        plan_copy(0, 0).start()
        for copy in weight_copies(be_ref[0]):
            copy.start()
        plan_copy(0, 0).wait()
        start_gathers(0, 0)
        plan_copy(1, 1).start()

    def step(slot):
        other = 1 - slot
        changed = jnp.logical_or(b == 0, be_ref[b] != be_ref[jnp.maximum(b - 1, 0)])

        @pl.when(changed)
        def _():
            for copy in weight_copies(be_ref[b]):
                copy.wait()
            wg_bf[...] = wg_f32[...].astype(BF16)
            wu_bf[...] = wu_f32[...].astype(BF16)
            wd_bf[...] = wd_f32[...].astype(BF16)
            upcoming = ne_ref[b]

            @pl.when(upcoming >= 0)
            def _():
                for copy in weight_copies(upcoming):
                    copy.start()

        wait_rows(xbuf[slot], gsem.at[slot])
        wait_rows(ybuf[slot], ssem.at[slot])
        plan_copy(b + 1, other).wait()
        start_gathers(other, other)
        start_scatters(other, other)
        plan_copy(b + 2, slot).start()

        x = jnp.concatenate([p.astype(BF16) for p in _load_packed_rows(xbuf[slot], rows)], axis=1)
        gate = jnp.dot(x, wg_bf[...], preferred_element_type=F32)
        up = jnp.dot(x, wu_bf[...], preferred_element_type=F32)
        h = (_silu(gate) * up).astype(BF16)
        _store_packed_rows(ybuf[slot], jnp.dot(h, wd_bf[...], preferred_element_type=F32))

        @pl.when(b == n_used - 1)
        def _():
            plan_copy(b + 2, slot).wait()
            start_scatters(slot, slot)
            wait_rows(xbuf[other], gsem.at[other])
            wait_rows(ybuf[other], ssem.at[other])
            wait_rows(ybuf[slot], ssem.at[slot])

    for parity in range(2):
        pl.when(jnp.logical_and(b < n_used, b % 2 == parity))(functools.partial(step, parity))


def _experts(x_packed, layer, w_gate, w_up, w_down, block_expert, next_expert, n_used, plan, *,
             n_out_rows):
    t = x_packed.shape[0] // PACK_SUB
    d = D_MODEL
    assert t & (t - 1) == 0, "token id is taken as the low bits of the flat (choice, token) index"
    n_blocks = plan.shape[0] - 2
    row_buf = pltpu.VMEM((MOE_ROWS * PACK_SUB, LANES), jnp.int32)
    grid_spec = pltpu.PrefetchScalarGridSpec(
        num_scalar_prefetch=3,
        grid=(n_blocks,),
        in_specs=[pl.BlockSpec(memory_space=pl.ANY)] * 5,
        out_specs=pl.BlockSpec(memory_space=pl.ANY),
        scratch_shapes=[
            pltpu.SMEM((2, 2 * MOE_ROWS), jnp.int32),
            row_buf, row_buf, row_buf, row_buf,
            pltpu.VMEM((d, EXPERT_FF), F32),
            pltpu.VMEM((d, EXPERT_FF), F32),
            pltpu.VMEM((EXPERT_FF, d), F32),
            pltpu.VMEM((d, EXPERT_FF), BF16),
            pltpu.VMEM((d, EXPERT_FF), BF16),
            pltpu.VMEM((EXPERT_FF, d), BF16),
            pltpu.SemaphoreType.DMA((2,)),
            pltpu.SemaphoreType.DMA((2,)),
            pltpu.SemaphoreType.DMA((2,)),
            pltpu.SemaphoreType.DMA((3,)),
        ],
    )
    return pl.pallas_call(
        functools.partial(_experts_kernel, tok_mask=(t - 1) * PACK_SUB, layer=layer),
        grid_spec=grid_spec,
        out_shape=jax.ShapeDtypeStruct((n_out_rows * PACK_SUB, LANES), jnp.int32),
        name="routed_experts",
        compiler_params=_params("arbitrary"),
    )(block_expert, next_expert, n_used, plan, x_packed, w_gate, w_up, w_down)


def _combine_kernel(x_ref, wg_ref, wu_ref, wd_ref, w_ref, *refs, tm):
    y_refs = refs[:TOP_K]
    g_ref, beta_ref, o_ref = refs[TOP_K:]
    x = x_ref[...]
    xb = x.astype(BF16)
    gate = jnp.dot(xb, wg_ref[...], preferred_element_type=F32)
    up = jnp.dot(xb, wu_ref[...], preferred_element_type=F32)
    h = (_silu(gate) * up).astype(BF16)
    s = DN_ALPHA * x + jnp.dot(h, wd_ref[...], preferred_element_type=F32)
    pieces = [s[:, c * LANES:(c + 1) * LANES] for c in range(2 * PACK_SUB)]
    for k in range(TOP_K):
        wk = w_ref[:, k:k + 1]
        yk = _load_packed_rows(y_refs[k], tm)
        pieces = [p + wk * y for p, y in zip(pieces, yk)]
    o_ref[...] = _layer_norm(jnp.concatenate(pieces, axis=1), g_ref[...], beta_ref[...])


def _combine(x, ws_gate, ws_up, ws_down, wts_t, y, g, beta, *, tm):
    t, d = x.shape
    f = ws_gate.shape[1]
    blocks = t // tm
    fixed = lambda i: (0, 0)
    y_specs = [pl.BlockSpec((tm * PACK_SUB, LANES),
                            functools.partial(lambda i, k: (k * blocks + i, 0), k=k))
               for k in range(TOP_K)]
    return pl.pallas_call(
        functools.partial(_combine_kernel, tm=tm),
        grid=(blocks,),
        in_specs=[
            pl.BlockSpec((tm, d), lambda i: (i, 0)),
            pl.BlockSpec((d, f), fixed),
            pl.BlockSpec((d, f), fixed),
            pl.BlockSpec((f, d), fixed),
            pl.BlockSpec((tm, TOP_K), lambda i: (i, 0)),
            *y_specs,
            pl.BlockSpec((1, d), fixed),
            pl.BlockSpec((1, d), fixed),
        ],
        out_specs=pl.BlockSpec((tm, d), lambda i: (i, 0)),
        out_shape=jax.ShapeDtypeStruct((t, d), F32),
        name="moe_combine",
        compiler_params=_params("parallel"),
    )(x, ws_gate, ws_up, ws_down, wts_t, *([y] * TOP_K), g.reshape(1, d), beta.reshape(1, d))


def _dispatch_plan(idx, rank, counts, n_tok):
    n_assign = n_tok * TOP_K
    n_blocks = n_assign // MOE_ROWS + N_EXPERTS
    n_slots = n_blocks * MOE_ROWS
    blocks_per_expert = (counts + MOE_ROWS - 1) // MOE_ROWS
    block_end = jnp.cumsum(blocks_per_expert)
    row_start = (block_end - blocks_per_expert) * MOE_ROWS
    experts = jnp.arange(N_EXPERTS, dtype=jnp.int32)
    row_start_of = jnp.sum(jnp.where(idx[..., None] == experts, row_start, 0), axis=-1)
    dest = (row_start_of + rank).reshape(-1)
    slot = jnp.arange(n_slots, dtype=jnp.int32)
    n_pad = n_slots - n_assign
    pad_counts = jnp.concatenate([blocks_per_expert * MOE_ROWS - counts,
                                  n_slots - block_end[-1:] * MOE_ROWS])
    pad_first = jnp.concatenate([row_start + counts, block_end[-1:] * MOE_ROWS])
    pad_end = jnp.cumsum(pad_counts)
    j = jnp.arange(n_pad, dtype=jnp.int32)
    pad_begin = pad_end - pad_counts
    in_owner = (j[:, None] >= pad_begin[None, :]) & (j[:, None] < pad_end[None, :])
    pad_slot = jnp.sum(jnp.where(in_owner, (pad_first - pad_begin)[None, :] + j[:, None], 0),
                       axis=1)
    spare = n_assign + ((pad_slot // MOE_ROWS) % 2) * MOE_ROWS + pad_slot % MOE_ROWS
    flat = jnp.arange(n_assign, dtype=jnp.int32)
    _, slot_flat = lax.sort_key_val(jnp.concatenate([dest, pad_slot]),
                                    jnp.concatenate([flat, spare]))
    slot_flat = slot_flat.reshape(n_blocks, MOE_ROWS)
    rows = slot_flat * PACK_SUB
    pad = jnp.broadcast_to((n_assign + MOE_ROWS + slot[:MOE_ROWS]) * PACK_SUB, (2, MOE_ROWS))
    plan = jnp.concatenate([jnp.concatenate([rows, pad], axis=0),
                            jnp.concatenate([pad, rows], axis=0)], axis=1)
    blocks = jnp.arange(n_blocks, dtype=jnp.int32)
    block_expert = jnp.minimum(
        jnp.sum((block_end[None, :] <= blocks[:, None]).astype(jnp.int32), axis=1), N_EXPERTS - 1)
    n_used = block_end[-1:].astype(jnp.int32)
    later_used = (experts[None, :] > experts[:, None]) & (blocks_per_expert[None, :] > 0)
    following = jnp.min(jnp.where(later_used, experts[None, :], N_EXPERTS), axis=1)
    following = jnp.where(following == N_EXPERTS, -1, following)
    next_expert = jnp.sum(jnp.where(block_expert[:, None] == experts[None, :], following, 0),
                          axis=1).astype(jnp.int32)
    return block_expert, next_expert, n_used, plan, n_assign + 2 * MOE_ROWS


def _moe(x, x_packed, layer, w_router, router_bias, w_gate, w_up, w_down, ws_gate, ws_up, ws_down,
         g, beta):
    n_tok = x.shape[0]
    idx, wts, rank, cnt = _router(x, w_router, router_bias, tm=512)
    counts = cnt[:, 0].astype(jnp.int32)
    block_expert, next_expert, n_used, plan, n_out_rows = _dispatch_plan(idx, rank, counts, n_tok)
    y = _experts(x_packed, layer, w_gate, w_up, w_down, block_expert, next_expert, n_used, plan,
                 n_out_rows=n_out_rows)
    return _combine(x, ws_gate.astype(BF16), ws_up.astype(BF16), ws_down.astype(BF16), wts.T, y,
                    g, beta, tm=256)


def kernel(x, a_w_in, a_b_in, a_ln_g, a_ln_b, a_w_s, a_b_s, a_w_out, b_w_qkv, b_b_qkv, b_sinks,
           b_w_o, b_b_o, moe_w_router, moe_router_bias, moe_w_gate, moe_w_up, moe_w_down,
           moe_ws_gate, moe_ws_up, moe_ws_down, norm_g, norm_b):
    bsz, seq_len, d = x.shape
    h = x.reshape(bsz * seq_len, d)
    for i in range(DEPTH):
        j = i // 2
        if i % 2 == 0:
            z = _linear(h, a_w_in[j].astype(BF16), a_b_in[j], act="gelu", tm=1024, tn=1024,
                        out_dtype=BF16)
            causal = jnp.tril(jnp.ones((A_CHUNK, A_CHUNK), dtype=bool))
            w_mix = jnp.where(causal[None], a_w_s[j], 0.0).astype(BF16)
            gated = _gmlp_gate(z, a_ln_g[j], a_ln_b[j], w_mix, a_b_s[j].T, tm=256)
            h, h_packed = _linear_res_ln(gated, a_w_out[j].astype(BF16), jnp.zeros((d,), F32), h,
                                         norm_g[i, 0], norm_b[i, 0], tm=512)
        else:
            q_scale = jnp.where(jnp.arange(QKV_DIM) < Q_DIM, HEAD_DIM ** -0.5, 1.0).astype(F32)
            qkv = _linear(h, (b_w_qkv[j] * q_scale).astype(BF16), b_b_qkv[j] * q_scale, act=None,
                          tm=1024, tn=1280, out_dtype=BF16)
            o = _attention(qkv, b_sinks[j], seq_len=seq_len)
            h, h_packed = _linear_res_ln(o, b_w_o[j].astype(BF16), b_b_o[j], h,
                                         norm_g[i, 0], norm_b[i, 0], tm=512)
        h = _moe(h, h_packed, i, moe_w_router[i], moe_router_bias[i], moe_w_gate, moe_w_up, moe_w_down,
                 moe_ws_gate[i], moe_ws_up[i], moe_ws_down[i], norm_g[i, 1], norm_b[i, 1])
    return h.reshape(bsz, seq_len, d)
```

```python
import functools

import jax
import jax.numpy as jnp
from jax import lax
from jax.experimental import pallas as pl
from jax.experimental.pallas import tpu as pltpu

F32 = jnp.float32
BF16 = jnp.bfloat16

D_MODEL = 2048
DEPTH = 2
A_CHUNK = 128
A_HALF = D_MODEL
A_GROUPS = A_HALF // 128
HEAD_DIM = 64
N_Q_HEADS = D_MODEL // HEAD_DIM
N_KV_HEADS = N_Q_HEADS // 8
Q_PER_KV = N_Q_HEADS // N_KV_HEADS
Q_DIM = N_Q_HEADS * HEAD_DIM
KV_DIM = N_KV_HEADS * HEAD_DIM
QKV_DIM = Q_DIM + 2 * KV_DIM
WINDOW = 128
ATTN_BLOCK = 128
N_EXPERTS = 64
TOP_K = 8
N_GROUPS = 8
GROUP_SIZE = N_EXPERTS // N_GROUPS
TOPK_GROUPS = 4
EXPERT_FF = D_MODEL // 4
ROUTED_SCALE = 2.5
DN_ALPHA = (2 * DEPTH) ** 0.25
LN_EPS = 1e-5

MXU_ROWS = 256
MOE_ROWS = 2 * MXU_ROWS
VMEM_LIMIT = 56 * 1024 * 1024


def _params(*sem):
    return pltpu.CompilerParams(dimension_semantics=sem, vmem_limit_bytes=VMEM_LIMIT)


def _layer_norm(y, g, b):
    mu = jnp.mean(y, axis=-1, keepdims=True)
    d = y - mu
    var = jnp.mean(d * d, axis=-1, keepdims=True)
    return d * lax.rsqrt(var + LN_EPS) * g + b


def _gelu(x):
    return 0.5 * x * (1.0 + lax.erf(x * (2.0 ** -0.5)))


def _silu(x):
    return x * jax.nn.sigmoid(x)


LANES = 128
PACK_SUB = D_MODEL // (2 * LANES)
HIGH_HALF = -65536


def _store_packed_rows(p_ref, rows_f32, first_row=0):
    n = rows_f32.shape[0]
    base = first_row * PACK_SUB
    bits = lax.bitcast_convert_type(rows_f32.astype(BF16).astype(F32), jnp.int32)
    for s in range(PACK_SUB):
        hi = bits[:, s * LANES:(s + 1) * LANES] & HIGH_HALF
        lo = lax.shift_right_logical(
            bits[:, D_MODEL // 2 + s * LANES:D_MODEL // 2 + (s + 1) * LANES], 16)
        p_ref[pl.ds(base + s, n, stride=PACK_SUB), :] = hi | lo


def _load_packed_rows(p_ref, n):
    his, los = [], []
    for s in range(PACK_SUB):
        w = p_ref[pl.ds(s, n, stride=PACK_SUB), :]
        his.append(lax.bitcast_convert_type(w & HIGH_HALF, F32))
        los.append(lax.bitcast_convert_type(lax.shift_left(w, 16), F32))
    return his + los


def _linear_kernel(x_ref, w_ref, b_ref, o_ref, *, act):
    acc = jnp.dot(x_ref[...].astype(BF16), w_ref[...], preferred_element_type=F32)
    acc = acc + b_ref[...]
    if act == "gelu":
        acc = _gelu(acc)
    o_ref[...] = acc.astype(o_ref.dtype)


def _linear(x, w, b, *, act, tm, tn, out_dtype):
    m, k = x.shape
    n = w.shape[1]
    return pl.pallas_call(
        functools.partial(_linear_kernel, act=act),
        grid=(m // tm, n // tn),
        in_specs=[
            pl.BlockSpec((tm, k), lambda i, j: (i, 0)),
            pl.BlockSpec((k, tn), lambda i, j: (0, j)),
            pl.BlockSpec((1, tn), lambda i, j: (0, j)),
        ],
        out_specs=pl.BlockSpec((tm, tn), lambda i, j: (i, j)),
        out_shape=jax.ShapeDtypeStruct((m, n), out_dtype),
        name="linear_" + str(act),
        compiler_params=_params("parallel", "arbitrary"),
    )(x, w, b.reshape(1, n))


def _linear_res_ln_kernel(x_ref, w_ref, b_ref, res_ref, g_ref, beta_ref, o_ref, p_ref):
    acc = jnp.dot(x_ref[...], w_ref[...], preferred_element_type=F32)
    y = DN_ALPHA * res_ref[...] + (acc + b_ref[...])
    out = _layer_norm(y, g_ref[...], beta_ref[...])
    o_ref[...] = out
    _store_packed_rows(p_ref, out)


def _linear_res_ln(x, w, b, res, g, beta, *, tm):
    m, k = x.shape
    n = w.shape[1]
    row = lambda i: (i, 0)
    fixed = lambda i: (0, 0)
    return pl.pallas_call(
        _linear_res_ln_kernel,
        grid=(m // tm,),
        in_specs=[
            pl.BlockSpec((tm, k), row),
            pl.BlockSpec((k, n), fixed),
            pl.BlockSpec((1, n), fixed),
            pl.BlockSpec((tm, n), row),
            pl.BlockSpec((1, n), fixed),
            pl.BlockSpec((1, n), fixed),
        ],
        out_specs=[pl.BlockSpec((tm, n), row), pl.BlockSpec((tm * PACK_SUB, LANES), row)],
        out_shape=[jax.ShapeDtypeStruct((m, n), F32),
                   jax.ShapeDtypeStruct((m * PACK_SUB, LANES), jnp.int32)],
        name="linear_res_ln",
        compiler_params=_params("parallel"),
    )(x, w, b.reshape(1, n), res, g.reshape(1, n), beta.reshape(1, n))


def _gmlp_gate_kernel(z_ref, lng_ref, lnb_ref, wmix_ref, bst_ref, o_ref, *, tm):
    v = z_ref[:, A_HALF:].astype(F32)
    vn = _layer_norm(v, lng_ref[...], lnb_ref[...]).astype(BF16)
    for c in range(tm // A_CHUNK):
        rows = slice(c * A_CHUNK, (c + 1) * A_CHUNK)
        for g in range(A_GROUPS):
            cols = slice(g * 128, (g + 1) * 128)
            mixed = jnp.dot(wmix_ref[g], vn[rows, cols], preferred_element_type=F32)
            mixed = mixed + bst_ref[:, g:g + 1]
            u = z_ref[rows, cols].astype(F32)
            o_ref[rows, cols] = (u * mixed).astype(o_ref.dtype)


def _gmlp_gate(z, ln_g, ln_b, w_mix, b_s_t, *, tm):
    m = z.shape[0]
    return pl.pallas_call(
        functools.partial(_gmlp_gate_kernel, tm=tm),
        grid=(m // tm,),
        in_specs=[
            pl.BlockSpec((tm, 2 * A_HALF), lambda i: (i, 0)),
            pl.BlockSpec((1, A_HALF), lambda i: (0, 0)),
            pl.BlockSpec((1, A_HALF), lambda i: (0, 0)),
            pl.BlockSpec((A_GROUPS, A_CHUNK, A_CHUNK), lambda i: (0, 0, 0)),
            pl.BlockSpec((A_CHUNK, A_GROUPS), lambda i: (0, 0)),
        ],
        out_specs=pl.BlockSpec((tm, A_HALF), lambda i: (i, 0)),
        out_shape=jax.ShapeDtypeStruct((m, A_HALF), BF16),
        name="gmlp_gate",
        compiler_params=_params("parallel"),
    )(z, ln_g.reshape(1, A_HALF), ln_b.reshape(1, A_HALF), w_mix, b_s_t)


def _attn_kernel(sinks_ref, q_ref, kp_ref, kc_ref, vp_ref, vc_ref, bias_ref, o_ref, *,
                 blocks_per_seq):
    n = pl.program_id(0) % blocks_per_seq
    first = (n == 0).astype(jnp.int32)
    qi = lax.broadcasted_iota(jnp.int32, (ATTN_BLOCK, ATTN_BLOCK), 0)
    ci = lax.broadcasted_iota(jnp.int32, (ATTN_BLOCK, ATTN_BLOCK), 1)
    upper = ci > qi
    low_half = lax.broadcasted_iota(jnp.int32, (ATTN_BLOCK, 2 * HEAD_DIM), 1) < HEAD_DIM
    ones = jnp.ones((2 * ATTN_BLOCK, HEAD_DIM), BF16)
    for g in range(N_KV_HEADS):
        kv_cols = slice(g * HEAD_DIM, (g + 1) * HEAD_DIM)
        kk = jnp.concatenate([kp_ref[:, kv_cols], kc_ref[:, kv_cols]], axis=0)
        vv = jnp.concatenate([vp_ref[:, kv_cols], vc_ref[:, kv_cols]], axis=0)
        vv_ones = jnp.concatenate([vv, ones, ones, vv], axis=1)
        heads = [g * Q_PER_KV + j for j in range(Q_PER_KV)]
        qg = jnp.concatenate([q_ref[:, h * HEAD_DIM:(h + 1) * HEAD_DIM] for h in heads], axis=0)
        s = lax.dot_general(qg, kk, (((1,), (1,)), ((), ())), preferred_element_type=F32)
        es, maxes = [], []
        for j, h in enumerate(heads):
            sj = s[j * ATTN_BLOCK:(j + 1) * ATTN_BLOCK]
            sm = jnp.where(upper, sj[:, :ATTN_BLOCK], sj[:, ATTN_BLOCK:]) + bias_ref[first, h]
            mx = jnp.maximum(jnp.max(sm, axis=-1, keepdims=True), sinks_ref[h])
            e = jnp.exp(sm - mx)
            es.append(jnp.concatenate([jnp.where(upper, e, 0.0), jnp.where(upper, 0.0, e)],
                                      axis=1).astype(BF16))
            maxes.append(mx)
        o = jnp.dot(jnp.concatenate(es, axis=0), vv_ones, preferred_element_type=F32)
        for j in range(0, Q_PER_KV, 2):
            even = o[j * ATTN_BLOCK:(j + 1) * ATTN_BLOCK]
            odd = o[(j + 1) * ATTN_BLOCK:(j + 2) * ATTN_BLOCK]
            num = jnp.where(low_half, even[:, :2 * HEAD_DIM], odd[:, 2 * HEAD_DIM:])
            total = jnp.where(low_half, even[:, 2 * HEAD_DIM:], odd[:, :2 * HEAD_DIM])
            sink_e = jnp.where(low_half, jnp.exp(sinks_ref[heads[j]] - maxes[j]),
                               jnp.exp(sinks_ref[heads[j + 1]] - maxes[j + 1]))
            cols = slice(heads[j] * HEAD_DIM, (heads[j] + 2) * HEAD_DIM)
            o_ref[:, cols] = (num * (1.0 / (total + sink_e))).astype(o_ref.dtype)


def _attention_bias():
    qi = jnp.arange(ATTN_BLOCK, dtype=jnp.int32)[:, None]
    kc = jnp.arange(ATTN_BLOCK, dtype=jnp.int32)[None, :]
    upper = kc > qi
    dist = (qi - kc + jnp.where(upper, ATTN_BLOCK, 0)).astype(F32)
    head = jnp.arange(1, N_Q_HEADS + 1, dtype=F32)
    slopes = jnp.exp2(-8.0 * head / N_Q_HEADS)
    alibi = -slopes[:, None, None] * dist[None]
    return jnp.stack([alibi, jnp.where(upper[None], -jnp.inf, alibi)])


def _attention(qkv, sinks, *, seq_len):
    assert WINDOW == ATTN_BLOCK
    t = qkv.shape[0]
    blocks_per_seq = seq_len // ATTN_BLOCK
    kcol = Q_DIM // KV_DIM
    vcol = kcol + 1
    prev = lambda i: jnp.maximum(i - 1, 0)
    grid_spec = pltpu.PrefetchScalarGridSpec(
        num_scalar_prefetch=1,
        grid=(t // ATTN_BLOCK,),
        in_specs=[
            pl.BlockSpec((ATTN_BLOCK, Q_DIM), lambda i, s: (i, 0)),
            pl.BlockSpec((ATTN_BLOCK, KV_DIM), lambda i, s: (prev(i), kcol)),
            pl.BlockSpec((ATTN_BLOCK, KV_DIM), lambda i, s: (i, kcol)),
            pl.BlockSpec((ATTN_BLOCK, KV_DIM), lambda i, s: (prev(i), vcol)),
            pl.BlockSpec((ATTN_BLOCK, KV_DIM), lambda i, s: (i, vcol)),
            pl.BlockSpec((2, N_Q_HEADS, ATTN_BLOCK, ATTN_BLOCK), lambda i, s: (0, 0, 0, 0)),
        ],
        out_specs=pl.BlockSpec((ATTN_BLOCK, Q_DIM), lambda i, s: (i, 0)),
    )
    return pl.pallas_call(
        functools.partial(_attn_kernel, blocks_per_seq=blocks_per_seq),
        grid_spec=grid_spec,
        out_shape=jax.ShapeDtypeStruct((t, Q_DIM), BF16),
        name="swa_attention",
        compiler_params=_params("parallel"),
    )(sinks, qkv, qkv, qkv, qkv, qkv, _attention_bias())


def _router_kernel(x_ref, wr_ref, rb_ref, idx_ref, wts_ref, cnt_ref, carry_ref, *, tm):
    @pl.when(pl.program_id(0) == 0)
    def _():
        carry_ref[...] = jnp.zeros_like(carry_ref)

    x = x_ref[...]
    x_hi = x.astype(BF16)
    x_lo = (x - x_hi.astype(F32)).astype(BF16)
    w_split = wr_ref[...]
    by_hi = jnp.dot(x_hi, w_split, preferred_element_type=F32)
    by_lo = jnp.dot(x_lo, w_split[:, :LANES], preferred_element_type=F32)
    logits = by_hi[:, :LANES] + (by_hi[:, LANES:] + by_lo)
    sc = jax.nn.sigmoid(logits.T[:N_EXPERTS])
    sel = sc + rb_ref[...]
    neg = -jnp.inf
    iota_g = lax.broadcasted_iota(jnp.int32, (GROUP_SIZE, tm), 0).astype(F32)

    def first_argmax(v, iota, size):
        m = jnp.max(v, axis=0, keepdims=True)
        return jnp.min(jnp.where(v == m, iota, float(size)), axis=0, keepdims=True)

    group_rows = []
    for g in range(N_GROUPS):
        v = sel[g * GROUP_SIZE:(g + 1) * GROUP_SIZE]
        m1 = jnp.max(v, axis=0, keepdims=True)
        i1 = first_argmax(v, iota_g, GROUP_SIZE)
        m2 = jnp.max(jnp.where(iota_g == i1, neg, v), axis=0, keepdims=True)
        group_rows.append(m1 + m2)
    cur = jnp.concatenate(group_rows, axis=0)
    iota_n = lax.broadcasted_iota(jnp.int32, (N_GROUPS, tm), 0).astype(F32)
    gsel = jnp.zeros((N_GROUPS, tm), F32)
    for _ in range(TOPK_GROUPS):
        hit = iota_n == first_argmax(cur, iota_n, N_GROUPS)
        gsel = jnp.where(hit, 1.0, gsel)
        cur = jnp.where(hit, neg, cur)
    cur = jnp.concatenate(
        [jnp.where(gsel[g:g + 1] > 0.5, sel[g * GROUP_SIZE:(g + 1) * GROUP_SIZE], neg)
         for g in range(N_GROUPS)], axis=0)

    iota_e = lax.broadcasted_iota(jnp.int32, (N_EXPERTS, tm), 0).astype(F32)
    member = jnp.zeros((N_EXPERTS, tm), F32)
    idx_rows, w_rows = [], []
    for _ in range(TOP_K):
        ii = first_argmax(cur, iota_e, N_EXPERTS)
        hit = iota_e == ii
        idx_rows.append(ii)
        w_rows.append(jnp.sum(jnp.where(hit, sc, 0.0), axis=0, keepdims=True))
        member = jnp.where(hit, 1.0, member)
        cur = jnp.where(hit, neg, cur)
    w = jnp.concatenate(w_rows, axis=0)
    w = w / (jnp.sum(w, axis=0, keepdims=True) + 1e-20) * ROUTED_SCALE

    idx_ref[...] = jnp.concatenate(idx_rows, axis=0).astype(jnp.int32)
    wts_ref[...] = w
    carry_ref[...] = carry_ref[...] + jnp.sum(member, axis=1, keepdims=True)
    cnt_ref[...] = jnp.broadcast_to(carry_ref[...], cnt_ref.shape)


def _router(x, w_router, router_bias, *, tm):
    t, d = x.shape
    tok = lambda i: (0, i)
    w_padded = jnp.pad(w_router, ((0, 0), (0, LANES - N_EXPERTS)))
    w_hi = w_padded.astype(BF16)
    w_lo = (w_padded - w_hi.astype(F32)).astype(BF16)
    w_split = jnp.concatenate([w_hi, w_lo], axis=1)
    return pl.pallas_call(
        functools.partial(_router_kernel, tm=tm),
        grid=(t // tm,),
        in_specs=[
            pl.BlockSpec((tm, d), lambda i: (i, 0)),
            pl.BlockSpec((d, 2 * LANES), lambda i: (0, 0)),
            pl.BlockSpec((N_EXPERTS, 1), lambda i: (0, 0)),
        ],
        out_specs=[
            pl.BlockSpec((TOP_K, tm), tok),
            pl.BlockSpec((TOP_K, tm), tok),
            pl.BlockSpec((N_EXPERTS, LANES), lambda i: (0, 0)),
        ],
        out_shape=[
            jax.ShapeDtypeStruct((TOP_K, t), jnp.int32),
            jax.ShapeDtypeStruct((TOP_K, t), F32),
            jax.ShapeDtypeStruct((N_EXPERTS, LANES), F32),
        ],
        scratch_shapes=[pltpu.VMEM((N_EXPERTS, 1), F32)],
        name="router",
        compiler_params=_params("arbitrary"),
    )(x, w_split, router_bias.reshape(N_EXPERTS, 1))


def _experts_kernel(be_ref, ne_ref, nu_ref, plan_hbm, x_hbm, wg_hbm, wu_hbm, wd_hbm, y_hbm,
                    plan_smem, xbuf0, xbuf1, ybuf0, ybuf1, wg_f32, wu_f32, wd_f32,
                    wg_bf, wu_bf, wd_bf, psem, gsem, ssem, wsem, *, tok_mask, layer):
    b = pl.program_id(0)
    n_used = nu_ref[0]
    rows = MOE_ROWS
    xbuf = (xbuf0, xbuf1)
    ybuf = (ybuf0, ybuf1)

    def weight_copies(expert):
        return [pltpu.make_async_copy(w_hbm.at[layer, expert], stage, wsem.at[i])
                for i, (w_hbm, stage) in enumerate(
                    ((wg_hbm, wg_f32), (wu_hbm, wu_f32), (wd_hbm, wd_f32)))]

    def plan_copy(row, slot):
        return pltpu.make_async_copy(plan_hbm.at[row], plan_smem.at[slot], psem.at[slot])

    def start_gathers(pslot, slot):
        for r in range(rows):
            src = pl.multiple_of(plan_smem[pslot, r] & tok_mask, PACK_SUB)
            pltpu.make_async_copy(x_hbm.at[pl.ds(src, PACK_SUB)],
                                  xbuf[slot].at[pl.ds(r * PACK_SUB, PACK_SUB)],
                                  gsem.at[slot]).start()

    def start_scatters(pslot, slot):
        for r in range(rows):
            dst = pl.multiple_of(plan_smem[pslot, rows + r], PACK_SUB)
            pltpu.make_async_copy(ybuf[slot].at[pl.ds(r * PACK_SUB, PACK_SUB)],
                                  y_hbm.at[pl.ds(dst, PACK_SUB)],
                                  ssem.at[slot]).start(priority=1)

    def wait_rows(buf, sem):
        pltpu.make_async_copy(buf, buf, sem).wait()

    @pl.when(b == 0)
    def _():
        ybuf0[...] = jnp.zeros_like(ybuf0)
        ybuf1[...] = jnp.zeros_like(ybuf1)
        spare0 = y_hbm.shape[0] - 2 * rows * PACK_SUB
        pltpu.make_async_copy(ybuf0, y_hbm.at[pl.ds(spare0, rows * PACK_SUB)], ssem.at[0]).start()
        plan_copy(0, 0).start()
        for copy in weight_copies(be_ref[0]):
            copy.start()
        plan_copy(0, 0).wait()
        start_gathers(0, 0)
        plan_copy(1, 1).start()

    def step(slot):
        other = 1 - slot
        changed = jnp.logical_or(b == 0, be_ref[b] != be_ref[jnp.maximum(b - 1, 0)])

        @pl.when(changed)
        def _():
            for copy in weight_copies(be_ref[b]):
                copy.wait()
            wg_bf[...] = wg_f32[...].astype(BF16)
            wu_bf[...] = wu_f32[...].astype(BF16)
            wd_bf[...] = wd_f32[...].astype(BF16)
            upcoming = ne_ref[b]

            @pl.when(upcoming >= 0)
            def _():
                for copy in weight_copies(upcoming):
                    copy.start()

        wait_rows(xbuf[slot], gsem.at[slot])
        wait_rows(ybuf[slot], ssem.at[slot])
        plan_copy(b + 1, other).wait()
        start_gathers(other, other)
        start_scatters(other, other)
        plan_copy(b + 2, slot).start()

        x = jnp.concatenate([p.astype(BF16) for p in _load_packed_rows(xbuf[slot], rows)], axis=1)
        gate = jnp.dot(x, wg_bf[...], preferred_element_type=F32)
        up = jnp.dot(x, wu_bf[...], preferred_element_type=F32)
        h = (_silu(gate) * up).astype(BF16)
        _store_packed_rows(ybuf[slot], jnp.dot(h, wd_bf[...], preferred_element_type=F32))

        @pl.when(b == n_used - 1)
        def _():
            plan_copy(b + 2, slot).wait()
            start_scatters(slot, slot)
            wait_rows(xbuf[other], gsem.at[other])
            wait_rows(ybuf[other], ssem.at[other])
            wait_rows(ybuf[slot], ssem.at[slot])

    for parity in range(2):
        pl.when(jnp.logical_and(b < n_used, b % 2 == parity))(functools.partial(step, parity))


def _experts(x_packed, layer, w_gate, w_up, w_down, block_expert, next_expert, n_used, plan, *,
             n_out_rows):
    t = x_packed.shape[0] // PACK_SUB
    d = D_MODEL
    assert t & (t - 1) == 0, "token id is taken as the low bits of the flat (choice, token) index"
    n_blocks = plan.shape[0] - 2
    row_buf = pltpu.VMEM((MOE_ROWS * PACK_SUB, LANES), jnp.int32)
    grid_spec = pltpu.PrefetchScalarGridSpec(
        num_scalar_prefetch=3,
        grid=(n_blocks,),
        in_specs=[pl.BlockSpec(memory_space=pl.ANY)] * 5,
        out_specs=pl.BlockSpec(memory_space=pl.ANY),
        scratch_shapes=[
            pltpu.SMEM((2, 2 * MOE_ROWS), jnp.int32),
            row_buf, row_buf, row_buf, row_buf,
            pltpu.VMEM((d, EXPERT_FF), F32),
            pltpu.VMEM((d, EXPERT_FF), F32),
            pltpu.VMEM((EXPERT_FF, d), F32),
            pltpu.VMEM((d, EXPERT_FF), BF16),
            pltpu.VMEM((d, EXPERT_FF), BF16),
            pltpu.VMEM((EXPERT_FF, d), BF16),
            pltpu.SemaphoreType.DMA((2,)),
            pltpu.SemaphoreType.DMA((2,)),
            pltpu.SemaphoreType.DMA((2,)),
            pltpu.SemaphoreType.DMA((3,)),
        ],
    )
    return pl.pallas_call(
        functools.partial(_experts_kernel, tok_mask=(t - 1) * PACK_SUB, layer=layer),
        grid_spec=grid_spec,
        out_shape=jax.ShapeDtypeStruct((n_out_rows * PACK_SUB, LANES), jnp.int32),
        name="routed_experts",
        compiler_params=_params("arbitrary"),
    )(block_expert, next_expert, n_used, plan, x_packed, w_gate, w_up, w_down)


def _combine_kernel(x_ref, wg_ref, wu_ref, wd_ref, w_ref, *refs, tm):
    y_refs = refs[:TOP_K]
    g_ref, beta_ref, o_ref = refs[TOP_K:]
    x = x_ref[...]
    xb = x.astype(BF16)
    gate = jnp.dot(xb, wg_ref[...], preferred_element_type=F32)
    up = jnp.dot(xb, wu_ref[...], preferred_element_type=F32)
    h = (_silu(gate) * up).astype(BF16)
    s = DN_ALPHA * x + jnp.dot(h, wd_ref[...], preferred_element_type=F32)
    pieces = [s[:, c * LANES:(c + 1) * LANES] for c in range(2 * PACK_SUB)]
    for k in range(TOP_K):
        wk = w_ref[:, k:k + 1]
        yk = _load_packed_rows(y_refs[k], tm)
        pieces = [p + wk * y for p, y in zip(pieces, yk)]
    o_ref[...] = _layer_norm(jnp.concatenate(pieces, axis=1), g_ref[...], beta_ref[...])


def _combine(x, ws_gate, ws_up, ws_down, wts_t, y, g, beta, *, tm):
    t, d = x.shape
    f = ws_gate.shape[1]
    blocks = t // tm
    fixed = lambda i: (0, 0)
    y_specs = [pl.BlockSpec((tm * PACK_SUB, LANES),
                            functools.partial(lambda i, k: (k * blocks + i, 0), k=k))
               for k in range(TOP_K)]
    return pl.pallas_call(
        functools.partial(_combine_kernel, tm=tm),
        grid=(blocks,),
        in_specs=[
            pl.BlockSpec((tm, d), lambda i: (i, 0)),
            pl.BlockSpec((d, f), fixed),
            pl.BlockSpec((d, f), fixed),
            pl.BlockSpec((f, d), fixed),
            pl.BlockSpec((tm, TOP_K), lambda i: (i, 0)),
            *y_specs,
            pl.BlockSpec((1, d), fixed),
            pl.BlockSpec((1, d), fixed),
        ],
        out_specs=pl.BlockSpec((tm, d), lambda i: (i, 0)),
        out_shape=jax.ShapeDtypeStruct((t, d), F32),
        name="moe_combine",
        compiler_params=_params("parallel"),
    )(x, ws_gate, ws_up, ws_down, wts_t, *([y] * TOP_K), g.reshape(1, d), beta.reshape(1, d))


def _dispatch_plan(idx, counts, n_tok):
    n_assign = n_tok * TOP_K
    n_blocks = n_assign // MOE_ROWS + N_EXPERTS
    n_slots = n_blocks * MOE_ROWS
    blocks_per_expert = (counts + MOE_ROWS - 1) // MOE_ROWS
    block_end = jnp.cumsum(blocks_per_expert)
    experts = jnp.arange(N_EXPERTS, dtype=jnp.int32)
    slot = jnp.arange(n_slots, dtype=jnp.int32)
    payload_bits = (n_assign - 1).bit_length()
    n_pad = n_slots - n_assign
    assert n_pad <= 1 << payload_bits and (N_EXPERTS + 1) << (payload_bits + 1) < 1 << 31
    k_of = lax.broadcasted_iota(jnp.int32, idx.shape, 0)
    t_of = lax.broadcasted_iota(jnp.int32, idx.shape, 1)
    real_keys = (idx << (payload_bits + 1)) | (t_of * TOP_K + k_of)
    pad_counts = jnp.concatenate([blocks_per_expert * MOE_ROWS - counts,
                                  n_slots - block_end[-1:] * MOE_ROWS])
    pad_end = jnp.cumsum(pad_counts)
    j = jnp.arange(n_pad, dtype=jnp.int32)
    pad_owner = jnp.sum((pad_end[None, :] <= j[:, None]).astype(jnp.int32), axis=1)
    pad_keys = (pad_owner << (payload_bits + 1)) | (1 << payload_bits) | j
    keys = lax.sort(jnp.concatenate([real_keys.reshape(-1), pad_keys]))
    payload = keys & ((1 << payload_bits) - 1)
    flat = (payload % TOP_K) * n_tok + payload // TOP_K
    spare = n_assign + ((slot // MOE_ROWS) % 2) * MOE_ROWS + slot % MOE_ROWS
    is_pad = (keys >> payload_bits) & 1
    slot_flat = jnp.where(is_pad == 1, spare, flat).reshape(n_blocks, MOE_ROWS)
    rows = slot_flat * PACK_SUB
    pad = jnp.broadcast_to((n_assign + MOE_ROWS + slot[:MOE_ROWS]) * PACK_SUB, (2, MOE_ROWS))
    plan = jnp.concatenate([jnp.concatenate([rows, pad], axis=0),
                            jnp.concatenate([pad, rows], axis=0)], axis=1)
    blocks = jnp.arange(n_blocks, dtype=jnp.int32)
    block_expert = jnp.minimum(
        jnp.sum((block_end[None, :] <= blocks[:, None]).astype(jnp.int32), axis=1), N_EXPERTS - 1)
    n_used = block_end[-1:].astype(jnp.int32)
    later_used = (experts[None, :] > experts[:, None]) & (blocks_per_expert[None, :] > 0)
    following = jnp.min(jnp.where(later_used, experts[None, :], N_EXPERTS), axis=1)
    following = jnp.where(following == N_EXPERTS, -1, following)
    next_expert = jnp.sum(jnp.where(block_expert[:, None] == experts[None, :], following, 0),
                          axis=1).astype(jnp.int32)
    return block_expert, next_expert, n_used, plan, n_assign + 2 * MOE_ROWS


def _moe(x, x_packed, layer, w_router, router_bias, w_gate, w_up, w_down, ws_gate, ws_up, ws_down,
         g, beta):
    n_tok = x.shape[0]
    idx, wts, cnt = _router(x, w_router, router_bias, tm=512)
    counts = cnt[:, 0].astype(jnp.int32)
    block_expert, next_expert, n_used, plan, n_out_rows = _dispatch_plan(idx, counts, n_tok)
    y = _experts(x_packed, layer, w_gate, w_up, w_down, block_expert, next_expert, n_used, plan,
                 n_out_rows=n_out_rows)
    return _combine(x, ws_gate.astype(BF16), ws_up.astype(BF16), ws_down.astype(BF16), wts.T, y,
                    g, beta, tm=256)


def kernel(x, a_w_in, a_b_in, a_ln_g, a_ln_b, a_w_s, a_b_s, a_w_out, b_w_qkv, b_b_qkv, b_sinks,
           b_w_o, b_b_o, moe_w_router, moe_router_bias, moe_w_gate, moe_w_up, moe_w_down,
           moe_ws_gate, moe_ws_up, moe_ws_down, norm_g, norm_b):
    bsz, seq_len, d = x.shape
    h = x.reshape(bsz * seq_len, d)
    for i in range(DEPTH):
        j = i // 2
        if i % 2 == 0:
            z = _linear(h, a_w_in[j].astype(BF16), a_b_in[j], act="gelu", tm=1024, tn=1024,
                        out_dtype=BF16)
            causal = jnp.tril(jnp.ones((A_CHUNK, A_CHUNK), dtype=bool))
            w_mix = jnp.where(causal[None], a_w_s[j], 0.0).astype(BF16)
            gated = _gmlp_gate(z, a_ln_g[j], a_ln_b[j], w_mix, a_b_s[j].T, tm=256)
            h, h_packed = _linear_res_ln(gated, a_w_out[j].astype(BF16), jnp.zeros((d,), F32), h,
                                         norm_g[i, 0], norm_b[i, 0], tm=512)
        else:
            q_scale = jnp.where(jnp.arange(QKV_DIM) < Q_DIM, HEAD_DIM ** -0.5, 1.0).astype(F32)
            qkv = _linear(h, (b_w_qkv[j] * q_scale).astype(BF16), b_b_qkv[j] * q_scale, act=None,
                          tm=1024, tn=1280, out_dtype=BF16)
            o = _attention(qkv, b_sinks[j], seq_len=seq_len)
            h, h_packed = _linear_res_ln(o, b_w_o[j].astype(BF16), b_b_o[j], h,
                                         norm_g[i, 0], norm_b[i, 0], tm=512)
        h = _moe(h, h_packed, i, moe_w_router[i], moe_router_bias[i], moe_w_gate, moe_w_up, moe_w_down,
                 moe_ws_gate[i], moe_ws_up[i], moe_ws_down[i], norm_g[i, 1], norm_b[i, 1])
    return h.reshape(bsz, seq_len, d)
```

```python
import functools

import jax
import jax.numpy as jnp
from jax import lax
from jax.experimental import pallas as pl
from jax.experimental.pallas import tpu as pltpu

F32 = jnp.float32
BF16 = jnp.bfloat16

D_MODEL = 2048
DEPTH = 2
A_CHUNK = 128
A_HALF = D_MODEL
A_GROUPS = A_HALF // 128
HEAD_DIM = 64
N_Q_HEADS = D_MODEL // HEAD_DIM
N_KV_HEADS = N_Q_HEADS // 8
Q_PER_KV = N_Q_HEADS // N_KV_HEADS
Q_DIM = N_Q_HEADS * HEAD_DIM
KV_DIM = N_KV_HEADS * HEAD_DIM
QKV_DIM = Q_DIM + 2 * KV_DIM
WINDOW = 128
ATTN_BLOCK = 128
N_EXPERTS = 64
TOP_K = 8
N_GROUPS = 8
GROUP_SIZE = N_EXPERTS // N_GROUPS
TOPK_GROUPS = 4
EXPERT_FF = D_MODEL // 4
ROUTED_SCALE = 2.5
DN_ALPHA = (2 * DEPTH) ** 0.25
LN_EPS = 1e-5

MXU_ROWS = 256
MOE_ROWS = 2 * MXU_ROWS
VMEM_LIMIT = 56 * 1024 * 1024


def _params(*sem):
    return pltpu.CompilerParams(dimension_semantics=sem, vmem_limit_bytes=VMEM_LIMIT)


def _layer_norm(y, g, b):
    mu = jnp.mean(y, axis=-1, keepdims=True)
    d = y - mu
    var = jnp.mean(d * d, axis=-1, keepdims=True)
    return d * lax.rsqrt(var + LN_EPS) * g + b


def _gelu(x):
    return 0.5 * x * (1.0 + lax.erf(x * (2.0 ** -0.5)))


def _silu(x):
    return x * jax.nn.sigmoid(x)


LANES = 128
PACK_SUB = D_MODEL // (2 * LANES)
HIGH_HALF = -65536


def _pack_words(rows_f32):
    bits = lax.bitcast_convert_type(rows_f32.astype(BF16).astype(F32), jnp.int32)
    half = D_MODEL // 2
    return [(bits[:, s * LANES:(s + 1) * LANES] & HIGH_HALF)
            | lax.shift_right_logical(bits[:, half + s * LANES:half + (s + 1) * LANES], 16)
            for s in range(PACK_SUB)]


def _unpack_words(words):
    his = [lax.bitcast_convert_type(w & HIGH_HALF, F32) for w in words]
    los = [lax.bitcast_convert_type(lax.shift_left(w, 16), F32) for w in words]
    return his + los


def _store_packed_rows(p_ref, rows_f32):
    n = rows_f32.shape[0]
    for s, word in enumerate(_pack_words(rows_f32)):
        p_ref[pl.ds(s, n, stride=PACK_SUB), :] = word


def _load_packed_rows(p_ref, n):
    return _unpack_words([p_ref[pl.ds(s, n, stride=PACK_SUB), :] for s in range(PACK_SUB)])


def _linear_kernel(x_ref, w_ref, b_ref, o_ref, *, act):
    acc = jnp.dot(x_ref[...].astype(BF16), w_ref[...], preferred_element_type=F32)
    acc = acc + b_ref[...]
    if act == "gelu":
        acc = _gelu(acc)
    o_ref[...] = acc.astype(o_ref.dtype)


def _linear(x, w, b, *, act, tm, tn, out_dtype):
    m, k = x.shape
    n = w.shape[1]
    return pl.pallas_call(
        functools.partial(_linear_kernel, act=act),
        grid=(m // tm, n // tn),
        in_specs=[
            pl.BlockSpec((tm, k), lambda i, j: (i, 0)),
            pl.BlockSpec((k, tn), lambda i, j: (0, j)),
            pl.BlockSpec((1, tn), lambda i, j: (0, j)),
        ],
        out_specs=pl.BlockSpec((tm, tn), lambda i, j: (i, j)),
        out_shape=jax.ShapeDtypeStruct((m, n), out_dtype),
        name="linear_" + str(act),
        compiler_params=_params("parallel", "arbitrary"),
    )(x, w, b.reshape(1, n))


def _linear_res_ln_kernel(x_ref, w_ref, b_ref, res_ref, g_ref, beta_ref, o_ref, p_ref):
    acc = jnp.dot(x_ref[...], w_ref[...], preferred_element_type=F32)
    y = DN_ALPHA * res_ref[...] + (acc + b_ref[...])
    out = _layer_norm(y, g_ref[...], beta_ref[...])
    o_ref[...] = out
    _store_packed_rows(p_ref, out)


def _linear_res_ln(x, w, b, res, g, beta, *, tm):
    m, k = x.shape
    n = w.shape[1]
    row = lambda i: (i, 0)
    fixed = lambda i: (0, 0)
    return pl.pallas_call(
        _linear_res_ln_kernel,
        grid=(m // tm,),
        in_specs=[
            pl.BlockSpec((tm, k), row),
            pl.BlockSpec((k, n), fixed),
            pl.BlockSpec((1, n), fixed),
            pl.BlockSpec((tm, n), row),
            pl.BlockSpec((1, n), fixed),
            pl.BlockSpec((1, n), fixed),
        ],
        out_specs=[pl.BlockSpec((tm, n), row), pl.BlockSpec((tm * PACK_SUB, LANES), row)],
        out_shape=[jax.ShapeDtypeStruct((m, n), F32),
                   jax.ShapeDtypeStruct((m * PACK_SUB, LANES), jnp.int32)],
        name="linear_res_ln",
        compiler_params=_params("parallel"),
    )(x, w, b.reshape(1, n), res, g.reshape(1, n), beta.reshape(1, n))


def _gmlp_gate_kernel(z_ref, lng_ref, lnb_ref, wmix_ref, bst_ref, o_ref, *, tm):
    v = z_ref[:, A_HALF:].astype(F32)
    vn = _layer_norm(v, lng_ref[...], lnb_ref[...]).astype(BF16)
    for c in range(tm // A_CHUNK):
        rows = slice(c * A_CHUNK, (c + 1) * A_CHUNK)
        for g in range(A_GROUPS):
            cols = slice(g * 128, (g + 1) * 128)
            mixed = jnp.dot(wmix_ref[g], vn[rows, cols], preferred_element_type=F32)
            mixed = mixed + bst_ref[:, g:g + 1]
            u = z_ref[rows, cols].astype(F32)
            o_ref[rows, cols] = (u * mixed).astype(o_ref.dtype)


def _gmlp_gate(z, ln_g, ln_b, w_mix, b_s_t, *, tm):
    m = z.shape[0]
    return pl.pallas_call(
        functools.partial(_gmlp_gate_kernel, tm=tm),
        grid=(m // tm,),
        in_specs=[
            pl.BlockSpec((tm, 2 * A_HALF), lambda i: (i, 0)),
            pl.BlockSpec((1, A_HALF), lambda i: (0, 0)),
            pl.BlockSpec((1, A_HALF), lambda i: (0, 0)),
            pl.BlockSpec((A_GROUPS, A_CHUNK, A_CHUNK), lambda i: (0, 0, 0)),
            pl.BlockSpec((A_CHUNK, A_GROUPS), lambda i: (0, 0)),
        ],
        out_specs=pl.BlockSpec((tm, A_HALF), lambda i: (i, 0)),
        out_shape=jax.ShapeDtypeStruct((m, A_HALF), BF16),
        name="gmlp_gate",
        compiler_params=_params("parallel"),
    )(z, ln_g.reshape(1, A_HALF), ln_b.reshape(1, A_HALF), w_mix, b_s_t)


def _attn_kernel(sinks_ref, q_ref, kp_ref, kc_ref, vp_ref, vc_ref, bias_ref, o_ref, *,
                 blocks_per_seq):
    n = pl.program_id(0) % blocks_per_seq
    first = (n == 0).astype(jnp.int32)
    qi = lax.broadcasted_iota(jnp.int32, (ATTN_BLOCK, ATTN_BLOCK), 0)
    ci = lax.broadcasted_iota(jnp.int32, (ATTN_BLOCK, ATTN_BLOCK), 1)
    upper = ci > qi
    low_half = lax.broadcasted_iota(jnp.int32, (ATTN_BLOCK, 2 * HEAD_DIM), 1) < HEAD_DIM
    ones = jnp.ones((2 * ATTN_BLOCK, HEAD_DIM), BF16)
    for g in range(N_KV_HEADS):
        kv_cols = slice(g * HEAD_DIM, (g + 1) * HEAD_DIM)
        kk = jnp.concatenate([kp_ref[:, kv_cols], kc_ref[:, kv_cols]], axis=0)
        vv = jnp.concatenate([vp_ref[:, kv_cols], vc_ref[:, kv_cols]], axis=0)
        vv_ones = jnp.concatenate([vv, ones, ones, vv], axis=1)
        heads = [g * Q_PER_KV + j for j in range(Q_PER_KV)]
        qg = jnp.concatenate([q_ref[:, h * HEAD_DIM:(h + 1) * HEAD_DIM] for h in heads], axis=0)
        s = lax.dot_general(qg, kk, (((1,), (1,)), ((), ())), preferred_element_type=F32)
        es, maxes = [], []
        for j, h in enumerate(heads):
            sj = s[j * ATTN_BLOCK:(j + 1) * ATTN_BLOCK]
            sm = jnp.where(upper, sj[:, :ATTN_BLOCK], sj[:, ATTN_BLOCK:]) + bias_ref[first, h]
            mx = jnp.maximum(jnp.max(sm, axis=-1, keepdims=True), sinks_ref[h])
            e = jnp.exp(sm - mx)
            es.append(jnp.concatenate([jnp.where(upper, e, 0.0), jnp.where(upper, 0.0, e)],
                                      axis=1).astype(BF16))
            maxes.append(mx)
        o = jnp.dot(jnp.concatenate(es, axis=0), vv_ones, preferred_element_type=F32)
        for j in range(0, Q_PER_KV, 2):
            even = o[j * ATTN_BLOCK:(j + 1) * ATTN_BLOCK]
            odd = o[(j + 1) * ATTN_BLOCK:(j + 2) * ATTN_BLOCK]
            num = jnp.where(low_half, even[:, :2 * HEAD_DIM], odd[:, 2 * HEAD_DIM:])
            total = jnp.where(low_half, even[:, 2 * HEAD_DIM:], odd[:, :2 * HEAD_DIM])
            sink_e = jnp.where(low_half, jnp.exp(sinks_ref[heads[j]] - maxes[j]),
                               jnp.exp(sinks_ref[heads[j + 1]] - maxes[j + 1]))
            cols = slice(heads[j] * HEAD_DIM, (heads[j] + 2) * HEAD_DIM)
            o_ref[:, cols] = (num * (1.0 / (total + sink_e))).astype(o_ref.dtype)


def _attention_bias():
    qi = jnp.arange(ATTN_BLOCK, dtype=jnp.int32)[:, None]
    kc = jnp.arange(ATTN_BLOCK, dtype=jnp.int32)[None, :]
    upper = kc > qi
    dist = (qi - kc + jnp.where(upper, ATTN_BLOCK, 0)).astype(F32)
    head = jnp.arange(1, N_Q_HEADS + 1, dtype=F32)
    slopes = jnp.exp2(-8.0 * head / N_Q_HEADS)
    alibi = -slopes[:, None, None] * dist[None]
    return jnp.stack([alibi, jnp.where(upper[None], -jnp.inf, alibi)])


def _attention(qkv, sinks, *, seq_len):
    assert WINDOW == ATTN_BLOCK
    t = qkv.shape[0]
    blocks_per_seq = seq_len // ATTN_BLOCK
    kcol = Q_DIM // KV_DIM
    vcol = kcol + 1
    prev = lambda i: jnp.maximum(i - 1, 0)
    grid_spec = pltpu.PrefetchScalarGridSpec(
        num_scalar_prefetch=1,
        grid=(t // ATTN_BLOCK,),
        in_specs=[
            pl.BlockSpec((ATTN_BLOCK, Q_DIM), lambda i, s: (i, 0)),
            pl.BlockSpec((ATTN_BLOCK, KV_DIM), lambda i, s: (prev(i), kcol)),
            pl.BlockSpec((ATTN_BLOCK, KV_DIM), lambda i, s: (i, kcol)),
            pl.BlockSpec((ATTN_BLOCK, KV_DIM), lambda i, s: (prev(i), vcol)),
            pl.BlockSpec((ATTN_BLOCK, KV_DIM), lambda i, s: (i, vcol)),
            pl.BlockSpec((2, N_Q_HEADS, ATTN_BLOCK, ATTN_BLOCK), lambda i, s: (0, 0, 0, 0)),
        ],
        out_specs=pl.BlockSpec((ATTN_BLOCK, Q_DIM), lambda i, s: (i, 0)),
    )
    return pl.pallas_call(
        functools.partial(_attn_kernel, blocks_per_seq=blocks_per_seq),
        grid_spec=grid_spec,
        out_shape=jax.ShapeDtypeStruct((t, Q_DIM), BF16),
        name="swa_attention",
        compiler_params=_params("parallel"),
    )(sinks, qkv, qkv, qkv, qkv, qkv, _attention_bias())


def _router_kernel(x_ref, wr_ref, rb_ref, idx_ref, wts_ref, cnt_ref, carry_ref, *, tm):
    @pl.when(pl.program_id(0) == 0)
    def _():
        carry_ref[...] = jnp.zeros_like(carry_ref)

    x = x_ref[...]
    x_hi = x.astype(BF16)
    x_lo = (x - x_hi.astype(F32)).astype(BF16)
    w_split = wr_ref[...]
    by_hi = jnp.dot(x_hi, w_split, preferred_element_type=F32)
    by_lo = jnp.dot(x_lo, w_split[:, :LANES], preferred_element_type=F32)
    logits = by_hi[:, :LANES] + (by_hi[:, LANES:] + by_lo)
    sc = jax.nn.sigmoid(logits.T[:N_EXPERTS])
    sel = sc + rb_ref[...]
    neg = -jnp.inf
    iota_g = lax.broadcasted_iota(jnp.int32, (GROUP_SIZE, tm), 0).astype(F32)

    def first_argmax(v, iota, size):
        m = jnp.max(v, axis=0, keepdims=True)
        return jnp.min(jnp.where(v == m, iota, float(size)), axis=0, keepdims=True)

    group_rows = []
    for g in range(N_GROUPS):
        v = sel[g * GROUP_SIZE:(g + 1) * GROUP_SIZE]
        m1 = jnp.max(v, axis=0, keepdims=True)
        i1 = first_argmax(v, iota_g, GROUP_SIZE)
        m2 = jnp.max(jnp.where(iota_g == i1, neg, v), axis=0, keepdims=True)
        group_rows.append(m1 + m2)
    cur = jnp.concatenate(group_rows, axis=0)
    iota_n = lax.broadcasted_iota(jnp.int32, (N_GROUPS, tm), 0).astype(F32)
    gsel = jnp.zeros((N_GROUPS, tm), F32)
    for _ in range(TOPK_GROUPS):
        hit = iota_n == first_argmax(cur, iota_n, N_GROUPS)
        gsel = jnp.where(hit, 1.0, gsel)
        cur = jnp.where(hit, neg, cur)
    cur = jnp.concatenate(
        [jnp.where(gsel[g:g + 1] > 0.5, sel[g * GROUP_SIZE:(g + 1) * GROUP_SIZE], neg)
         for g in range(N_GROUPS)], axis=0)

    iota_e = lax.broadcasted_iota(jnp.int32, (N_EXPERTS, tm), 0).astype(F32)
    member = jnp.zeros((N_EXPERTS, tm), F32)
    idx_rows, w_rows = [], []
    for _ in range(TOP_K):
        ii = first_argmax(cur, iota_e, N_EXPERTS)
        hit = iota_e == ii
        idx_rows.append(ii)
        w_rows.append(jnp.sum(jnp.where(hit, sc, 0.0), axis=0, keepdims=True))
        member = jnp.where(hit, 1.0, member)
        cur = jnp.where(hit, neg, cur)
    w = jnp.concatenate(w_rows, axis=0)
    w = w / (jnp.sum(w, axis=0, keepdims=True) + 1e-20) * ROUTED_SCALE

    idx_ref[...] = jnp.concatenate(idx_rows, axis=0).astype(jnp.int32)
    wts_ref[...] = w
    carry_ref[...] = carry_ref[...] + jnp.sum(member, axis=1, keepdims=True)
    cnt_ref[...] = jnp.broadcast_to(carry_ref[...], cnt_ref.shape)


def _router(x, w_router, router_bias, *, tm):
    t, d = x.shape
    tok = lambda i: (0, i)
    w_padded = jnp.pad(w_router, ((0, 0), (0, LANES - N_EXPERTS)))
    w_hi = w_padded.astype(BF16)
    w_lo = (w_padded - w_hi.astype(F32)).astype(BF16)
    w_split = jnp.concatenate([w_hi, w_lo], axis=1)
    return pl.pallas_call(
        functools.partial(_router_kernel, tm=tm),
        grid=(t // tm,),
        in_specs=[
            pl.BlockSpec((tm, d), lambda i: (i, 0)),
            pl.BlockSpec((d, 2 * LANES), lambda i: (0, 0)),
            pl.BlockSpec((N_EXPERTS, 1), lambda i: (0, 0)),
        ],
        out_specs=[
            pl.BlockSpec((TOP_K, tm), tok),
            pl.BlockSpec((TOP_K, tm), tok),
            pl.BlockSpec((N_EXPERTS, LANES), lambda i: (0, 0)),
        ],
        out_shape=[
            jax.ShapeDtypeStruct((TOP_K, t), jnp.int32),
            jax.ShapeDtypeStruct((TOP_K, t), F32),
            jax.ShapeDtypeStruct((N_EXPERTS, LANES), F32),
        ],
        scratch_shapes=[pltpu.VMEM((N_EXPERTS, 1), F32)],
        name="router",
        compiler_params=_params("arbitrary"),
    )(x, w_split, router_bias.reshape(N_EXPERTS, 1))


def _experts_kernel(be_ref, ne_ref, nu_ref, plan_hbm, x_hbm, wg_hbm, wu_hbm, wd_hbm, y_hbm,
                    plan_smem, xbuf0, xbuf1, ybuf0, ybuf1, wg_f32, wu_f32, wd_f32,
                    wg_bf, wu_bf, wd_bf, psem, gsem, ssem, wsem, *, tok_mask, layer):
    b = pl.program_id(0)
    n_used = nu_ref[0]
    rows = MOE_ROWS
    xbuf = (xbuf0, xbuf1)
    ybuf = (ybuf0, ybuf1)

    def weight_copies(expert):
        return [pltpu.make_async_copy(w_hbm.at[layer, expert], stage, wsem.at[i])
                for i, (w_hbm, stage) in enumerate(
                    ((wg_hbm, wg_f32), (wu_hbm, wu_f32), (wd_hbm, wd_f32)))]

    def plan_copy(row, slot):
        return pltpu.make_async_copy(plan_hbm.at[row], plan_smem.at[slot], psem.at[slot])

    def start_gathers(pslot, slot):
        for r in range(rows):
            pltpu.make_async_copy(x_hbm.at[plan_smem[pslot, r] & tok_mask],
                                  xbuf[slot].at[:, r, :], gsem.at[slot]).start()

    def start_scatters(pslot, slot):
        for r in range(rows):
            pltpu.make_async_copy(ybuf[slot].at[:, r, :], y_hbm.at[plan_smem[pslot, rows + r]],
                                  ssem.at[slot]).start(priority=1)

    def wait_rows(buf, sem):
        pltpu.make_async_copy(buf, buf, sem).wait()

    @pl.when(b == 0)
    def _():
        ybuf0[...] = jnp.zeros_like(ybuf0)
        ybuf1[...] = jnp.zeros_like(ybuf1)
        spare0 = y_hbm.shape[0] - 2 * rows
        for r in range(rows):
            pltpu.make_async_copy(ybuf0.at[:, r, :], y_hbm.at[spare0 + r], ssem.at[0]).start()
        plan_copy(0, 0).start()
        for copy in weight_copies(be_ref[0]):
            copy.start()
        plan_copy(0, 0).wait()
        start_gathers(0, 0)
        plan_copy(1, 1).start()

    def step(slot):
        other = 1 - slot
        changed = jnp.logical_or(b == 0, be_ref[b] != be_ref[jnp.maximum(b - 1, 0)])

        @pl.when(changed)
        def _():
            for copy in weight_copies(be_ref[b]):
                copy.wait()
            wg_bf[...] = wg_f32[...].astype(BF16)
            wu_bf[...] = wu_f32[...].astype(BF16)
            wd_bf[...] = wd_f32[...].astype(BF16)
            upcoming = ne_ref[b]

            @pl.when(upcoming >= 0)
            def _():
                for copy in weight_copies(upcoming):
                    copy.start()

        wait_rows(xbuf[slot], gsem.at[slot])
        wait_rows(ybuf[slot], ssem.at[slot])
        plan_copy(b + 1, other).wait()
        start_gathers(other, other)
        start_scatters(other, other)
        plan_copy(b + 2, slot).start()

        pieces = _unpack_words([xbuf[slot][s] for s in range(PACK_SUB)])
        x = jnp.concatenate([p.astype(BF16) for p in pieces], axis=1)
        gate = jnp.dot(x, wg_bf[...], preferred_element_type=F32)
        up = jnp.dot(x, wu_bf[...], preferred_element_type=F32)
        h = (_silu(gate) * up).astype(BF16)
        y = jnp.dot(h, wd_bf[...], preferred_element_type=F32)
        for s, word in enumerate(_pack_words(y)):
            ybuf[slot][s] = word

        @pl.when(b == n_used - 1)
        def _():
            plan_copy(b + 2, slot).wait()
            start_scatters(slot, slot)
            wait_rows(xbuf[other], gsem.at[other])
            wait_rows(ybuf[other], ssem.at[other])
            wait_rows(ybuf[slot], ssem.at[slot])

    for parity in range(2):
        pl.when(jnp.logical_and(b < n_used, b % 2 == parity))(functools.partial(step, parity))


def _experts(x_packed, layer, w_gate, w_up, w_down, block_expert, next_expert, n_used, plan, *,
             n_out_rows):
    t = x_packed.shape[0] // PACK_SUB
    d = D_MODEL
    assert t & (t - 1) == 0, "token id is taken as the low bits of the flat (choice, token) index"
    n_blocks = plan.shape[0] - 2
    row_buf = pltpu.VMEM((PACK_SUB, MOE_ROWS, LANES), jnp.int32)
    grid_spec = pltpu.PrefetchScalarGridSpec(
        num_scalar_prefetch=3,
        grid=(n_blocks,),
        in_specs=[pl.BlockSpec(memory_space=pl.ANY)] * 5,
        out_specs=pl.BlockSpec(memory_space=pl.ANY),
        scratch_shapes=[
            pltpu.SMEM((2, 2 * MOE_ROWS), jnp.int32),
            row_buf, row_buf, row_buf, row_buf,
            pltpu.VMEM((d, EXPERT_FF), F32),
            pltpu.VMEM((d, EXPERT_FF), F32),
            pltpu.VMEM((EXPERT_FF, d), F32),
            pltpu.VMEM((d, EXPERT_FF), BF16),
            pltpu.VMEM((d, EXPERT_FF), BF16),
            pltpu.VMEM((EXPERT_FF, d), BF16),
            pltpu.SemaphoreType.DMA((2,)),
            pltpu.SemaphoreType.DMA((2,)),
            pltpu.SemaphoreType.DMA((2,)),
            pltpu.SemaphoreType.DMA((3,)),
        ],
    )
    y = pl.pallas_call(
        functools.partial(_experts_kernel, tok_mask=t - 1, layer=layer),
        grid_spec=grid_spec,
        out_shape=jax.ShapeDtypeStruct((n_out_rows, PACK_SUB, LANES), jnp.int32),
        name="routed_experts",
        compiler_params=_params("arbitrary"),
    )(block_expert, next_expert, n_used, plan, x_packed.reshape(t, PACK_SUB, LANES),
      w_gate, w_up, w_down)
    return y.reshape(n_out_rows * PACK_SUB, LANES)


def _combine_kernel(x_ref, wg_ref, wu_ref, wd_ref, w_ref, *refs, tm):
    y_refs = refs[:TOP_K]
    g_ref, beta_ref, o_ref = refs[TOP_K:]
    x = x_ref[...]
    xb = x.astype(BF16)
    gate = jnp.dot(xb, wg_ref[...], preferred_element_type=F32)
    up = jnp.dot(xb, wu_ref[...], preferred_element_type=F32)
    h = (_silu(gate) * up).astype(BF16)
    s = DN_ALPHA * x + jnp.dot(h, wd_ref[...], preferred_element_type=F32)
    pieces = [s[:, c * LANES:(c + 1) * LANES] for c in range(2 * PACK_SUB)]
    for k in range(TOP_K):
        wk = w_ref[:, k:k + 1]
        yk = _load_packed_rows(y_refs[k], tm)
        pieces = [p + wk * y for p, y in zip(pieces, yk)]
    o_ref[...] = _layer_norm(jnp.concatenate(pieces, axis=1), g_ref[...], beta_ref[...])


def _combine(x, ws_gate, ws_up, ws_down, wts_t, y, g, beta, *, tm):
    t, d = x.shape
    f = ws_gate.shape[1]
    blocks = t // tm
    fixed = lambda i: (0, 0)
    y_specs = [pl.BlockSpec((tm * PACK_SUB, LANES),
                            functools.partial(lambda i, k: (k * blocks + i, 0), k=k))
               for k in range(TOP_K)]
    return pl.pallas_call(
        functools.partial(_combine_kernel, tm=tm),
        grid=(blocks,),
        in_specs=[
            pl.BlockSpec((tm, d), lambda i: (i, 0)),
            pl.BlockSpec((d, f), fixed),
            pl.BlockSpec((d, f), fixed),
            pl.BlockSpec((f, d), fixed),
            pl.BlockSpec((tm, TOP_K), lambda i: (i, 0)),
            *y_specs,
            pl.BlockSpec((1, d), fixed),
            pl.BlockSpec((1, d), fixed),
        ],
        out_specs=pl.BlockSpec((tm, d), lambda i: (i, 0)),
        out_shape=jax.ShapeDtypeStruct((t, d), F32),
        name="moe_combine",
        compiler_params=_params("parallel"),
    )(x, ws_gate, ws_up, ws_down, wts_t, *([y] * TOP_K), g.reshape(1, d), beta.reshape(1, d))


def _dispatch_plan(idx, counts, n_tok):
    n_assign = n_tok * TOP_K
    n_blocks = n_assign // MOE_ROWS + N_EXPERTS
    n_slots = n_blocks * MOE_ROWS
    blocks_per_expert = (counts + MOE_ROWS - 1) // MOE_ROWS
    block_end = jnp.cumsum(blocks_per_expert)
    experts = jnp.arange(N_EXPERTS, dtype=jnp.int32)
    slot = jnp.arange(n_slots, dtype=jnp.int32)
    payload_bits = (n_assign - 1).bit_length()
    n_pad = n_slots - n_assign
    assert n_pad <= 1 << payload_bits and (N_EXPERTS + 1) << (payload_bits + 1) < 1 << 31
    k_of = lax.broadcasted_iota(jnp.int32, idx.shape, 0)
    t_of = lax.broadcasted_iota(jnp.int32, idx.shape, 1)
    real_keys = (idx << (payload_bits + 1)) | (t_of * TOP_K + k_of)
    pad_counts = jnp.concatenate([blocks_per_expert * MOE_ROWS - counts,
                                  n_slots - block_end[-1:] * MOE_ROWS])
    pad_end = jnp.cumsum(pad_counts)
    j = jnp.arange(n_pad, dtype=jnp.int32)
    pad_owner = jnp.sum((pad_end[None, :] <= j[:, None]).astype(jnp.int32), axis=1)
    pad_keys = (pad_owner << (payload_bits + 1)) | (1 << payload_bits) | j
    keys = lax.sort(jnp.concatenate([real_keys.reshape(-1), pad_keys]))
    payload = keys & ((1 << payload_bits) - 1)
    flat = (payload % TOP_K) * n_tok + payload // TOP_K
    spare = n_assign + ((slot // MOE_ROWS) % 2) * MOE_ROWS + slot % MOE_ROWS
    is_pad = (keys >> payload_bits) & 1
    slot_flat = jnp.where(is_pad == 1, spare, flat).reshape(n_blocks, MOE_ROWS)
    rows = slot_flat
    pad = jnp.broadcast_to(n_assign + MOE_ROWS + slot[:MOE_ROWS], (2, MOE_ROWS))
    plan = jnp.concatenate([jnp.concatenate([rows, pad], axis=0),
                            jnp.concatenate([pad, rows], axis=0)], axis=1)
    blocks = jnp.arange(n_blocks, dtype=jnp.int32)
    block_expert = jnp.minimum(
        jnp.sum((block_end[None, :] <= blocks[:, None]).astype(jnp.int32), axis=1), N_EXPERTS - 1)
    n_used = block_end[-1:].astype(jnp.int32)
    later_used = (experts[None, :] > experts[:, None]) & (blocks_per_expert[None, :] > 0)
    following = jnp.min(jnp.where(later_used, experts[None, :], N_EXPERTS), axis=1)
    following = jnp.where(following == N_EXPERTS, -1, following)
    next_expert = jnp.sum(jnp.where(block_expert[:, None] == experts[None, :], following, 0),
                          axis=1).astype(jnp.int32)
    return block_expert, next_expert, n_used, plan, n_assign + 2 * MOE_ROWS


def _moe(x, x_packed, layer, w_router, router_bias, w_gate, w_up, w_down, ws_gate, ws_up, ws_down,
         g, beta):
    n_tok = x.shape[0]
    idx, wts, cnt = _router(x, w_router, router_bias, tm=512)
    counts = cnt[:, 0].astype(jnp.int32)
    block_expert, next_expert, n_used, plan, n_out_rows = _dispatch_plan(idx, counts, n_tok)
    y = _experts(x_packed, layer, w_gate, w_up, w_down, block_expert, next_expert, n_used, plan,
                 n_out_rows=n_out_rows)
    return _combine(x, ws_gate.astype(BF16), ws_up.astype(BF16), ws_down.astype(BF16), wts.T, y,
                    g, beta, tm=256)


def kernel(x, a_w_in, a_b_in, a_ln_g, a_ln_b, a_w_s, a_b_s, a_w_out, b_w_qkv, b_b_qkv, b_sinks,
           b_w_o, b_b_o, moe_w_router, moe_router_bias, moe_w_gate, moe_w_up, moe_w_down,
           moe_ws_gate, moe_ws_up, moe_ws_down, norm_g, norm_b):
    bsz, seq_len, d = x.shape
    h = x.reshape(bsz * seq_len, d)
    for i in range(DEPTH):
        j = i // 2
        if i % 2 == 0:
            z = _linear(h, a_w_in[j].astype(BF16), a_b_in[j], act="gelu", tm=1024, tn=1024,
                        out_dtype=BF16)
            causal = jnp.tril(jnp.ones((A_CHUNK, A_CHUNK), dtype=bool))
            w_mix = jnp.where(causal[None], a_w_s[j], 0.0).astype(BF16)
            gated = _gmlp_gate(z, a_ln_g[j], a_ln_b[j], w_mix, a_b_s[j].T, tm=256)
            h, h_packed = _linear_res_ln(gated, a_w_out[j].astype(BF16), jnp.zeros((d,), F32), h,
                                         norm_g[i, 0], norm_b[i, 0], tm=512)
        else:
            q_scale = jnp.where(jnp.arange(QKV_DIM) < Q_DIM, HEAD_DIM ** -0.5, 1.0).astype(F32)
            qkv = _linear(h, (b_w_qkv[j] * q_scale).astype(BF16), b_b_qkv[j] * q_scale, act=None,
                          tm=1024, tn=1280, out_dtype=BF16)
            o = _attention(qkv, b_sinks[j], seq_len=seq_len)
            h, h_packed = _linear_res_ln(o, b_w_o[j].astype(BF16), b_b_o[j], h,
                                         norm_g[i, 0], norm_b[i, 0], tm=512)
        h = _moe(h, h_packed, i, moe_w_router[i], moe_router_bias[i], moe_w_gate, moe_w_up, moe_w_down,
                 moe_ws_gate[i], moe_ws_up[i], moe_ws_down[i], norm_g[i, 1], norm_b[i, 1])
    return h.reshape(bsz, seq_len, d)
```

```python
import functools

import jax
import jax.numpy as jnp
from jax import lax
from jax.experimental import pallas as pl
from jax.experimental.pallas import tpu as pltpu

F32 = jnp.float32
BF16 = jnp.bfloat16

D_MODEL = 2048
DEPTH = 2
A_CHUNK = 128
A_HALF = D_MODEL
A_GROUPS = A_HALF // 128
HEAD_DIM = 64
N_Q_HEADS = D_MODEL // HEAD_DIM
N_KV_HEADS = N_Q_HEADS // 8
Q_PER_KV = N_Q_HEADS // N_KV_HEADS
Q_DIM = N_Q_HEADS * HEAD_DIM
KV_DIM = N_KV_HEADS * HEAD_DIM
QKV_DIM = Q_DIM + 2 * KV_DIM
WINDOW = 128
ATTN_BLOCK = 128
N_EXPERTS = 64
TOP_K = 8
N_GROUPS = 8
GROUP_SIZE = N_EXPERTS // N_GROUPS
TOPK_GROUPS = 4
EXPERT_FF = D_MODEL // 4
ROUTED_SCALE = 2.5
DN_ALPHA = (2 * DEPTH) ** 0.25
LN_EPS = 1e-5

MXU_ROWS = 256
MOE_ROWS = 2 * MXU_ROWS
VMEM_LIMIT = 56 * 1024 * 1024


def _params(*sem):
    return pltpu.CompilerParams(dimension_semantics=sem, vmem_limit_bytes=VMEM_LIMIT)


def _layer_norm(y, g, b):
    mu = jnp.mean(y, axis=-1, keepdims=True)
    d = y - mu
    var = jnp.mean(d * d, axis=-1, keepdims=True)
    return d * lax.rsqrt(var + LN_EPS) * g + b


def _gelu(x):
    return 0.5 * x * (1.0 + lax.erf(x * (2.0 ** -0.5)))


def _silu(x):
    return x * jax.nn.sigmoid(x)


LANES = 128
PACK_SUB = D_MODEL // (2 * LANES)
HIGH_HALF = -65536


def _pack_words(rows_f32):
    bits = lax.bitcast_convert_type(rows_f32.astype(BF16).astype(F32), jnp.int32)
    half = D_MODEL // 2
    return [(bits[:, s * LANES:(s + 1) * LANES] & HIGH_HALF)
            | lax.shift_right_logical(bits[:, half + s * LANES:half + (s + 1) * LANES], 16)
            for s in range(PACK_SUB)]


def _unpack_words(words):
    his = [lax.bitcast_convert_type(w & HIGH_HALF, F32) for w in words]
    los = [lax.bitcast_convert_type(lax.shift_left(w, 16), F32) for w in words]
    return his + los


def _store_packed_rows(p_ref, rows_f32):
    n = rows_f32.shape[0]
    for s, word in enumerate(_pack_words(rows_f32)):
        p_ref[pl.ds(s, n, stride=PACK_SUB), :] = word


def _load_packed_rows(p_ref, n):
    return _unpack_words([p_ref[pl.ds(s, n, stride=PACK_SUB), :] for s in range(PACK_SUB)])


def _linear_kernel(x_ref, w_ref, b_ref, o_ref, *, act):
    acc = jnp.dot(x_ref[...].astype(BF16), w_ref[...], preferred_element_type=F32)
    acc = acc + b_ref[...]
    if act == "gelu":
        acc = _gelu(acc)
    o_ref[...] = acc.astype(o_ref.dtype)


def _linear(x, w, b, *, act, tm, tn, out_dtype):
    m, k = x.shape
    n = w.shape[1]
    return pl.pallas_call(
        functools.partial(_linear_kernel, act=act),
        grid=(m // tm, n // tn),
        in_specs=[
            pl.BlockSpec((tm, k), lambda i, j: (i, 0)),
            pl.BlockSpec((k, tn), lambda i, j: (0, j)),
            pl.BlockSpec((1, tn), lambda i, j: (0, j)),
        ],
        out_specs=pl.BlockSpec((tm, tn), lambda i, j: (i, j)),
        out_shape=jax.ShapeDtypeStruct((m, n), out_dtype),
        name="linear_" + str(act),
        compiler_params=_params("parallel", "arbitrary"),
    )(x, w, b.reshape(1, n))


def _linear_res_ln_kernel(x_ref, w_ref, b_ref, res_ref, g_ref, beta_ref, o_ref, p_ref):
    acc = jnp.dot(x_ref[...], w_ref[...], preferred_element_type=F32)
    y = DN_ALPHA * res_ref[...] + (acc + b_ref[...])
    out = _layer_norm(y, g_ref[...], beta_ref[...])
    o_ref[...] = out
    _store_packed_rows(p_ref, out)


def _linear_res_ln(x, w, b, res, g, beta, *, tm):
    m, k = x.shape
    n = w.shape[1]
    row = lambda i: (i, 0)
    fixed = lambda i: (0, 0)
    return pl.pallas_call(
        _linear_res_ln_kernel,
        grid=(m // tm,),
        in_specs=[
            pl.BlockSpec((tm, k), row),
            pl.BlockSpec((k, n), fixed),
            pl.BlockSpec((1, n), fixed),
            pl.BlockSpec((tm, n), row),
            pl.BlockSpec((1, n), fixed),
            pl.BlockSpec((1, n), fixed),
        ],
        out_specs=[pl.BlockSpec((tm, n), row), pl.BlockSpec((tm * PACK_SUB, LANES), row)],
        out_shape=[jax.ShapeDtypeStruct((m, n), F32),
                   jax.ShapeDtypeStruct((m * PACK_SUB, LANES), jnp.int32)],
        name="linear_res_ln",
        compiler_params=_params("parallel"),
    )(x, w, b.reshape(1, n), res, g.reshape(1, n), beta.reshape(1, n))


def _gmlp_gate_kernel(z_ref, lng_ref, lnb_ref, wmix_ref, bst_ref, o_ref, *, tm):
    v = z_ref[:, A_HALF:].astype(F32)
    vn = _layer_norm(v, lng_ref[...], lnb_ref[...]).astype(BF16)
    for c in range(tm // A_CHUNK):
        rows = slice(c * A_CHUNK, (c + 1) * A_CHUNK)
        for g in range(A_GROUPS):
            cols = slice(g * 128, (g + 1) * 128)
            mixed = jnp.dot(wmix_ref[g], vn[rows, cols], preferred_element_type=F32)
            mixed = mixed + bst_ref[:, g:g + 1]
            u = z_ref[rows, cols].astype(F32)
            o_ref[rows, cols] = (u * mixed).astype(o_ref.dtype)


def _gmlp_gate(z, ln_g, ln_b, w_mix, b_s_t, *, tm):
    m = z.shape[0]
    return pl.pallas_call(
        functools.partial(_gmlp_gate_kernel, tm=tm),
        grid=(m // tm,),
        in_specs=[
            pl.BlockSpec((tm, 2 * A_HALF), lambda i: (i, 0)),
            pl.BlockSpec((1, A_HALF), lambda i: (0, 0)),
            pl.BlockSpec((1, A_HALF), lambda i: (0, 0)),
            pl.BlockSpec((A_GROUPS, A_CHUNK, A_CHUNK), lambda i: (0, 0, 0)),
            pl.BlockSpec((A_CHUNK, A_GROUPS), lambda i: (0, 0)),
        ],
        out_specs=pl.BlockSpec((tm, A_HALF), lambda i: (i, 0)),
        out_shape=jax.ShapeDtypeStruct((m, A_HALF), BF16),
        name="gmlp_gate",
        compiler_params=_params("parallel"),
    )(z, ln_g.reshape(1, A_HALF), ln_b.reshape(1, A_HALF), w_mix, b_s_t)


def _attn_kernel(sinks_ref, q_ref, kp_ref, kc_ref, vp_ref, vc_ref, bias_ref, o_ref, *,
                 blocks_per_seq):
    n = pl.program_id(0) % blocks_per_seq
    first = (n == 0).astype(jnp.int32)
    qi = lax.broadcasted_iota(jnp.int32, (ATTN_BLOCK, ATTN_BLOCK), 0)
    ci = lax.broadcasted_iota(jnp.int32, (ATTN_BLOCK, ATTN_BLOCK), 1)
    upper = ci > qi
    low_half = lax.broadcasted_iota(jnp.int32, (ATTN_BLOCK, 2 * HEAD_DIM), 1) < HEAD_DIM
    ones = jnp.ones((2 * ATTN_BLOCK, HEAD_DIM), BF16)
    for g in range(N_KV_HEADS):
        kv_cols = slice(g * HEAD_DIM, (g + 1) * HEAD_DIM)
        kk = jnp.concatenate([kp_ref[:, kv_cols], kc_ref[:, kv_cols]], axis=0)
        vv = jnp.concatenate([vp_ref[:, kv_cols], vc_ref[:, kv_cols]], axis=0)
        vv_ones = jnp.concatenate([vv, ones, ones, vv], axis=1)
        heads = [g * Q_PER_KV + j for j in range(Q_PER_KV)]
        qg = jnp.concatenate([q_ref[:, h * HEAD_DIM:(h + 1) * HEAD_DIM] for h in heads], axis=0)
        s = lax.dot_general(qg, kk, (((1,), (1,)), ((), ())), preferred_element_type=F32)
        es, maxes = [], []
        for j, h in enumerate(heads):
            sj = s[j * ATTN_BLOCK:(j + 1) * ATTN_BLOCK]
            sm = jnp.where(upper, sj[:, :ATTN_BLOCK], sj[:, ATTN_BLOCK:]) + bias_ref[first, h]
            mx = jnp.maximum(jnp.max(sm, axis=-1, keepdims=True), sinks_ref[h])
            e = jnp.exp(sm - mx)
            es.append(jnp.concatenate([jnp.where(upper, e, 0.0), jnp.where(upper, 0.0, e)],
                                      axis=1).astype(BF16))
            maxes.append(mx)
        o = jnp.dot(jnp.concatenate(es, axis=0), vv_ones, preferred_element_type=F32)
        for j in range(0, Q_PER_KV, 2):
            even = o[j * ATTN_BLOCK:(j + 1) * ATTN_BLOCK]
            odd = o[(j + 1) * ATTN_BLOCK:(j + 2) * ATTN_BLOCK]
            num = jnp.where(low_half, even[:, :2 * HEAD_DIM], odd[:, 2 * HEAD_DIM:])
            total = jnp.where(low_half, even[:, 2 * HEAD_DIM:], odd[:, :2 * HEAD_DIM])
            sink_e = jnp.where(low_half, jnp.exp(sinks_ref[heads[j]] - maxes[j]),
                               jnp.exp(sinks_ref[heads[j + 1]] - maxes[j + 1]))
            cols = slice(heads[j] * HEAD_DIM, (heads[j] + 2) * HEAD_DIM)
            o_ref[:, cols] = (num * (1.0 / (total + sink_e))).astype(o_ref.dtype)


def _attention_bias():
    qi = jnp.arange(ATTN_BLOCK, dtype=jnp.int32)[:, None]
    kc = jnp.arange(ATTN_BLOCK, dtype=jnp.int32)[None, :]
    upper = kc > qi
    dist = (qi - kc + jnp.where(upper, ATTN_BLOCK, 0)).astype(F32)
    head = jnp.arange(1, N_Q_HEADS + 1, dtype=F32)
    slopes = jnp.exp2(-8.0 * head / N_Q_HEADS)
    alibi = -slopes[:, None, None] * dist[None]
    return jnp.stack([alibi, jnp.where(upper[None], -jnp.inf, alibi)])


def _attention(qkv, sinks, *, seq_len):
    assert WINDOW == ATTN_BLOCK
    t = qkv.shape[0]
    blocks_per_seq = seq_len // ATTN_BLOCK
    kcol = Q_DIM // KV_DIM
    vcol = kcol + 1
    prev = lambda i: jnp.maximum(i - 1, 0)
    grid_spec = pltpu.PrefetchScalarGridSpec(
        num_scalar_prefetch=1,
        grid=(t // ATTN_BLOCK,),
        in_specs=[
            pl.BlockSpec((ATTN_BLOCK, Q_DIM), lambda i, s: (i, 0)),
            pl.BlockSpec((ATTN_BLOCK, KV_DIM), lambda i, s: (prev(i), kcol)),
            pl.BlockSpec((ATTN_BLOCK, KV_DIM), lambda i, s: (i, kcol)),
            pl.BlockSpec((ATTN_BLOCK, KV_DIM), lambda i, s: (prev(i), vcol)),
            pl.BlockSpec((ATTN_BLOCK, KV_DIM), lambda i, s: (i, vcol)),
            pl.BlockSpec((2, N_Q_HEADS, ATTN_BLOCK, ATTN_BLOCK), lambda i, s: (0, 0, 0, 0)),
        ],
        out_specs=pl.BlockSpec((ATTN_BLOCK, Q_DIM), lambda i, s: (i, 0)),
    )
    return pl.pallas_call(
        functools.partial(_attn_kernel, blocks_per_seq=blocks_per_seq),
        grid_spec=grid_spec,
        out_shape=jax.ShapeDtypeStruct((t, Q_DIM), BF16),
        name="swa_attention",
        compiler_params=_params("parallel"),
    )(sinks, qkv, qkv, qkv, qkv, qkv, _attention_bias())


def _router_kernel(x_ref, wr_ref, rb_ref, idx_ref, wts_ref, cnt_ref, carry_ref, *, tm):
    @pl.when(pl.program_id(0) == 0)
    def _():
        carry_ref[...] = jnp.zeros_like(carry_ref)

    x = x_ref[...]
    x_hi = x.astype(BF16)
    x_lo = (x - x_hi.astype(F32)).astype(BF16)
    w_split = wr_ref[...]
    by_hi = jnp.dot(x_hi, w_split, preferred_element_type=F32)
    by_lo = jnp.dot(x_lo, w_split[:, :LANES], preferred_element_type=F32)
    logits = by_hi[:, :LANES] + (by_hi[:, LANES:] + by_lo)
    sc = jax.nn.sigmoid(logits.T[:N_EXPERTS])
    sel = sc + rb_ref[...]
    neg = -jnp.inf
    iota_g = lax.broadcasted_iota(jnp.int32, (GROUP_SIZE, tm), 0).astype(F32)

    def first_argmax(v, iota, size):
        m = jnp.max(v, axis=0, keepdims=True)
        return jnp.min(jnp.where(v == m, iota, float(size)), axis=0, keepdims=True)

    group_rows = []
    for g in range(N_GROUPS):
        v = sel[g * GROUP_SIZE:(g + 1) * GROUP_SIZE]
        m1 = jnp.max(v, axis=0, keepdims=True)
        i1 = first_argmax(v, iota_g, GROUP_SIZE)
        m2 = jnp.max(jnp.where(iota_g == i1, neg, v), axis=0, keepdims=True)
        group_rows.append(m1 + m2)
    cur = jnp.concatenate(group_rows, axis=0)
    iota_n = lax.broadcasted_iota(jnp.int32, (N_GROUPS, tm), 0).astype(F32)
    gsel = jnp.zeros((N_GROUPS, tm), F32)
    for _ in range(TOPK_GROUPS):
        hit = iota_n == first_argmax(cur, iota_n, N_GROUPS)
        gsel = jnp.where(hit, 1.0, gsel)
        cur = jnp.where(hit, neg, cur)
    cur = jnp.concatenate(
        [jnp.where(gsel[g:g + 1] > 0.5, sel[g * GROUP_SIZE:(g + 1) * GROUP_SIZE], neg)
         for g in range(N_GROUPS)], axis=0)

    iota_e = lax.broadcasted_iota(jnp.int32, (N_EXPERTS, tm), 0).astype(F32)
    member = jnp.zeros((N_EXPERTS, tm), F32)
    idx_rows, w_rows = [], []
    for _ in range(TOP_K):
        ii = first_argmax(cur, iota_e, N_EXPERTS)
        hit = iota_e == ii
        idx_rows.append(ii)
        w_rows.append(jnp.sum(jnp.where(hit, sc, 0.0), axis=0, keepdims=True))
        member = jnp.where(hit, 1.0, member)
        cur = jnp.where(hit, neg, cur)
    w = jnp.concatenate(w_rows, axis=0)
    w = w / (jnp.sum(w, axis=0, keepdims=True) + 1e-20) * ROUTED_SCALE

    idx_ref[...] = jnp.concatenate(idx_rows, axis=0).astype(jnp.int32)
    wts_ref[...] = w
    carry_ref[...] = carry_ref[...] + jnp.sum(member, axis=1, keepdims=True)
    cnt_ref[...] = jnp.broadcast_to(carry_ref[...], cnt_ref.shape)


def _router(x, w_router, router_bias, *, tm):
    t, d = x.shape
    tok = lambda i: (0, i)
    w_padded = jnp.pad(w_router, ((0, 0), (0, LANES - N_EXPERTS)))
    w_hi = w_padded.astype(BF16)
    w_lo = (w_padded - w_hi.astype(F32)).astype(BF16)
    w_split = jnp.concatenate([w_hi, w_lo], axis=1)
    return pl.pallas_call(
        functools.partial(_router_kernel, tm=tm),
        grid=(t // tm,),
        in_specs=[
            pl.BlockSpec((tm, d), lambda i: (i, 0)),
            pl.BlockSpec((d, 2 * LANES), lambda i: (0, 0)),
            pl.BlockSpec((N_EXPERTS, 1), lambda i: (0, 0)),
        ],
        out_specs=[
            pl.BlockSpec((TOP_K, tm), tok),
            pl.BlockSpec((TOP_K, tm), tok),
            pl.BlockSpec((N_EXPERTS, LANES), lambda i: (0, 0)),
        ],
        out_shape=[
            jax.ShapeDtypeStruct((TOP_K, t), jnp.int32),
            jax.ShapeDtypeStruct((TOP_K, t), F32),
            jax.ShapeDtypeStruct((N_EXPERTS, LANES), F32),
        ],
        scratch_shapes=[pltpu.VMEM((N_EXPERTS, 1), F32)],
        name="router",
        compiler_params=_params("arbitrary"),
    )(x, w_split, router_bias.reshape(N_EXPERTS, 1))


def _experts_kernel(be_ref, ne_ref, nu_ref, plan_hbm, x_hbm, wg_hbm, wu_hbm, wd_hbm, y_hbm,
                    plan_smem, xbuf0, xbuf1, ybuf0, ybuf1, wg_f32, wu_f32, wd_f32,
                    wg_bf, wu_bf, wd_bf, psem, gsem, ssem, wsem, *, tok_mask, layer):
    b = pl.program_id(0)
    n_used = nu_ref[0]
    rows = MOE_ROWS
    xbuf = (xbuf0, xbuf1)
    ybuf = (ybuf0, ybuf1)

    def weight_copies(expert):
        return [pltpu.make_async_copy(w_hbm.at[layer, expert], stage, wsem.at[i])
                for i, (w_hbm, stage) in enumerate(
                    ((wg_hbm, wg_f32), (wu_hbm, wu_f32), (wd_hbm, wd_f32)))]

    def plan_copy(row, slot):
        return pltpu.make_async_copy(plan_hbm.at[row], plan_smem.at[slot], psem.at[slot])

    def start_gathers(pslot, slot):
        for r in range(rows):
            src = pl.multiple_of(plan_smem[pslot, r] & tok_mask, PACK_SUB)
            pltpu.make_async_copy(x_hbm.at[pl.ds(src, PACK_SUB)],
                                  xbuf[slot].at[pl.ds(r * PACK_SUB, PACK_SUB)],
                                  gsem.at[slot]).start()

    def start_scatters(pslot, slot):
        for r in range(rows):
            dst = pl.multiple_of(plan_smem[pslot, rows + r], PACK_SUB)
            pltpu.make_async_copy(ybuf[slot].at[pl.ds(r * PACK_SUB, PACK_SUB)],
                                  y_hbm.at[pl.ds(dst, PACK_SUB)],
                                  ssem.at[slot]).start(priority=1)

    def wait_rows(buf, sem):
        pltpu.make_async_copy(buf, buf, sem).wait()

    @pl.when(b == 0)
    def _():
        ybuf0[...] = jnp.zeros_like(ybuf0)
        ybuf1[...] = jnp.zeros_like(ybuf1)
        spare0 = y_hbm.shape[0] - 2 * rows * PACK_SUB
        pltpu.make_async_copy(ybuf0, y_hbm.at[pl.ds(spare0, rows * PACK_SUB)], ssem.at[0]).start()
        plan_copy(0, 0).start()
        for copy in weight_copies(be_ref[0]):
            copy.start()
        plan_copy(0, 0).wait()
        start_gathers(0, 0)
        plan_copy(1, 1).start()

    def step(slot):
        other = 1 - slot
        changed = jnp.logical_or(b == 0, be_ref[b] != be_ref[jnp.maximum(b - 1, 0)])

        @pl.when(changed)
        def _():
            for copy in weight_copies(be_ref[b]):
                copy.wait()
            wg_bf[...] = wg_f32[...].astype(BF16)
            wu_bf[...] = wu_f32[...].astype(BF16)
            wd_bf[...] = wd_f32[...].astype(BF16)
            upcoming = ne_ref[b]

            @pl.when(upcoming >= 0)
            def _():
                for copy in weight_copies(upcoming):
                    copy.start()

        wait_rows(xbuf[slot], gsem.at[slot])
        wait_rows(ybuf[slot], ssem.at[slot])
        plan_copy(b + 1, other).wait()
        start_gathers(other, other)
        start_scatters(other, other)
        plan_copy(b + 2, slot).start()

        x = jnp.concatenate([p.astype(BF16) for p in _load_packed_rows(xbuf[slot], rows)], axis=1)
        gate = jnp.dot(x, wg_bf[...], preferred_element_type=F32)
        up = jnp.dot(x, wu_bf[...], preferred_element_type=F32)
        h = (_silu(gate) * up).astype(BF16)
        _store_packed_rows(ybuf[slot], jnp.dot(h, wd_bf[...], preferred_element_type=F32))

        @pl.when(b == n_used - 1)
        def _():
            plan_copy(b + 2, slot).wait()
            start_scatters(slot, slot)
            wait_rows(xbuf[other], gsem.at[other])
            wait_rows(ybuf[other], ssem.at[other])
            wait_rows(ybuf[slot], ssem.at[slot])

    for parity in range(2):
        pl.when(jnp.logical_and(b < n_used, b % 2 == parity))(functools.partial(step, parity))


def _experts(x_packed, layer, w_gate, w_up, w_down, block_expert, next_expert, n_used, plan, *,
             n_out_rows):
    t = x_packed.shape[0] // PACK_SUB
    d = D_MODEL
    assert t & (t - 1) == 0, "token id is taken as the low bits of the flat (choice, token) index"
    n_blocks = plan.shape[0] - 2
    row_buf = pltpu.VMEM((MOE_ROWS * PACK_SUB, LANES), jnp.int32)
    grid_spec = pltpu.PrefetchScalarGridSpec(
        num_scalar_prefetch=3,
        grid=(n_blocks,),
        in_specs=[pl.BlockSpec(memory_space=pl.ANY)] * 5,
        out_specs=pl.BlockSpec(memory_space=pl.ANY),
        scratch_shapes=[
            pltpu.SMEM((2, 2 * MOE_ROWS), jnp.int32),
            row_buf, row_buf, row_buf, row_buf,
            pltpu.VMEM((d, EXPERT_FF), F32),
            pltpu.VMEM((d, EXPERT_FF), F32),
            pltpu.VMEM((EXPERT_FF, d), F32),
            pltpu.VMEM((d, EXPERT_FF), BF16),
            pltpu.VMEM((d, EXPERT_FF), BF16),
            pltpu.VMEM((EXPERT_FF, d), BF16),
            pltpu.SemaphoreType.DMA((2,)),
            pltpu.SemaphoreType.DMA((2,)),
            pltpu.SemaphoreType.DMA((2,)),
            pltpu.SemaphoreType.DMA((3,)),
        ],
    )
    return pl.pallas_call(
        functools.partial(_experts_kernel, tok_mask=(t - 1) * PACK_SUB, layer=layer),
        grid_spec=grid_spec,
        out_shape=jax.ShapeDtypeStruct((n_out_rows * PACK_SUB, LANES), jnp.int32),
        name="routed_experts",
        compiler_params=_params("arbitrary"),
    )(block_expert, next_expert, n_used, plan, x_packed, w_gate, w_up, w_down)


def _combine_kernel(x_ref, wg_ref, wu_ref, wd_ref, w_ref, *refs, tm):
    y_refs = refs[:TOP_K]
    g_ref, beta_ref, o_ref = refs[TOP_K:]
    x = x_ref[...]
    xb = x.astype(BF16)
    gate = jnp.dot(xb, wg_ref[...], preferred_element_type=F32)
    up = jnp.dot(xb, wu_ref[...], preferred_element_type=F32)
    h = (_silu(gate) * up).astype(BF16)
    s = DN_ALPHA * x + jnp.dot(h, wd_ref[...], preferred_element_type=F32)
    pieces = [s[:, c * LANES:(c + 1) * LANES] for c in range(2 * PACK_SUB)]
    for k in range(TOP_K):
        wk = w_ref[:, k:k + 1]
        yk = _load_packed_rows(y_refs[k], tm)
        pieces = [p + wk * y for p, y in zip(pieces, yk)]
    o_ref[...] = _layer_norm(jnp.concatenate(pieces, axis=1), g_ref[...], beta_ref[...])


def _combine(x, ws_gate, ws_up, ws_down, wts_t, y, g, beta, *, tm):
    t, d = x.shape
    f = ws_gate.shape[1]
    blocks = t // tm
    fixed = lambda i: (0, 0)
    y_specs = [pl.BlockSpec((tm * PACK_SUB, LANES),
                            functools.partial(lambda i, k: (k * blocks + i, 0), k=k))
               for k in range(TOP_K)]
    return pl.pallas_call(
        functools.partial(_combine_kernel, tm=tm),
        grid=(blocks,),
        in_specs=[
            pl.BlockSpec((tm, d), lambda i: (i, 0)),
            pl.BlockSpec((d, f), fixed),
            pl.BlockSpec((d, f), fixed),
            pl.BlockSpec((f, d), fixed),
            pl.BlockSpec((tm, TOP_K), lambda i: (i, 0)),
            *y_specs,
            pl.BlockSpec((1, d), fixed),
            pl.BlockSpec((1, d), fixed),
        ],
        out_specs=pl.BlockSpec((tm, d), lambda i: (i, 0)),
        out_shape=jax.ShapeDtypeStruct((t, d), F32),
        name="moe_combine",
        compiler_params=_params("parallel"),
    )(x, ws_gate, ws_up, ws_down, wts_t, *([y] * TOP_K), g.reshape(1, d), beta.reshape(1, d))


def _dispatch_plan(idx, counts, n_tok):
    n_assign = n_tok * TOP_K
    n_blocks = n_assign // MOE_ROWS + N_EXPERTS
    n_slots = n_blocks * MOE_ROWS
    blocks_per_expert = (counts + MOE_ROWS - 1) // MOE_ROWS
    block_end = jnp.cumsum(blocks_per_expert)
    experts = jnp.arange(N_EXPERTS, dtype=jnp.int32)
    slot = jnp.arange(n_slots, dtype=jnp.int32)
    blocks = jnp.arange(n_blocks, dtype=jnp.int32)
    block_expert = jnp.minimum(
        jnp.sum((block_end[None, :] <= blocks[:, None]).astype(jnp.int32), axis=1), N_EXPERTS - 1)
    payload_bits = (n_assign - 1).bit_length()
    assert N_EXPERTS << payload_bits < 1 << 31
    k_of = lax.broadcasted_iota(jnp.int32, idx.shape, 0)
    t_of = lax.broadcasted_iota(jnp.int32, idx.shape, 1)
    keys = lax.sort(((idx << payload_bits) | (t_of * TOP_K + k_of)).reshape(-1))
    payload = keys & ((1 << payload_bits) - 1)
    flat = (payload % TOP_K) * n_tok + payload // TOP_K
    of_block = lambda per_expert: jnp.sum(
        jnp.where(block_expert[:, None] == experts[None, :], per_expert[None, :], 0), axis=1)
    in_expert = (blocks - of_block(block_end - blocks_per_expert)) * MOE_ROWS
    valid = jnp.where(blocks < block_end[-1], of_block(counts) - in_expert, 0)
    first = jnp.where(valid > 0, of_block(jnp.cumsum(counts) - counts) + in_expert, 0)
    flat_padded = jnp.concatenate([flat, jnp.zeros((MOE_ROWS,), jnp.int32)])
    block_rows = jax.vmap(lambda f: lax.dynamic_slice(flat_padded, (f,), (MOE_ROWS,)))(first)
    spare = (n_assign + ((slot // MOE_ROWS) % 2) * MOE_ROWS + slot % MOE_ROWS).reshape(
        n_blocks, MOE_ROWS)
    lane = jnp.arange(MOE_ROWS, dtype=jnp.int32)
    slot_flat = jnp.where(lane[None, :] < valid[:, None], block_rows, spare)
    rows = slot_flat * PACK_SUB
    pad = jnp.broadcast_to((n_assign + MOE_ROWS + slot[:MOE_ROWS]) * PACK_SUB, (2, MOE_ROWS))
    plan = jnp.concatenate([jnp.concatenate([rows, pad], axis=0),
                            jnp.concatenate([pad, rows], axis=0)], axis=1)
    n_used = block_end[-1:].astype(jnp.int32)
    later_used = (experts[None, :] > experts[:, None]) & (blocks_per_expert[None, :] > 0)
    following = jnp.min(jnp.where(later_used, experts[None, :], N_EXPERTS), axis=1)
    following = jnp.where(following == N_EXPERTS, -1, following)
    next_expert = jnp.sum(jnp.where(block_expert[:, None] == experts[None, :], following, 0),
                          axis=1).astype(jnp.int32)
    return block_expert, next_expert, n_used, plan, n_assign + 2 * MOE_ROWS


def _moe(x, x_packed, layer, w_router, router_bias, w_gate, w_up, w_down, ws_gate, ws_up, ws_down,
         g, beta):
    n_tok = x.shape[0]
    idx, wts, cnt = _router(x, w_router, router_bias, tm=512)
    counts = cnt[:, 0].astype(jnp.int32)
    block_expert, next_expert, n_used, plan, n_out_rows = _dispatch_plan(idx, counts, n_tok)
    y = _experts(x_packed, layer, w_gate, w_up, w_down, block_expert, next_expert, n_used, plan,
                 n_out_rows=n_out_rows)
    return _combine(x, ws_gate.astype(BF16), ws_up.astype(BF16), ws_down.astype(BF16), wts.T, y,
                    g, beta, tm=256)


def kernel(x, a_w_in, a_b_in, a_ln_g, a_ln_b, a_w_s, a_b_s, a_w_out, b_w_qkv, b_b_qkv, b_sinks,
           b_w_o, b_b_o, moe_w_router, moe_router_bias, moe_w_gate, moe_w_up, moe_w_down,
           moe_ws_gate, moe_ws_up, moe_ws_down, norm_g, norm_b):
    bsz, seq_len, d = x.shape
    h = x.reshape(bsz * seq_len, d)
    for i in range(DEPTH):
        j = i // 2
        if i % 2 == 0:
            z = _linear(h, a_w_in[j].astype(BF16), a_b_in[j], act="gelu", tm=1024, tn=1024,
                        out_dtype=BF16)
            causal = jnp.tril(jnp.ones((A_CHUNK, A_CHUNK), dtype=bool))
            w_mix = jnp.where(causal[None], a_w_s[j], 0.0).astype(BF16)
            gated = _gmlp_gate(z, a_ln_g[j], a_ln_b[j], w_mix, a_b_s[j].T, tm=256)
            h, h_packed = _linear_res_ln(gated, a_w_out[j].astype(BF16), jnp.zeros((d,), F32), h,
                                         norm_g[i, 0], norm_b[i, 0], tm=512)
        else:
            q_scale = jnp.where(jnp.arange(QKV_DIM) < Q_DIM, HEAD_DIM ** -0.5, 1.0).astype(F32)
            qkv = _linear(h, (b_w_qkv[j] * q_scale).astype(BF16), b_b_qkv[j] * q_scale, act=None,
                          tm=1024, tn=1280, out_dtype=BF16)
            o = _attention(qkv, b_sinks[j], seq_len=seq_len)
            h, h_packed = _linear_res_ln(o, b_w_o[j].astype(BF16), b_b_o[j], h,
                                         norm_g[i, 0], norm_b[i, 0], tm=512)
        h = _moe(h, h_packed, i, moe_w_router[i], moe_router_bias[i], moe_w_gate, moe_w_up, moe_w_down,
                 moe_ws_gate[i], moe_ws_up[i], moe_ws_down[i], norm_g[i, 1], norm_b[i, 1])
    return h.reshape(bsz, seq_len, d)
```

```python
import functools

import jax
import jax.numpy as jnp
from jax import lax
from jax.experimental import pallas as pl
from jax.experimental.pallas import tpu as pltpu

F32 = jnp.float32
BF16 = jnp.bfloat16

D_MODEL = 2048
DEPTH = 2
A_CHUNK = 128
A_HALF = D_MODEL
A_GROUPS = A_HALF // 128
HEAD_DIM = 64
N_Q_HEADS = D_MODEL // HEAD_DIM
N_KV_HEADS = N_Q_HEADS // 8
Q_PER_KV = N_Q_HEADS // N_KV_HEADS
Q_DIM = N_Q_HEADS * HEAD_DIM
KV_DIM = N_KV_HEADS * HEAD_DIM
QKV_DIM = Q_DIM + 2 * KV_DIM
WINDOW = 128
ATTN_BLOCK = 128
N_EXPERTS = 64
TOP_K = 8
N_GROUPS = 8
GROUP_SIZE = N_EXPERTS // N_GROUPS
TOPK_GROUPS = 4
EXPERT_FF = D_MODEL // 4
ROUTED_SCALE = 2.5
DN_ALPHA = (2 * DEPTH) ** 0.25
LN_EPS = 1e-5

MXU_ROWS = 256
MOE_ROWS = 2 * MXU_ROWS
VMEM_LIMIT = 56 * 1024 * 1024


def _params(*sem):
    return pltpu.CompilerParams(dimension_semantics=sem, vmem_limit_bytes=VMEM_LIMIT)


def _layer_norm(y, g, b):
    mu = jnp.mean(y, axis=-1, keepdims=True)
    d = y - mu
    var = jnp.mean(d * d, axis=-1, keepdims=True)
    return d * lax.rsqrt(var + LN_EPS) * g + b


def _gelu(x):
    return 0.5 * x * (1.0 + lax.erf(x * (2.0 ** -0.5)))


def _silu(x):
    return x * jax.nn.sigmoid(x)


LANES = 128
PACK_SUB = D_MODEL // (2 * LANES)
HIGH_HALF = -65536


def _pack_words(rows_f32):
    bits = lax.bitcast_convert_type(rows_f32.astype(BF16).astype(F32), jnp.int32)
    half = D_MODEL // 2
    return [(bits[:, s * LANES:(s + 1) * LANES] & HIGH_HALF)
            | lax.shift_right_logical(bits[:, half + s * LANES:half + (s + 1) * LANES], 16)
            for s in range(PACK_SUB)]


def _unpack_words(words):
    his = [lax.bitcast_convert_type(w & HIGH_HALF, F32) for w in words]
    los = [lax.bitcast_convert_type(lax.shift_left(w, 16), F32) for w in words]
    return his + los


def _store_packed_rows(p_ref, rows_f32):
    n = rows_f32.shape[0]
    for s, word in enumerate(_pack_words(rows_f32)):
        p_ref[pl.ds(s, n, stride=PACK_SUB), :] = word


def _load_packed_rows(p_ref, n):
    return _unpack_words([p_ref[pl.ds(s, n, stride=PACK_SUB), :] for s in range(PACK_SUB)])


def _linear_kernel(x_ref, w_ref, b_ref, o_ref, *, act):
    acc = jnp.dot(x_ref[...].astype(BF16), w_ref[...], preferred_element_type=F32)
    acc = acc + b_ref[...]
    if act == "gelu":
        acc = _gelu(acc)
    o_ref[...] = acc.astype(o_ref.dtype)


def _linear(x, w, b, *, act, tm, tn, out_dtype):
    m, k = x.shape
    n = w.shape[1]
    return pl.pallas_call(
        functools.partial(_linear_kernel, act=act),
        grid=(m // tm, n // tn),
        in_specs=[
            pl.BlockSpec((tm, k), lambda i, j: (i, 0)),
            pl.BlockSpec((k, tn), lambda i, j: (0, j)),
            pl.BlockSpec((1, tn), lambda i, j: (0, j)),
        ],
        out_specs=pl.BlockSpec((tm, tn), lambda i, j: (i, j)),
        out_shape=jax.ShapeDtypeStruct((m, n), out_dtype),
        name="linear_" + str(act),
        compiler_params=_params("parallel", "arbitrary"),
    )(x, w, b.reshape(1, n))


def _linear_res_ln_kernel(x_ref, w_ref, b_ref, res_ref, g_ref, beta_ref, o_ref, p_ref):
    acc = jnp.dot(x_ref[...], w_ref[...], preferred_element_type=F32)
    y = DN_ALPHA * res_ref[...] + (acc + b_ref[...])
    out = _layer_norm(y, g_ref[...], beta_ref[...])
    o_ref[...] = out
    _store_packed_rows(p_ref, out)


def _linear_res_ln(x, w, b, res, g, beta, *, tm):
    m, k = x.shape
    n = w.shape[1]
    row = lambda i: (i, 0)
    fixed = lambda i: (0, 0)
    return pl.pallas_call(
        _linear_res_ln_kernel,
        grid=(m // tm,),
        in_specs=[
            pl.BlockSpec((tm, k), row),
            pl.BlockSpec((k, n), fixed),
            pl.BlockSpec((1, n), fixed),
            pl.BlockSpec((tm, n), row),
            pl.BlockSpec((1, n), fixed),
            pl.BlockSpec((1, n), fixed),
        ],
        out_specs=[pl.BlockSpec((tm, n), row), pl.BlockSpec((tm * PACK_SUB, LANES), row)],
        out_shape=[jax.ShapeDtypeStruct((m, n), F32),
                   jax.ShapeDtypeStruct((m * PACK_SUB, LANES), jnp.int32)],
        name="linear_res_ln",
        compiler_params=_params("parallel"),
    )(x, w, b.reshape(1, n), res, g.reshape(1, n), beta.reshape(1, n))


def _gmlp_gate_kernel(z_ref, lng_ref, lnb_ref, wmix_ref, bst_ref, o_ref, *, tm):
    v = z_ref[:, A_HALF:].astype(F32)
    vn = _layer_norm(v, lng_ref[...], lnb_ref[...]).astype(BF16)
    for c in range(tm // A_CHUNK):
        rows = slice(c * A_CHUNK, (c + 1) * A_CHUNK)
        for g in range(A_GROUPS):
            cols = slice(g * 128, (g + 1) * 128)
            mixed = jnp.dot(wmix_ref[g], vn[rows, cols], preferred_element_type=F32)
            mixed = mixed + bst_ref[:, g:g + 1]
            u = z_ref[rows, cols].astype(F32)
            o_ref[rows, cols] = (u * mixed).astype(o_ref.dtype)


def _gmlp_gate(z, ln_g, ln_b, w_mix, b_s_t, *, tm):
    m = z.shape[0]
    return pl.pallas_call(
        functools.partial(_gmlp_gate_kernel, tm=tm),
        grid=(m // tm,),
        in_specs=[
            pl.BlockSpec((tm, 2 * A_HALF), lambda i: (i, 0)),
            pl.BlockSpec((1, A_HALF), lambda i: (0, 0)),
            pl.BlockSpec((1, A_HALF), lambda i: (0, 0)),
            pl.BlockSpec((A_GROUPS, A_CHUNK, A_CHUNK), lambda i: (0, 0, 0)),
            pl.BlockSpec((A_CHUNK, A_GROUPS), lambda i: (0, 0)),
        ],
        out_specs=pl.BlockSpec((tm, A_HALF), lambda i: (i, 0)),
        out_shape=jax.ShapeDtypeStruct((m, A_HALF), BF16),
        name="gmlp_gate",
        compiler_params=_params("parallel"),
    )(z, ln_g.reshape(1, A_HALF), ln_b.reshape(1, A_HALF), w_mix, b_s_t)


def _attn_kernel(sinks_ref, q_ref, kp_ref, kc_ref, vp_ref, vc_ref, bias_ref, o_ref, *,
                 blocks_per_seq):
    n = pl.program_id(0) % blocks_per_seq
    first = (n == 0).astype(jnp.int32)
    qi = lax.broadcasted_iota(jnp.int32, (ATTN_BLOCK, ATTN_BLOCK), 0)
    ci = lax.broadcasted_iota(jnp.int32, (ATTN_BLOCK, ATTN_BLOCK), 1)
    upper = ci > qi
    low_half = lax.broadcasted_iota(jnp.int32, (ATTN_BLOCK, 2 * HEAD_DIM), 1) < HEAD_DIM
    ones = jnp.ones((2 * ATTN_BLOCK, HEAD_DIM), BF16)
    for g in range(N_KV_HEADS):
        kv_cols = slice(g * HEAD_DIM, (g + 1) * HEAD_DIM)
        kk = jnp.concatenate([kp_ref[:, kv_cols], kc_ref[:, kv_cols]], axis=0)
        vv = jnp.concatenate([vp_ref[:, kv_cols], vc_ref[:, kv_cols]], axis=0)
        vv_ones = jnp.concatenate([vv, ones, ones, vv], axis=1)
        heads = [g * Q_PER_KV + j for j in range(Q_PER_KV)]
        qg = jnp.concatenate([q_ref[:, h * HEAD_DIM:(h + 1) * HEAD_DIM] for h in heads], axis=0)
        s = lax.dot_general(qg, kk, (((1,), (1,)), ((), ())), preferred_element_type=F32)
        es, maxes = [], []
        for j, h in enumerate(heads):
            sj = s[j * ATTN_BLOCK:(j + 1) * ATTN_BLOCK]
            sm = jnp.where(upper, sj[:, :ATTN_BLOCK], sj[:, ATTN_BLOCK:]) + bias_ref[first, h]
            mx = jnp.maximum(jnp.max(sm, axis=-1, keepdims=True), sinks_ref[h])
            e = jnp.exp(sm - mx)
            es.append(jnp.concatenate([jnp.where(upper, e, 0.0), jnp.where(upper, 0.0, e)],
                                      axis=1).astype(BF16))
            maxes.append(mx)
        o = jnp.dot(jnp.concatenate(es, axis=0), vv_ones, preferred_element_type=F32)
        for j in range(0, Q_PER_KV, 2):
            even = o[j * ATTN_BLOCK:(j + 1) * ATTN_BLOCK]
            odd = o[(j + 1) * ATTN_BLOCK:(j + 2) * ATTN_BLOCK]
            num = jnp.where(low_half, even[:, :2 * HEAD_DIM], odd[:, 2 * HEAD_DIM:])
            total = jnp.where(low_half, even[:, 2 * HEAD_DIM:], odd[:, :2 * HEAD_DIM])
            sink_e = jnp.where(low_half, jnp.exp(sinks_ref[heads[j]] - maxes[j]),
                               jnp.exp(sinks_ref[heads[j + 1]] - maxes[j + 1]))
            cols = slice(heads[j] * HEAD_DIM, (heads[j] + 2) * HEAD_DIM)
            o_ref[:, cols] = (num * (1.0 / (total + sink_e))).astype(o_ref.dtype)


def _attention_bias():
    qi = jnp.arange(ATTN_BLOCK, dtype=jnp.int32)[:, None]
    kc = jnp.arange(ATTN_BLOCK, dtype=jnp.int32)[None, :]
    upper = kc > qi
    dist = (qi - kc + jnp.where(upper, ATTN_BLOCK, 0)).astype(F32)
    head = jnp.arange(1, N_Q_HEADS + 1, dtype=F32)
    slopes = jnp.exp2(-8.0 * head / N_Q_HEADS)
    alibi = -slopes[:, None, None] * dist[None]
    return jnp.stack([alibi, jnp.where(upper[None], -jnp.inf, alibi)])


def _attention(qkv, sinks, *, seq_len):
    assert WINDOW == ATTN_BLOCK
    t = qkv.shape[0]
    blocks_per_seq = seq_len // ATTN_BLOCK
    kcol = Q_DIM // KV_DIM
    vcol = kcol + 1
    prev = lambda i: jnp.maximum(i - 1, 0)
    grid_spec = pltpu.PrefetchScalarGridSpec(
        num_scalar_prefetch=1,
        grid=(t // ATTN_BLOCK,),
        in_specs=[
            pl.BlockSpec((ATTN_BLOCK, Q_DIM), lambda i, s: (i, 0)),
            pl.BlockSpec((ATTN_BLOCK, KV_DIM), lambda i, s: (prev(i), kcol)),
            pl.BlockSpec((ATTN_BLOCK, KV_DIM), lambda i, s: (i, kcol)),
            pl.BlockSpec((ATTN_BLOCK, KV_DIM), lambda i, s: (prev(i), vcol)),
            pl.BlockSpec((ATTN_BLOCK, KV_DIM), lambda i, s: (i, vcol)),
            pl.BlockSpec((2, N_Q_HEADS, ATTN_BLOCK, ATTN_BLOCK), lambda i, s: (0, 0, 0, 0)),
        ],
        out_specs=pl.BlockSpec((ATTN_BLOCK, Q_DIM), lambda i, s: (i, 0)),
    )
    return pl.pallas_call(
        functools.partial(_attn_kernel, blocks_per_seq=blocks_per_seq),
        grid_spec=grid_spec,
        out_shape=jax.ShapeDtypeStruct((t, Q_DIM), BF16),
        name="swa_attention",
        compiler_params=_params("parallel"),
    )(sinks, qkv, qkv, qkv, qkv, qkv, _attention_bias())


def _router_kernel(x_ref, wr_ref, rb_ref, idx_ref, wts_ref, cnt_ref, carry_ref, *, tm):
    @pl.when(pl.program_id(0) == 0)
    def _():
        carry_ref[...] = jnp.zeros_like(carry_ref)

    x = x_ref[...]
    x_hi = x.astype(BF16)
    x_lo = (x - x_hi.astype(F32)).astype(BF16)
    w_split = wr_ref[...]
    by_hi = jnp.dot(x_hi, w_split, preferred_element_type=F32)
    by_lo = jnp.dot(x_lo, w_split[:, :LANES], preferred_element_type=F32)
    logits = by_hi[:, :LANES] + (by_hi[:, LANES:] + by_lo)
    sc = jax.nn.sigmoid(logits.T[:N_EXPERTS])
    sel = sc + rb_ref[...]
    neg = -jnp.inf
    iota_g = lax.broadcasted_iota(jnp.int32, (GROUP_SIZE, tm), 0).astype(F32)

    def first_argmax(v, iota, size):
        m = jnp.max(v, axis=0, keepdims=True)
        return jnp.min(jnp.where(v == m, iota, float(size)), axis=0, keepdims=True)

    group_rows = []
    for g in range(N_GROUPS):
        v = sel[g * GROUP_SIZE:(g + 1) * GROUP_SIZE]
        m1 = jnp.max(v, axis=0, keepdims=True)
        i1 = first_argmax(v, iota_g, GROUP_SIZE)
        m2 = jnp.max(jnp.where(iota_g == i1, neg, v), axis=0, keepdims=True)
        group_rows.append(m1 + m2)
    cur = jnp.concatenate(group_rows, axis=0)
    iota_n = lax.broadcasted_iota(jnp.int32, (N_GROUPS, tm), 0).astype(F32)
    gsel = jnp.zeros((N_GROUPS, tm), F32)
    for _ in range(TOPK_GROUPS):
        hit = iota_n == first_argmax(cur, iota_n, N_GROUPS)
        gsel = jnp.where(hit, 1.0, gsel)
        cur = jnp.where(hit, neg, cur)
    cur = jnp.concatenate(
        [jnp.where(gsel[g:g + 1] > 0.5, sel[g * GROUP_SIZE:(g + 1) * GROUP_SIZE], neg)
         for g in range(N_GROUPS)], axis=0)

    iota_e = lax.broadcasted_iota(jnp.int32, (N_EXPERTS, tm), 0).astype(F32)
    member = jnp.zeros((N_EXPERTS, tm), F32)
    idx_rows, w_rows = [], []
    for _ in range(TOP_K):
        ii = first_argmax(cur, iota_e, N_EXPERTS)
        hit = iota_e == ii
        idx_rows.append(ii)
        w_rows.append(jnp.sum(jnp.where(hit, sc, 0.0), axis=0, keepdims=True))
        member = jnp.where(hit, 1.0, member)
        cur = jnp.where(hit, neg, cur)
    w = jnp.concatenate(w_rows, axis=0)
    w = w / (jnp.sum(w, axis=0, keepdims=True) + 1e-20) * ROUTED_SCALE

    idx_ref[...] = jnp.concatenate(idx_rows, axis=0).astype(jnp.int32)
    wts_ref[...] = w
    carry_ref[...] = carry_ref[...] + jnp.sum(member, axis=1, keepdims=True)
    cnt_ref[...] = jnp.broadcast_to(carry_ref[...], cnt_ref.shape)


def _router(x, w_router, router_bias, *, tm):
    t, d = x.shape
    tok = lambda i: (0, i)
    w_padded = jnp.pad(w_router, ((0, 0), (0, LANES - N_EXPERTS)))
    w_hi = w_padded.astype(BF16)
    w_lo = (w_padded - w_hi.astype(F32)).astype(BF16)
    w_split = jnp.concatenate([w_hi, w_lo], axis=1)
    return pl.pallas_call(
        functools.partial(_router_kernel, tm=tm),
        grid=(t // tm,),
        in_specs=[
            pl.BlockSpec((tm, d), lambda i: (i, 0)),
            pl.BlockSpec((d, 2 * LANES), lambda i: (0, 0)),
            pl.BlockSpec((N_EXPERTS, 1), lambda i: (0, 0)),
        ],
        out_specs=[
            pl.BlockSpec((TOP_K, tm), tok),
            pl.BlockSpec((TOP_K, tm), tok),
            pl.BlockSpec((N_EXPERTS, LANES), lambda i: (0, 0)),
        ],
        out_shape=[
            jax.ShapeDtypeStruct((TOP_K, t), jnp.int32),
            jax.ShapeDtypeStruct((TOP_K, t), F32),
            jax.ShapeDtypeStruct((N_EXPERTS, LANES), F32),
        ],
        scratch_shapes=[pltpu.VMEM((N_EXPERTS, 1), F32)],
        name="router",
        compiler_params=_params("arbitrary"),
    )(x, w_split, router_bias.reshape(N_EXPERTS, 1))


def _experts_kernel(be_ref, ne_ref, vr_ref, nu_ref, plan_hbm, x_hbm, wg_hbm, wu_hbm, wd_hbm, y_hbm,
                    plan_smem, xbuf0, xbuf1, ybuf0, ybuf1, wg_f32, wu_f32, wd_f32,
                    wg_bf, wu_bf, wd_bf, psem, gsem, ssem, wsem, *, tok_mask, layer):
    b = pl.program_id(0)
    n_used = nu_ref[0]
    rows = MOE_ROWS
    xbuf = (xbuf0, xbuf1)
    ybuf = (ybuf0, ybuf1)

    def weight_copies(expert):
        return [pltpu.make_async_copy(w_hbm.at[layer, expert], stage, wsem.at[i])
                for i, (w_hbm, stage) in enumerate(
                    ((wg_hbm, wg_f32), (wu_hbm, wu_f32), (wd_hbm, wd_f32)))]

    def plan_copy(row, slot):
        return pltpu.make_async_copy(plan_hbm.at[row], plan_smem.at[slot], psem.at[slot])

    def start_gathers(pslot, slot):
        for r in range(rows):
            src = pl.multiple_of(plan_smem[pslot, r] & tok_mask, PACK_SUB)
            pltpu.make_async_copy(x_hbm.at[pl.ds(src, PACK_SUB)],
                                  xbuf[slot].at[pl.ds(r * PACK_SUB, PACK_SUB)],
                                  gsem.at[slot]).start()

    def start_scatters(pslot, slot):
        for r in range(rows):
            dst = pl.multiple_of(plan_smem[pslot, rows + r], PACK_SUB)
            pltpu.make_async_copy(ybuf[slot].at[pl.ds(r * PACK_SUB, PACK_SUB)],
                                  y_hbm.at[pl.ds(dst, PACK_SUB)],
                                  ssem.at[slot]).start(priority=1)

    def wait_rows(buf, sem):
        pltpu.make_async_copy(buf, buf, sem).wait()

    @pl.when(b == 0)
    def _():
        ybuf0[...] = jnp.zeros_like(ybuf0)
        ybuf1[...] = jnp.zeros_like(ybuf1)
        spare0 = y_hbm.shape[0] - 2 * rows * PACK_SUB
        pltpu.make_async_copy(ybuf0, y_hbm.at[pl.ds(spare0, rows * PACK_SUB)], ssem.at[0]).start()
        plan_copy(0, 0).start()
        for copy in weight_copies(be_ref[0]):
            copy.start()
        plan_copy(0, 0).wait()
        start_gathers(0, 0)
        plan_copy(1, 1).start()

    def step(slot):
        other = 1 - slot
        changed = jnp.logical_or(b == 0, be_ref[b] != be_ref[jnp.maximum(b - 1, 0)])

        @pl.when(changed)
        def _():
            for copy in weight_copies(be_ref[b]):
                copy.wait()
            wg_bf[...] = wg_f32[...].astype(BF16)
            wu_bf[...] = wu_f32[...].astype(BF16)
            wd_bf[...] = wd_f32[...].astype(BF16)
            upcoming = ne_ref[b]

            @pl.when(upcoming >= 0)
            def _():
                for copy in weight_copies(upcoming):
                    copy.start()

        wait_rows(xbuf[slot], gsem.at[slot])
        wait_rows(ybuf[slot], ssem.at[slot])
        plan_copy(b + 1, other).wait()

        def issue_and_compute(n):
            start_gathers(other, other)
            start_scatters(other, other)
            plan_copy(b + 2, slot).start()
            pieces = _load_packed_rows(xbuf[slot], n)
            x = jnp.concatenate([p.astype(BF16) for p in pieces], axis=1)
            gate = jnp.dot(x, wg_bf[...], preferred_element_type=F32)
            up = jnp.dot(x, wu_bf[...], preferred_element_type=F32)
            h = (_silu(gate) * up).astype(BF16)
            _store_packed_rows(ybuf[slot], jnp.dot(h, wd_bf[...], preferred_element_type=F32))

        few = vr_ref[b] <= MXU_ROWS
        pl.when(jnp.logical_not(few))(functools.partial(issue_and_compute, rows))
        pl.when(few)(functools.partial(issue_and_compute, MXU_ROWS))

        @pl.when(b == n_used - 1)
        def _():
            plan_copy(b + 2, slot).wait()
            start_scatters(slot, slot)
            wait_rows(xbuf[other], gsem.at[other])
            wait_rows(ybuf[other], ssem.at[other])
            wait_rows(ybuf[slot], ssem.at[slot])

    for parity in range(2):
        pl.when(jnp.logical_and(b < n_used, b % 2 == parity))(functools.partial(step, parity))


def _experts(x_packed, layer, w_gate, w_up, w_down, block_expert, next_expert, block_valid, n_used,
             plan, *,
             n_out_rows):
    t = x_packed.shape[0] // PACK_SUB
    d = D_MODEL
    assert t & (t - 1) == 0, "token id is taken as the low bits of the flat (choice, token) index"
    n_blocks = plan.shape[0] - 2
    row_buf = pltpu.VMEM((MOE_ROWS * PACK_SUB, LANES), jnp.int32)
    grid_spec = pltpu.PrefetchScalarGridSpec(
        num_scalar_prefetch=4,
        grid=(n_blocks,),
        in_specs=[pl.BlockSpec(memory_space=pl.ANY)] * 5,
        out_specs=pl.BlockSpec(memory_space=pl.ANY),
        scratch_shapes=[
            pltpu.SMEM((2, 2 * MOE_ROWS), jnp.int32),
            row_buf, row_buf, row_buf, row_buf,
            pltpu.VMEM((d, EXPERT_FF), F32),
            pltpu.VMEM((d, EXPERT_FF), F32),
            pltpu.VMEM((EXPERT_FF, d), F32),
            pltpu.VMEM((d, EXPERT_FF), BF16),
            pltpu.VMEM((d, EXPERT_FF), BF16),
            pltpu.VMEM((EXPERT_FF, d), BF16),
            pltpu.SemaphoreType.DMA((2,)),
            pltpu.SemaphoreType.DMA((2,)),
            pltpu.SemaphoreType.DMA((2,)),
            pltpu.SemaphoreType.DMA((3,)),
        ],
    )
    return pl.pallas_call(
        functools.partial(_experts_kernel, tok_mask=(t - 1) * PACK_SUB, layer=layer),
        grid_spec=grid_spec,
        out_shape=jax.ShapeDtypeStruct((n_out_rows * PACK_SUB, LANES), jnp.int32),
        name="routed_experts",
        compiler_params=_params("arbitrary"),
    )(block_expert, next_expert, block_valid, n_used, plan, x_packed, w_gate, w_up, w_down)


def _combine_kernel(x_ref, wg_ref, wu_ref, wd_ref, w_ref, *refs, tm):
    y_refs = refs[:TOP_K]
    g_ref, beta_ref, o_ref = refs[TOP_K:]
    x = x_ref[...]
    xb = x.astype(BF16)
    gate = jnp.dot(xb, wg_ref[...], preferred_element_type=F32)
    up = jnp.dot(xb, wu_ref[...], preferred_element_type=F32)
    h = (_silu(gate) * up).astype(BF16)
    s = DN_ALPHA * x + jnp.dot(h, wd_ref[...], preferred_element_type=F32)
    pieces = [s[:, c * LANES:(c + 1) * LANES] for c in range(2 * PACK_SUB)]
    for k in range(TOP_K):
        wk = w_ref[:, k:k + 1]
        yk = _load_packed_rows(y_refs[k], tm)
        pieces = [p + wk * y for p, y in zip(pieces, yk)]
    o_ref[...] = _layer_norm(jnp.concatenate(pieces, axis=1), g_ref[...], beta_ref[...])


def _combine(x, ws_gate, ws_up, ws_down, wts_t, y, g, beta, *, tm):
    t, d = x.shape
    f = ws_gate.shape[1]
    blocks = t // tm
    fixed = lambda i: (0, 0)
    y_specs = [pl.BlockSpec((tm * PACK_SUB, LANES),
                            functools.partial(lambda i, k: (k * blocks + i, 0), k=k))
               for k in range(TOP_K)]
    return pl.pallas_call(
        functools.partial(_combine_kernel, tm=tm),
        grid=(blocks,),
        in_specs=[
            pl.BlockSpec((tm, d), lambda i: (i, 0)),
            pl.BlockSpec((d, f), fixed),
            pl.BlockSpec((d, f), fixed),
            pl.BlockSpec((f, d), fixed),
            pl.BlockSpec((tm, TOP_K), lambda i: (i, 0)),
            *y_specs,
            pl.BlockSpec((1, d), fixed),
            pl.BlockSpec((1, d), fixed),
        ],
        out_specs=pl.BlockSpec((tm, d), lambda i: (i, 0)),
        out_shape=jax.ShapeDtypeStruct((t, d), F32),
        name="moe_combine",
        compiler_params=_params("parallel"),
    )(x, ws_gate, ws_up, ws_down, wts_t, *([y] * TOP_K), g.reshape(1, d), beta.reshape(1, d))


def _dispatch_plan(idx, counts, n_tok):
    n_assign = n_tok * TOP_K
    n_blocks = n_assign // MOE_ROWS + N_EXPERTS
    n_slots = n_blocks * MOE_ROWS
    blocks_per_expert = (counts + MOE_ROWS - 1) // MOE_ROWS
    block_end = jnp.cumsum(blocks_per_expert)
    experts = jnp.arange(N_EXPERTS, dtype=jnp.int32)
    slot = jnp.arange(n_slots, dtype=jnp.int32)
    payload_bits = (n_assign - 1).bit_length()
    n_pad = n_slots - n_assign
    assert n_pad <= 1 << payload_bits and (N_EXPERTS + 1) << (payload_bits + 1) < 1 << 31
    k_of = lax.broadcasted_iota(jnp.int32, idx.shape, 0)
    t_of = lax.broadcasted_iota(jnp.int32, idx.shape, 1)
    real_keys = (idx << (payload_bits + 1)) | (t_of * TOP_K + k_of)
    pad_counts = jnp.concatenate([blocks_per_expert * MOE_ROWS - counts,
                                  n_slots - block_end[-1:] * MOE_ROWS])
    pad_end = jnp.cumsum(pad_counts)
    j = jnp.arange(n_pad, dtype=jnp.int32)
    pad_owner = jnp.sum((pad_end[None, :] <= j[:, None]).astype(jnp.int32), axis=1)
    pad_keys = (pad_owner << (payload_bits + 1)) | (1 << payload_bits) | j
    keys = lax.sort(jnp.concatenate([real_keys.reshape(-1), pad_keys]))
    payload = keys & ((1 << payload_bits) - 1)
    flat = (payload % TOP_K) * n_tok + payload // TOP_K
    spare = n_assign + ((slot // MOE_ROWS) % 2) * MOE_ROWS + slot % MOE_ROWS
    is_pad = (keys >> payload_bits) & 1
    slot_flat = jnp.where(is_pad == 1, spare, flat).reshape(n_blocks, MOE_ROWS)
    rows = slot_flat * PACK_SUB
    pad = jnp.broadcast_to((n_assign + MOE_ROWS + slot[:MOE_ROWS]) * PACK_SUB, (2, MOE_ROWS))
    plan = jnp.concatenate([jnp.concatenate([rows, pad], axis=0),
                            jnp.concatenate([pad, rows], axis=0)], axis=1)
    blocks = jnp.arange(n_blocks, dtype=jnp.int32)
    block_expert = jnp.minimum(
        jnp.sum((block_end[None, :] <= blocks[:, None]).astype(jnp.int32), axis=1), N_EXPERTS - 1)
    n_used = block_end[-1:].astype(jnp.int32)
    later_used = (experts[None, :] > experts[:, None]) & (blocks_per_expert[None, :] > 0)
    following = jnp.min(jnp.where(later_used, experts[None, :], N_EXPERTS), axis=1)
    following = jnp.where(following == N_EXPERTS, -1, following)
    next_expert = jnp.sum(jnp.where(block_expert[:, None] == experts[None, :], following, 0),
                          axis=1).astype(jnp.int32)
    of_block = lambda per_expert: jnp.sum(
        jnp.where(block_expert[:, None] == experts[None, :], per_expert[None, :], 0), axis=1)
    rows_before = (blocks - of_block(block_end - blocks_per_expert)) * MOE_ROWS
    block_valid = jnp.clip(of_block(counts) - rows_before, 0, MOE_ROWS).astype(jnp.int32)
    return block_expert, next_expert, block_valid, n_used, plan, n_assign + 2 * MOE_ROWS


def _moe(x, x_packed, layer, w_router, router_bias, w_gate, w_up, w_down, ws_gate, ws_up, ws_down,
         g, beta):
    n_tok = x.shape[0]
    idx, wts, cnt = _router(x, w_router, router_bias, tm=512)
    counts = cnt[:, 0].astype(jnp.int32)
    block_expert, next_expert, block_valid, n_used, plan, n_out_rows = _dispatch_plan(
        idx, counts, n_tok)
    y = _experts(x_packed, layer, w_gate, w_up, w_down, block_expert, next_expert, block_valid,
                 n_used, plan, n_out_rows=n_out_rows)
    return _combine(x, ws_gate.astype(BF16), ws_up.astype(BF16), ws_down.astype(BF16), wts.T, y,
                    g, beta, tm=256)


def kernel(x, a_w_in, a_b_in, a_ln_g, a_ln_b, a_w_s, a_b_s, a_w_out, b_w_qkv, b_b_qkv, b_sinks,
           b_w_o, b_b_o, moe_w_router, moe_router_bias, moe_w_gate, moe_w_up, moe_w_down,
           moe_ws_gate, moe_ws_up, moe_ws_down, norm_g, norm_b):
    bsz, seq_len, d = x.shape
    h = x.reshape(bsz * seq_len, d)
    for i in range(DEPTH):
        j = i // 2
        if i % 2 == 0:
            z = _linear(h, a_w_in[j].astype(BF16), a_b_in[j], act="gelu", tm=1024, tn=1024,
                        out_dtype=BF16)
            causal = jnp.tril(jnp.ones((A_CHUNK, A_CHUNK), dtype=bool))
            w_mix = jnp.where(causal[None], a_w_s[j], 0.0).astype(BF16)
            gated = _gmlp_gate(z, a_ln_g[j], a_ln_b[j], w_mix, a_b_s[j].T, tm=256)
            h, h_packed = _linear_res_ln(gated, a_w_out[j].astype(BF16), jnp.zeros((d,), F32), h,
                                         norm_g[i, 0], norm_b[i, 0], tm=512)
        else:
            q_scale = jnp.where(jnp.arange(QKV_DIM) < Q_DIM, HEAD_DIM ** -0.5, 1.0).astype(F32)
            qkv = _linear(h, (b_w_qkv[j] * q_scale).astype(BF16), b_b_qkv[j] * q_scale, act=None,
                          tm=1024, tn=1280, out_dtype=BF16)
            o = _attention(qkv, b_sinks[j], seq_len=seq_len)
            h, h_packed = _linear_res_ln(o, b_w_o[j].astype(BF16), b_b_o[j], h,
                                         norm_g[i, 0], norm_b[i, 0], tm=512)
        h = _moe(h, h_packed, i, moe_w_router[i], moe_router_bias[i], moe_w_gate, moe_w_up, moe_w_down,
                 moe_ws_gate[i], moe_ws_up[i], moe_ws_down[i], norm_g[i, 1], norm_b[i, 1])
    return h.reshape(bsz, seq_len, d)
```
